```python
import jax, jax.numpy as jnp
from jax import lax
import numpy as np

D_MODEL = 1024
BATCH = 16
SEQ = 4096
DEPTH = 1

N_META = 16
HEAD_DIM = 64
FOX_HEADS = 8
FOX_WIDTH = FOX_HEADS * HEAD_DIM
RWKV_HEADS = 8
RWKV_WIDTH = RWKV_HEADS * HEAD_DIM
DECAY_LORA = 64
AAA_LORA = 64
GATE_LORA = 160
ATTN_BLOCK = 128
PEER_HEADS = 8
PEER_QDIM = 256
N_KEYS = 128
N_EXPERTS = N_KEYS * N_KEYS
PEER_TOPK = 16
PEER_CHUNK = 128
NORM_EPS = 1e-6
RWKV_GN_EPS = 64e-5
MASK_VALUE = -1e30

FOX_COLS = 3 * FOX_WIDTH + FOX_HEADS
RWKV_COLS = 3 * RWKV_WIDTH + DECAY_LORA + AAA_LORA + GATE_LORA
GATE_COLS = 2 * D_MODEL
IN_COLS = FOX_COLS + RWKV_COLS + GATE_COLS

kernel_name = 'fox_rwkv7_peer_hybrid_block'


def _split(z, sizes):
    idx = np.cumsum(sizes)[:-1].tolist()
    return jnp.split(z, idx, axis=-1)


def rmsnorm(x, g):
    xf = x.astype(jnp.float32)
    y = xf * lax.rsqrt(jnp.mean(xf * xf, axis=-1, keepdims=True) + NORM_EPS)
    return (y * g.astype(jnp.float32)).astype(x.dtype)


def fox_attention(q, k, v, log_f):
    B, T, H, Dh = q.shape
    pad = (-T) % ATTN_BLOCK
    L = T + pad
    nb = L // ATTN_BLOCK
    pad4 = ((0, 0), (pad, 0), (0, 0), (0, 0))
    qp = jnp.pad(q, pad4)
    k32 = jnp.pad(k, pad4).astype(jnp.float32)
    vp = jnp.pad(v, pad4)
    c = jnp.cumsum(jnp.pad(log_f, ((0, 0), (pad, 0), (0, 0))), axis=1)
    c_k = jnp.transpose(c, (0, 2, 1))
    kpos = jnp.arange(L)
    k_valid = kpos >= pad
    q_blocks = jnp.moveaxis(qp.reshape(B, nb, ATTN_BLOCK, H, Dh), 1, 0)
    c_blocks = jnp.moveaxis(c.reshape(B, nb, ATTN_BLOCK, H), 1, 0)
    scale = Dh ** -0.5

    def one_block(args):
        i, q_i, c_i = args
        s = jnp.einsum('bqhd,bkhd->bhqk', q_i.astype(jnp.float32), k32) * scale
        s = s + jnp.transpose(c_i, (0, 2, 1))[..., :, None] - c_k[:, :, None, :]
        qpos = i * ATTN_BLOCK + jnp.arange(ATTN_BLOCK)
        mask = (kpos[None, :] <= qpos[:, None]) & k_valid[None, :]
        p = jax.nn.softmax(jnp.where(mask, s, MASK_VALUE), axis=-1)
        return jnp.einsum('bhqk,bkhd->bqhd', p.astype(vp.dtype), vp)

    o = lax.map(one_block, (jnp.arange(nb), q_blocks, c_blocks))
    o = jnp.moveaxis(o, 0, 1).reshape(B, L, H, Dh)
    return o[:, pad:]


def rwkv7_time_mix(z, mu, w0, w_up, a0, a_up, g_up, k_k, k_a, r_k, ln_g, ln_b):
    B, T, _ = z.shape
    f32 = jnp.float32
    z_prev = jnp.pad(z, ((0, 0), (1, 0), (0, 0)))[:, :-1]
    z = z + mu * (z_prev - z)
    r, k, v, wd, ad, gd = _split(z, [RWKV_WIDTH] * 3 + [DECAY_LORA, AAA_LORA, GATE_LORA])
    w = -jax.nn.softplus(-(w0 + jnp.tanh(wd) @ w_up)) - 0.5
    decay = jnp.exp(-jnp.exp(w.astype(f32)))
    a = jax.nn.sigmoid(a0 + ad @ a_up)
    g = jax.nn.sigmoid(gd) @ g_up
    kk = k * k_k
    k = k * (1.0 + (a - 1.0) * k_a)

    def heads(t):
        return t.reshape(B, T, RWKV_HEADS, HEAD_DIM).astype(f32)

    r, k, v, decay, a, kk = (heads(t) for t in (r, k, v, decay, a, kk))
    kk = kk / jnp.maximum(jnp.sqrt(jnp.sum(kk * kk, axis=-1, keepdims=True)), 1e-12)

    def step(S, inp):
        r_t, w_t, k_t, v_t, kk_t, a_t = inp
        sa = jnp.einsum('bhvk,bhk->bhv', S, -kk_t)
        S = (S * w_t[:, :, None, :] + sa[..., None] * (kk_t * a_t)[:, :, None, :]
             + v_t[..., None] * k_t[:, :, None, :])
        return S, jnp.einsum('bhvk,bhk->bhv', S, r_t)

    xs = tuple(jnp.moveaxis(t, 1, 0) for t in (r, decay, k, v, kk, a))
    S0 = jnp.zeros((B, RWKV_HEADS, HEAD_DIM, HEAD_DIM), f32)
    _, y = lax.scan(step, S0, xs)
    y = jnp.moveaxis(y, 0, 1)
    mean = jnp.mean(y, axis=-1, keepdims=True)
    var = jnp.mean(jnp.square(y - mean), axis=-1, keepdims=True)
    y = (y - mean) * lax.rsqrt(var + RWKV_GN_EPS)
    y = y.reshape(B, T, RWKV_WIDTH) * ln_g + ln_b
    bonus = (jnp.sum(r * k * r_k, axis=-1, keepdims=True) * v).reshape(B, T, RWKV_WIDTH)
    return ((y + bonus) * g).astype(z.dtype)


def peer_ffn(xn, w_q, sub_k1, sub_k2, u, v):
    B, T, D = xn.shape
    xc = xn.reshape(-1, PEER_CHUNK, D)
    half = PEER_QDIM // 2

    def one_chunk(x_c):
        q = (x_c @ w_q).reshape(PEER_CHUNK, PEER_HEADS, PEER_QDIM)
        s1 = jnp.einsum('chd,nd->chn', q[..., :half], sub_k1).astype(jnp.float32)
        s2 = jnp.einsum('chd,nd->chn', q[..., half:], sub_k2).astype(jnp.float32)
        t1, i1 = lax.top_k(s1, PEER_TOPK)
        t2, i2 = lax.top_k(s2, PEER_TOPK)
        cand_s = (t1[..., :, None] + t2[..., None, :]).reshape(PEER_CHUNK, PEER_HEADS, PEER_TOPK ** 2)
        cand_i = (i1[..., :, None] * N_KEYS + i2[..., None, :]).reshape(PEER_CHUNK, PEER_HEADS, PEER_TOPK ** 2)
        top_s, pos = lax.top_k(cand_s, PEER_TOPK)
        ids = jnp.take_along_axis(cand_i, pos, axis=-1)
        gate = jax.nn.softmax(top_s, axis=-1)
        hid = jax.nn.gelu(jnp.einsum('chkd,cd->chk', u[ids], x_c).astype(jnp.float32), approximate=False)
        return jnp.einsum('chk,chkd->cd', (gate * hid).astype(v.dtype), v[ids])

    return lax.map(one_chunk, xc).reshape(B, T, D)


def setup_inputs(seed: int = 0) -> dict:
    key = jax.random.key(seed)
    ks = iter(jax.random.split(key, 32))
    L = DEPTH

    def nrm(shape, scale):
        return scale * jax.random.normal(next(ks), shape, jnp.float32)

    def uni(shape, lo, hi):
        return jax.random.uniform(next(ks), shape, jnp.float32, lo, hi)

    return {
        'x': nrm((BATCH, SEQ, D_MODEL), 1.0),
        'meta_tokens': nrm((N_META, D_MODEL), 1.0),
        'norm1_g': 1.0 + nrm((L, D_MODEL), 0.02),
        'w_in': nrm((L, D_MODEL, IN_COLS), D_MODEL ** -0.5),
        'fox_q_norm': 1.0 + nrm((L, FOX_HEADS, HEAD_DIM), 0.02),
        'fox_k_norm': 1.0 + nrm((L, FOX_HEADS, HEAD_DIM), 0.02),
        'fox_f_bias': 4.0 + nrm((L, FOX_HEADS), 0.5),
        'rwkv_mu': uni((L, RWKV_COLS), 0.0, 1.0),
        'rwkv_w0': uni((L, RWKV_WIDTH), -6.0, -1.0),
        'rwkv_w_up': nrm((L, DECAY_LORA, RWKV_WIDTH), 0.1),
        'rwkv_a0': nrm((L, RWKV_WIDTH), 0.1),
        'rwkv_a_up': nrm((L, AAA_LORA, RWKV_WIDTH), 0.5 * AAA_LORA ** -0.5),
        'rwkv_g_up': nrm((L, GATE_LORA, RWKV_WIDTH), GATE_LORA ** -0.5),
        'rwkv_k_k': 0.85 + nrm((L, RWKV_WIDTH), 0.05),
        'rwkv_k_a': 1.0 + nrm((L, RWKV_WIDTH), 0.05),
        'rwkv_r_k': nrm((L, RWKV_HEADS, HEAD_DIM), 0.1),
        'rwkv_ln_g': 1.0 + nrm((L, RWKV_WIDTH), 0.02),
        'rwkv_ln_b': nrm((L, RWKV_WIDTH), 0.02),
        'w_branch_fox': nrm((L, FOX_WIDTH, D_MODEL), FOX_WIDTH ** -0.5),
        'w_branch_rwkv': nrm((L, RWKV_WIDTH, D_MODEL), RWKV_WIDTH ** -0.5),
        'w_out': nrm((L, D_MODEL, D_MODEL), D_MODEL ** -0.5),
        'norm2_g': 1.0 + nrm((L, D_MODEL), 0.02),
        'peer_w_q': nrm((L, D_MODEL, PEER_HEADS * PEER_QDIM), D_MODEL ** -0.5),
        'peer_sub_k1': nrm((L, N_KEYS, PEER_QDIM // 2), (PEER_QDIM // 2) ** -0.5),
        'peer_sub_k2': nrm((L, N_KEYS, PEER_QDIM // 2), (PEER_QDIM // 2) ** -0.5),
        'peer_u': nrm((L, N_EXPERTS, D_MODEL), D_MODEL ** -0.5),
        'peer_v': nrm((L, N_EXPERTS, D_MODEL), 0.5),
    }


def reference(x, meta_tokens, norm1_g, w_in, fox_q_norm, fox_k_norm, fox_f_bias,
              rwkv_mu, rwkv_w0, rwkv_w_up, rwkv_a0, rwkv_a_up, rwkv_g_up, rwkv_k_k,
              rwkv_k_a, rwkv_r_k, rwkv_ln_g, rwkv_ln_b, w_branch_fox, w_branch_rwkv,
              w_out, norm2_g, peer_w_q, peer_sub_k1, peer_sub_k2, peer_u, peer_v):
    B = x.shape[0]
    meta = jnp.broadcast_to(meta_tokens[None].astype(x.dtype), (B, N_META, D_MODEL))
    h = jnp.concatenate([meta, x], axis=1)
    for l in range(DEPTH):
        T = h.shape[1]
        xn = rmsnorm(h, norm1_g[l])
        z = xn @ w_in[l]
        z_fox, z_rwkv, z_gate = _split(z, [FOX_COLS, RWKV_COLS, GATE_COLS])
        fq, fk, fv, f_logit = _split(z_fox, [FOX_WIDTH] * 3 + [FOX_HEADS])
        fq = rmsnorm(fq.reshape(B, T, FOX_HEADS, HEAD_DIM), fox_q_norm[l])
        fk = rmsnorm(fk.reshape(B, T, FOX_HEADS, HEAD_DIM), fox_k_norm[l])
        fv = fv.reshape(B, T, FOX_HEADS, HEAD_DIM)
        log_f = jax.nn.log_sigmoid((f_logit + fox_f_bias[l]).astype(jnp.float32))
        y_fox = fox_attention(fq, fk, fv, log_f).reshape(B, T, FOX_WIDTH)
        y_rwkv = rwkv7_time_mix(z_rwkv, rwkv_mu[l], rwkv_w0[l], rwkv_w_up[l], rwkv_a0[l],
                                rwkv_a_up[l], rwkv_g_up[l], rwkv_k_k[l], rwkv_k_a[l],
                                rwkv_r_k[l], rwkv_ln_g[l], rwkv_ln_b[l])
        gate_fox, gate_rwkv = jnp.split(jax.nn.sigmoid(z_gate), 2, axis=-1)
        mixed = gate_fox * (y_fox @ w_branch_fox[l]) + gate_rwkv * (y_rwkv @ w_branch_rwkv[l])
        h = h + mixed @ w_out[l]
        if l == DEPTH - 1:
            h = h[:, N_META:]
        h = h + peer_ffn(rmsnorm(h, norm2_g[l]), peer_w_q[l], peer_sub_k1[l],
                         peer_sub_k2[l], peer_u[l], peer_v[l])
    return h
```

```python
import functools

import jax
import jax.numpy as jnp
from jax import lax
from jax.experimental import pallas as pl
from jax.experimental.pallas import tpu as pltpu

f32 = jnp.float32
bf16 = jnp.bfloat16
i32 = jnp.int32
HIGHEST = lax.Precision.HIGHEST

N_META = 16
HEAD_DIM = 64
N_HEADS = 8
WIDTH = N_HEADS * HEAD_DIM
DECAY_LORA, AAA_LORA, GATE_LORA = 64, 64, 160
ATTN_BLOCK = 128
PEER_HEADS, PEER_QDIM, N_KEYS, PEER_TOPK = 8, 256, 128, 16
NORM_EPS = 1e-6
RWKV_GN_EPS = 64e-5
MASK_VALUE = -1e30

LANES = 128
VMEM_LIMIT_BYTES = 56 * 1024 * 1024

CHUNK = 64
PAIR = 2 * HEAD_DIM
PEER_TOK = 128
PEER_PAIRS = PEER_HEADS * PEER_TOPK
FOX_COLS = 3 * WIDTH + LANES
RWKV_COLS = 3 * WIDTH + 2 * LANES + 2 * LANES


def _row_block(n):
    for t in (512, 384, 256, 128):
        if n % t == 0:
            return t
    raise ValueError(f"row count {n} is not a multiple of 128")


def _attn_block(length):
    return 384 if length % 384 == 0 else ATTN_BLOCK


def _cparams(*sem):
    return pltpu.CompilerParams(dimension_semantics=sem, vmem_limit_bytes=VMEM_LIMIT_BYTES)


def _const_spec(shape):
    nd = len(shape)
    return pl.BlockSpec(shape, lambda *_: (0,) * nd)


def _nt(a, b, **kw):
    return lax.dot_general(a, b, (((1,), (1,)), ((), ())), preferred_element_type=f32, **kw)


def _tn(a, b, **kw):
    return lax.dot_general(a, b, (((0,), (0,)), ((), ())), preferred_element_type=f32, **kw)


def _mm(a, b, **kw):
    return jnp.dot(a, b, preferred_element_type=f32, **kw)


def _softplus(y):
    return jnp.maximum(y, 0.0) + jnp.log1p(jnp.exp(-jnp.abs(y)))


def _sigmoid(y):
    return 1.0 / (1.0 + jnp.exp(-y))


def _proj_kernel(x_ref, g_ref, w_ref, qn_ref, kn_ref, fb_ref, bd_ref,
                 q_ref, k_ref, v_ref, lf_ref, zr_ref, gate_ref):
    x = x_ref[...]
    ms = jnp.mean(x * x, axis=-1, keepdims=True)
    xn = (x * lax.rsqrt(ms + NORM_EPS) * g_ref[...]).astype(bf16)

    def head_norm(z, gain):
        msq = _mm((z * z).astype(bf16), bd_ref[...])
        return z * lax.rsqrt(msq + NORM_EPS) * gain

    zq = _mm(xn, w_ref[:, 0:WIDTH])
    q_ref[...] = (head_norm(zq, qn_ref[...]) * (HEAD_DIM ** -0.5)).astype(bf16)
    zk = _mm(xn, w_ref[:, WIDTH:2 * WIDTH])
    k_ref[...] = head_norm(zk, kn_ref[...]).astype(bf16)
    v_ref[...] = _mm(xn, w_ref[:, 2 * WIDTH:3 * WIDTH]).astype(bf16)
    zf = _mm(xn, w_ref[:, 3 * WIDTH:FOX_COLS]) + fb_ref[...]
    lf_ref[...] = -_softplus(-zf)
    zr_ref[...] = _mm(xn, w_ref[:, FOX_COLS:FOX_COLS + RWKV_COLS])
    zg = _mm(xn, w_ref[:, FOX_COLS + RWKV_COLS:])
    gate_ref[...] = _sigmoid(zg).astype(bf16)


def _proj(hflat, g, w, qn, kn, fb, bd):
    n, d = hflat.shape
    tm = _row_block(n)
    ncol = w.shape[1]
    ngate = ncol - FOX_COLS - RWKV_COLS
    row = lambda c: pl.BlockSpec((tm, c), lambda i: (i, 0))
    return pl.pallas_call(
        _proj_kernel,
        grid=(n // tm,),
        in_specs=[row(d), _const_spec((1, d)), _const_spec((d, ncol)), _const_spec((1, WIDTH)),
                  _const_spec((1, WIDTH)), _const_spec((1, LANES)), _const_spec((WIDTH, WIDTH))],
        out_specs=[row(WIDTH), row(WIDTH), row(WIDTH), row(LANES), row(RWKV_COLS), row(ngate)],
        out_shape=[jax.ShapeDtypeStruct((n, WIDTH), bf16)] * 3
        + [jax.ShapeDtypeStruct((n, LANES), f32), jax.ShapeDtypeStruct((n, RWKV_COLS), f32),
           jax.ShapeDtypeStruct((n, ngate), bf16)],
        compiler_params=_cparams("parallel"),
        name="proj",
    )(hflat, g, w, qn, kn, fb, bd)


def _cumsum_kernel(lf_ref, o_ref, *, pad, ta):
    length = lf_ref.shape[0]
    r = lax.broadcasted_iota(i32, (LANES, LANES), 0)
    c = lax.broadcasted_iota(i32, (LANES, LANES), 1)
    tri = (r >= c).astype(f32)
    lane = lax.broadcasted_iota(i32, (N_HEADS, LANES), 1)
    carry = jnp.zeros((1, LANES), f32)
    per = ta // LANES
    for blk in range(length // LANES):
        cs = _mm(tri, lf_ref[blk * LANES:(blk + 1) * LANES, :], precision=HIGHEST) + carry
        carry = cs[LANES - 1:LANES, :]
        ct = cs.T[0:N_HEADS, :]
        neg = jnp.where(lane + blk * LANES >= pad, -ct, MASK_VALUE)
        j, off = blk // per, (blk % per) * LANES
        for h in range(N_HEADS):
            o_ref[h, j:j + 1, off:off + LANES] = neg[h:h + 1, :]


def _neg_cumsum(lf, pad, ta):
    b, length, _ = lf.shape
    nb = length // ta
    return pl.pallas_call(
        functools.partial(_cumsum_kernel, pad=pad, ta=ta),
        grid=(b,),
        in_specs=[pl.BlockSpec((None, length, LANES), lambda i: (i, 0, 0))],
        out_specs=pl.BlockSpec((None, N_HEADS, nb, ta), lambda i: (i, 0, 0, 0)),
        out_shape=jax.ShapeDtypeStruct((b, N_HEADS, nb, ta), f32),
        compiler_params=_cparams("parallel"),
        name="cumsum",
    )(lf)


def _attn_kernel(q_ref, k_ref, v_ref, nc_ref, o_ref, *, ta):
    qi = pl.program_id(2)
    q = q_ref[...]
    lane = lax.broadcasted_iota(i32, (1, PAIR), 1)
    row = lax.broadcasted_iota(i32, (ta, ta), 0)
    col = lax.broadcasted_iota(i32, (ta, ta), 1)
    outs = []
    for h in range(2):
        head = (lane // HEAD_DIM) == h
        qh = jnp.where(head, q, jnp.zeros_like(q))

        def scores(j):
            start = pl.multiple_of(j * ta, ta)
            ks = k_ref[pl.ds(start, ta), :]
            vs = v_ref[pl.ds(start, ta), :]
            s = _nt(qh, ks) + nc_ref[h, pl.ds(j, 1), :]
            return s, vs

        def update(carry, s, vs):
            m, l, acc = carry
            m_new = jnp.maximum(m, jnp.max(s, axis=-1, keepdims=True))
            p = jnp.exp(s - m_new)
            alpha = jnp.exp(m - m_new)
            l = alpha * l + jnp.sum(p, axis=-1, keepdims=True)
            acc = alpha * acc + _mm(p.astype(bf16), vs)
            return m_new, l, acc

        def body(j, carry):
            s, vs = scores(j)
            return update(carry, s, vs)

        init = (jnp.full((ta, 1), MASK_VALUE, f32), jnp.zeros((ta, 1), f32), jnp.zeros((ta, PAIR), f32))
        carry = lax.fori_loop(0, qi, body, init)
        s, vs = scores(qi)
        s = jnp.where(col <= row, s, MASK_VALUE)
        m, l, acc = update(carry, s, vs)
        outs.append(acc / l)
    o_ref[...] = jnp.where((lane // HEAD_DIM) == 0, outs[0], outs[1]).astype(o_ref.dtype)


def _attention(q, k, v, negc, ta):
    b, length, _ = q.shape
    nb = length // ta
    npair = N_HEADS // 2
    return pl.pallas_call(
        functools.partial(_attn_kernel, ta=ta),
        grid=(b, npair, nb),
        in_specs=[pl.BlockSpec((None, ta, PAIR), lambda i, p, j: (i, j, p)),
                  pl.BlockSpec((None, length, PAIR), lambda i, p, j: (i, 0, p)),
                  pl.BlockSpec((None, length, PAIR), lambda i, p, j: (i, 0, p)),
                  pl.BlockSpec((None, 2, nb, ta), lambda i, p, j: (i, p, 0, 0))],
        out_specs=pl.BlockSpec((None, ta, PAIR), lambda i, p, j: (i, j, p)),
        out_shape=jax.ShapeDtypeStruct((b, length, WIDTH), bf16),
        compiler_params=_cparams("parallel", "parallel", "arbitrary"),
        name="attn",
    )(q, k, v, negc)


def _stack(x, lane_head):
    zero = jnp.zeros_like(x)
    return jnp.concatenate([jnp.where(lane_head == 0, x, zero), jnp.where(lane_head == 1, x, zero)], axis=0)


def _unit_lower_inverse(n, r, c):
    eye = (r == c).astype(f32)
    t = eye - jnp.where(((r ^ c) == 1) & ((r & 1) == 1), n, 0.0)
    s = 2
    while s < CHUNK:
        sel = ((r // (2 * s)) == (c // (2 * s))) & ((r & s) != 0) & ((c & s) == 0)
        ns = jnp.where(sel, n, 0.0).astype(bf16)
        tn = _mm(t.astype(bf16), ns)
        t = t - _mm(tn.astype(bf16), t.astype(bf16))
        s *= 2
    return t


def _rwkv_kernel(z_ref, mu_ref, w0_ref, a0_ref, kk_ref, ka_ref, rk_ref, lng_ref, lnb_ref,
                 wup_ref, aup_ref, gup_ref, bdm_ref, o_ref, s_ref, prev_ref):
    ci = pl.program_id(1)

    @pl.when(ci == 0)
    def _():
        s_ref[...] = jnp.zeros_like(s_ref)
        prev_ref[...] = jnp.zeros_like(prev_ref)

    z = z_ref[...]
    rows = lax.broadcasted_iota(i32, (CHUNK, 1), 0)
    zprev = jnp.where(rows == 0, prev_ref[...], pltpu.roll(z, 1, 0))
    prev_ref[...] = z[CHUNK - 1:CHUNK, :]
    zs = z + mu_ref[...] * (zprev - z)
    r = zs[:, 0:WIDTH]
    k = zs[:, WIDTH:2 * WIDTH]
    v = zs[:, 2 * WIDTH:3 * WIDTH]
    o1 = 3 * WIDTH
    wd = zs[:, o1:o1 + LANES]
    ad = zs[:, o1 + LANES:o1 + 2 * LANES]
    gd = zs[:, o1 + 2 * LANES:o1 + 4 * LANES]

    w = -_softplus(-(w0_ref[...] + _mm(jnp.tanh(wd).astype(bf16), wup_ref[...]))) - 0.5
    ld = -jnp.exp(w)
    a = _sigmoid(a0_ref[...] + _mm(ad.astype(bf16), aup_ref[...]))
    g = _mm(_sigmoid(gd).astype(bf16), gup_ref[...])
    bdm = bdm_ref[...]
    kk = k * kk_ref[...]
    nrm = jnp.sqrt(_mm(kk * kk, bdm, precision=HIGHEST) * HEAD_DIM)
    kk = kk / jnp.maximum(nrm, 1e-12)
    k2 = k * (1.0 + (a - 1.0) * ka_ref[...])
    b = kk * a

    tr = lax.broadcasted_iota(i32, (CHUNK, CHUNK), 0)
    tc = lax.broadcasted_iota(i32, (CHUNK, CHUNK), 1)
    lc = _mm((tr >= tc).astype(f32), ld, precision=HIGHEST)
    lp = lc - ld
    mid = lc[CHUNK // 2 - 1:CHUNK // 2, :]
    tot = lc[CHUNK - 1:CHUNK, :]
    kq = kk * jnp.exp(lp - mid)
    rr = r * jnp.exp(lc - mid)
    e_after = jnp.exp(mid - lc)
    kh = k2 * e_after
    bh = b * e_after
    e_end = jnp.exp(tot - lc)
    kc = k2 * e_end
    bc = b * e_end
    e_mid = jnp.exp(mid)
    p_end = jnp.exp(tot)

    sr = lax.broadcasted_iota(i32, (PAIR, PAIR), 0)
    sc = lax.broadcasted_iota(i32, (PAIR, PAIR), 1)
    same = (sr // CHUNK) == (sc // CHUNK)
    strict = same & ((sr % CHUNK) > (sc % CHUNK))
    incl = same & ((sr % CHUNK) >= (sc % CHUNK))
    eye = sr == sc
    lane_head = lax.broadcasted_iota(i32, (1, PAIR), 1) // HEAD_DIM

    ys = []
    for p in range(N_HEADS // 2):
        sl = slice(p * PAIR, (p + 1) * PAIR)
        st = lambda x: _stack(x[:, sl], lane_head)
        kq_s, rr_s, kh_s, bh_s, v_s, kc_s, bc_s = (st(x) for x in (kq, rr, kh, bh, v, kc, bc))
        amat = _nt(jnp.concatenate([kq_s, rr_s], axis=0).astype(bf16),
                   jnp.concatenate([kh_s, bh_s], axis=0).astype(bf16))
        a_kk = jnp.where(strict, amat[0:PAIR, 0:PAIR], 0.0).astype(bf16)
        a_kb = jnp.where(strict, amat[0:PAIR, PAIR:], 0.0)
        a_rk = jnp.where(incl, amat[PAIR:, 0:PAIR], 0.0).astype(bf16)
        a_rb = jnp.where(incl, amat[PAIR:, PAIR:], 0.0).astype(bf16)
        t = _unit_lower_inverse(a_kb, sr, sc).astype(bf16)
        vb = v_s.astype(bf16)
        wm = _mm(t, kq_s.astype(bf16))
        uv = _mm(t, _mm(a_kk, vb).astype(bf16))
        rq = rr_s - _mm(a_rb, wm.astype(bf16))
        yv = _mm(a_rk, vb) - _mm(a_rb, uv.astype(bf16))
        wm_t = (wm * e_mid[:, sl]).astype(bf16)
        rq_t = (rq * e_mid[:, sl]).astype(bf16)
        bcb = bc_s.astype(bf16)
        omega = _tn(wm_t, bcb)
        psi = _tn(vb, kc_s.astype(bf16)) - _tn(uv.astype(bf16), bcb)
        phi = jnp.where(eye, p_end[:, sl], 0.0) - omega
        s0 = s_ref[p]
        y = _nt(rq_t, s0.astype(bf16)) + yv
        s_ref[p] = _mm(s0, phi, precision=HIGHEST) + psi
        ys.append(y[0:CHUNK, :] + y[CHUNK:, :])
    y = jnp.concatenate(ys, axis=1)

    mean = _mm(y, bdm, precision=HIGHEST)
    yc = y - mean
    var = _mm(yc * yc, bdm, precision=HIGHEST)
    yn = yc * lax.rsqrt(var + RWKV_GN_EPS) * lng_ref[...] + lnb_ref[...]
    bonus = _mm(r * k2 * rk_ref[...], bdm, precision=HIGHEST) * HEAD_DIM * v
    o_ref[...] = ((yn + bonus) * g).astype(o_ref.dtype)


def _rwkv(zr, mu, w0, a0, k_k, k_a, r_k, ln_g, ln_b, w_up, a_up, g_up, bdm):
    b, length, _ = zr.shape
    vec = _const_spec((1, WIDTH))
    return pl.pallas_call(
        _rwkv_kernel,
        grid=(b, length // CHUNK),
        in_specs=[pl.BlockSpec((None, CHUNK, RWKV_COLS), lambda i, c: (i, c, 0)),
                  _const_spec((1, RWKV_COLS)), vec, vec, vec, vec, vec, vec, vec,
                  _const_spec((LANES, WIDTH)), _const_spec((LANES, WIDTH)), _const_spec((2 * LANES, WIDTH)),
                  _const_spec((WIDTH, WIDTH))],
        out_specs=pl.BlockSpec((None, CHUNK, WIDTH), lambda i, c: (i, c, 0)),
        out_shape=jax.ShapeDtypeStruct((b, length, WIDTH), bf16),
        scratch_shapes=[pltpu.VMEM((N_HEADS // 2, PAIR, PAIR), f32), pltpu.VMEM((1, RWKV_COLS), f32)],
        compiler_params=_cparams("parallel", "arbitrary"),
        name="rwkv",
    )(zr, mu, w0, a0, k_k, k_a, r_k, ln_g, ln_b, w_up, a_up, g_up, bdm)


def _merge_kernel(yf_ref, yr_ref, gate_ref, h_ref, wf_ref, wr_ref, wo_ref, o_ref):
    d = h_ref.shape[1]
    pf = _mm(yf_ref[...], wf_ref[...])
    pr = _mm(yr_ref[...], wr_ref[...])
    mixed = gate_ref[:, 0:d].astype(f32) * pf + gate_ref[:, d:2 * d].astype(f32) * pr
    o_ref[...] = h_ref[...] + _mm(mixed.astype(bf16), wo_ref[...])


def _merge(yf, yr, gates, hflat, wf, wr, wo):
    n, d = hflat.shape
    tm = _row_block(n)
    row = lambda c: pl.BlockSpec((tm, c), lambda i: (i, 0))
    return pl.pallas_call(
        _merge_kernel,
        grid=(n // tm,),
        in_specs=[row(WIDTH), row(WIDTH), row(2 * d), row(d),
                  _const_spec((WIDTH, d)), _const_spec((WIDTH, d)), _const_spec((d, d))],
        out_specs=row(d),
        out_shape=jax.ShapeDtypeStruct((n, d), f32),
        compiler_params=_cparams("parallel"),
        name="merge",
    )(yf, yr, gates, hflat, wf, wr, wo)


def _top_rows(s, ids, k):
    n = s.shape[0]
    pos = lax.broadcasted_iota(i32, s.shape, 0)
    vals, outs = [], []
    for _ in range(k):
        m = jnp.max(s, axis=0, keepdims=True)
        first = jnp.min(jnp.where(s == m, pos, n), axis=0, keepdims=True)
        hit = pos == first
        vals.append(m)
        outs.append(jnp.max(jnp.where(hit, ids, -1), axis=0, keepdims=True))
        s = jnp.where(hit, -jnp.inf, s)
    return jnp.concatenate(vals, axis=0), jnp.concatenate(outs, axis=0)


def _peer_sel_kernel(h_ref, g_ref, wq_ref, k1_ref, k2_ref, xn_ref, idx_ref, sh_ref, gate_ref):
    x = h_ref[...]
    ms = jnp.mean(x * x, axis=-1, keepdims=True)
    xn = x * lax.rsqrt(ms + NORM_EPS) * g_ref[...]
    xn_ref[...] = xn
    qt = _nt(wq_ref[...], xn.astype(bf16))
    half = PEER_QDIM // 2
    key_id = lax.broadcasted_iota(i32, (N_KEYS, PEER_TOK), 0)
    ids, gates = [], []
    for h in range(PEER_HEADS):
        q1 = qt[h * PEER_QDIM:h * PEER_QDIM + half, :].astype(bf16)
        q2 = qt[h * PEER_QDIM + half:(h + 1) * PEER_QDIM, :].astype(bf16)
        t1, i1 = _top_rows(_mm(k1_ref[...], q1), key_id, PEER_TOPK)
        t2, i2 = _top_rows(_mm(k2_ref[...], q2), key_id, PEER_TOPK)
        cand_s = jnp.concatenate([t1[a:a + 1, :] + t2 for a in range(PEER_TOPK)], axis=0)
        cand_i = jnp.concatenate([i1[a:a + 1, :] * N_KEYS + i2 for a in range(PEER_TOPK)], axis=0)
        top_s, top_i = _top_rows(cand_s, cand_i, PEER_TOPK)
        e = jnp.exp(top_s - top_s[0:1, :])
        gates.append(e / jnp.sum(e, axis=0, keepdims=True))
        ids.append(top_i)
    ids = jnp.concatenate(ids, axis=0).T
    idx_ref[...] = ids >> 1
    sh_ref[...] = 16 - 16 * (ids & 1)
    gate_ref[...] = jnp.concatenate(gates, axis=0).T


def _peer_select(h2, g, wq_t, k1, k2, batch, seq, length):
    d = h2.shape[-1]
    nt = seq // PEER_TOK
    skip = (length - seq) // PEER_TOK
    nblk = batch * nt
    blk = pl.BlockSpec((None, PEER_TOK, PEER_PAIRS), lambda i, j: (i * nt + j, 0, 0))
    return pl.pallas_call(
        _peer_sel_kernel,
        grid=(batch, nt),
        in_specs=[pl.BlockSpec((None, PEER_TOK, d), lambda i, j: (i, j + skip, 0)),
                  _const_spec((1, d)), _const_spec(wq_t.shape), _const_spec(k1.shape), _const_spec(k2.shape)],
        out_specs=[pl.BlockSpec((PEER_TOK, d), lambda i, j: (i * nt + j, 0)), blk, blk, blk],
        out_shape=[jax.ShapeDtypeStruct((nblk * PEER_TOK, d), f32),
                   jax.ShapeDtypeStruct((nblk, PEER_TOK, PEER_PAIRS), i32),
                   jax.ShapeDtypeStruct((nblk, PEER_TOK, PEER_PAIRS), i32),
                   jax.ShapeDtypeStruct((nblk, PEER_TOK, PEER_PAIRS), f32)],
        compiler_params=_cparams("parallel", "parallel"),
        name="peer_sel",
    )(h2.reshape(batch, length, d), g, wq_t, k1, k2)


def _pack_pairs(tab):
    e, d = tab.shape
    t = tab.astype(bf16).reshape(e // 2, 2, d // LANES, LANES)
    lo = lax.bitcast_convert_type(t[:, 0], jnp.uint16).astype(jnp.uint32)
    hi = lax.bitcast_convert_type(t[:, 1], jnp.uint16).astype(jnp.uint32)
    return lax.bitcast_convert_type(lo | (hi << 16), i32)


def _row_f32(word, sh):
    return lax.bitcast_convert_type((word << sh) & jnp.int32(-65536), f32)


def _peer_hid_kernel(idx_ref, sh_ref, x_ref, gate_ref, tab_ref, o_ref, part_ref):
    ones = jnp.ones((8, LANES), f32)

    def tok(t, carry):
        xt = x_ref[t]
        for k in range(PEER_PAIRS):
            row = _row_f32(tab_ref[idx_ref[t, k]], sh_ref[t, k])
            part_ref[k:k + 1, :] = jnp.sum(row * xt, axis=0, keepdims=True)
        dots = _nt(ones, part_ref[...], precision=HIGHEST)
        o_ref[pl.ds(t, 1), :] = dots[0:1, :]
        return carry

    lax.fori_loop(0, PEER_TOK, tok, 0)
    hid = o_ref[...]
    gelu = 0.5 * hid * (1.0 + lax.erf(hid * (2.0 ** -0.5)))
    o_ref[...] = gate_ref[...] * gelu


def _peer_out_kernel(idx_ref, sh_ref, c_ref, h_ref, tab_ref, o_ref):
    def tok(t, carry):
        acc = h_ref[t]
        for k in range(PEER_PAIRS):
            acc = acc + c_ref[t, k] * _row_f32(tab_ref[idx_ref[t, k]], sh_ref[t, k])
        o_ref[t] = acc
        return carry

    lax.fori_loop(0, PEER_TOK, tok, 0)


def _smem_blk(index_map):
    return pl.BlockSpec((None, PEER_TOK, PEER_PAIRS), index_map, memory_space=pltpu.SMEM)


def _table_spec(tab, nd):
    return pl.BlockSpec(tab.shape, lambda *_: (0, 0, 0), pipeline_mode=pl.Buffered(1))


def _peer_hidden(idx, sh, xn, gate, utab):
    nblk = idx.shape[0]
    sub = xn.shape[1] // LANES
    x3 = xn.reshape(xn.shape[0], sub, LANES)
    vblk = pl.BlockSpec((None, PEER_TOK, PEER_PAIRS), lambda i: (i, 0, 0))
    return pl.pallas_call(
        _peer_hid_kernel,
        grid=(nblk,),
        in_specs=[_smem_blk(lambda i: (i, 0, 0)), _smem_blk(lambda i: (i, 0, 0)),
                  pl.BlockSpec((PEER_TOK, sub, LANES), lambda i: (i, 0, 0)), vblk, _table_spec(utab, 1)],
        out_specs=vblk,
        out_shape=jax.ShapeDtypeStruct((nblk, PEER_TOK, PEER_PAIRS), f32),
        scratch_shapes=[pltpu.VMEM((PEER_PAIRS, LANES), f32)],
        compiler_params=_cparams("arbitrary"),
        name="peer_hid",
    )(idx, sh, x3, gate, utab)


def _peer_output(idx, sh, coef, h2, vtab, batch, seq, length):
    d = h2.shape[-1]
    sub = d // LANES
    nt = seq // PEER_TOK
    skip = (length - seq) // PEER_TOK
    h4 = h2.reshape(batch, length, sub, LANES)
    sm = lambda: _smem_blk(lambda i, j: (i * nt + j, 0, 0))
    return pl.pallas_call(
        _peer_out_kernel,
        grid=(batch, nt),
        in_specs=[sm(), sm(), sm(),
                  pl.BlockSpec((None, PEER_TOK, sub, LANES), lambda i, j: (i, j + skip, 0, 0)),
                  _table_spec(vtab, 2)],
        out_specs=pl.BlockSpec((None, PEER_TOK, sub, LANES), lambda i, j: (i, j, 0, 0)),
        out_shape=jax.ShapeDtypeStruct((batch, seq, sub, LANES), f32),
        compiler_params=_cparams("arbitrary", "arbitrary"),
        name="peer_out",
    )(idx, sh, coef, h4, vtab)


def _pad_cols(w, width):
    return jnp.pad(w, ((0, 0), (0, width - w.shape[1])))


def _pad_rows(w, height):
    return jnp.pad(w, ((0, height - w.shape[0]), (0, 0)))


def kernel(x, meta_tokens, norm1_g, w_in, fox_q_norm, fox_k_norm, fox_f_bias, rwkv_mu, rwkv_w0, rwkv_w_up, rwkv_a0, rwkv_a_up, rwkv_g_up, rwkv_k_k, rwkv_k_a, rwkv_r_k, rwkv_ln_g, rwkv_ln_b, w_branch_fox, w_branch_rwkv, w_out, norm2_g, peer_w_q, peer_sub_k1, peer_sub_k2, peer_u, peer_v):
    batch, seq, d = x.shape
    assert w_in.shape[0] == 1, "one layer"
    assert seq % PEER_TOK == 0 and d % LANES == 0
    t_real = seq + N_META
    pad = (-t_real) % ATTN_BLOCK
    length = t_real + pad
    ta = _attn_block(length)

    meta = jnp.broadcast_to(meta_tokens[None].astype(x.dtype), (batch, N_META, d))
    hpad = jnp.concatenate([jnp.zeros((batch, pad, d), x.dtype), meta, x], axis=1).reshape(batch * length, d)
    w = w_in[0]
    fox_w = 3 * WIDTH + N_HEADS
    o = fox_w
    lora = lambda lo, n, width: _pad_cols(w[:, lo:lo + n], width)
    w_all = jnp.concatenate([
        w[:, 0:3 * WIDTH], _pad_cols(w[:, 3 * WIDTH:fox_w], LANES),
        w[:, o:o + 3 * WIDTH],
        lora(o + 3 * WIDTH, DECAY_LORA, LANES),
        lora(o + 3 * WIDTH + DECAY_LORA, AAA_LORA, LANES),
        lora(o + 3 * WIDTH + DECAY_LORA + AAA_LORA, GATE_LORA, 2 * LANES),
        w[:, o + 3 * WIDTH + DECAY_LORA + AAA_LORA + GATE_LORA:],
    ], axis=1).astype(bf16)
    mu = rwkv_mu[0]
    mu_all = jnp.concatenate([
        mu[0:3 * WIDTH], jnp.pad(mu[3 * WIDTH:3 * WIDTH + DECAY_LORA], (0, LANES - DECAY_LORA)),
        jnp.pad(mu[3 * WIDTH + DECAY_LORA:3 * WIDTH + DECAY_LORA + AAA_LORA], (0, LANES - AAA_LORA)),
        jnp.pad(mu[3 * WIDTH + DECAY_LORA + AAA_LORA:], (0, 2 * LANES - GATE_LORA)),
    ])[None]
    head_of = jnp.arange(WIDTH) // HEAD_DIM
    bdm = (head_of[:, None] == head_of[None, :]).astype(f32) / HEAD_DIM
    vec = lambda p: p[0].reshape(1, -1)

    q, k, v, lf, zr, gates = _proj(hpad, vec(norm1_g), w_all, vec(fox_q_norm), vec(fox_k_norm),
                                   jnp.pad(fox_f_bias[0], (0, LANES - N_HEADS))[None], bdm.astype(bf16))
    negc = _neg_cumsum(lf.reshape(batch, length, LANES), pad, ta)
    three = lambda a: a.reshape(batch, length, a.shape[-1])
    y_fox = _attention(three(q), three(k), three(v), negc, ta)
    y_rwkv = _rwkv(three(zr), mu_all, vec(rwkv_w0), vec(rwkv_a0), vec(rwkv_k_k), vec(rwkv_k_a), vec(rwkv_r_k),
                   vec(rwkv_ln_g), vec(rwkv_ln_b),
                   _pad_rows(rwkv_w_up[0], LANES).astype(bf16), _pad_rows(rwkv_a_up[0], LANES).astype(bf16),
                   _pad_rows(rwkv_g_up[0], 2 * LANES).astype(bf16), bdm)
    h2 = _merge(y_fox.reshape(-1, WIDTH), y_rwkv.reshape(-1, WIDTH), gates, hpad,
                w_branch_fox[0].astype(bf16), w_branch_rwkv[0].astype(bf16), w_out[0].astype(bf16))

    xn2, idx, sh, gate = _peer_select(h2, vec(norm2_g), peer_w_q[0].T.astype(bf16),
                                      peer_sub_k1[0].astype(bf16), peer_sub_k2[0].astype(bf16), batch, seq, length)
    coef = _peer_hidden(idx, sh, xn2, gate, _pack_pairs(peer_u[0]))
    out = _peer_output(idx, sh, coef, h2, _pack_pairs(peer_v[0]), batch, seq, length)
    return out.reshape(batch, seq, d)
```

```python
import functools

import jax
import jax.numpy as jnp
from jax import lax
from jax.experimental import pallas as pl
from jax.experimental.pallas import tpu as pltpu

f32 = jnp.float32
bf16 = jnp.bfloat16
i32 = jnp.int32
HIGHEST = lax.Precision.HIGHEST

N_META = 16
HEAD_DIM = 64
N_HEADS = 8
WIDTH = N_HEADS * HEAD_DIM
DECAY_LORA, AAA_LORA, GATE_LORA = 64, 64, 160
ATTN_BLOCK = 128
PEER_HEADS, PEER_QDIM, N_KEYS, PEER_TOPK = 8, 256, 128, 16
NORM_EPS = 1e-6
RWKV_GN_EPS = 64e-5
MASK_VALUE = -1e30

LANES = 128
VMEM_LIMIT_BYTES = 56 * 1024 * 1024

CHUNK = 64
PAIR = 2 * HEAD_DIM
PEER_TOK = 128
PEER_PAIRS = PEER_HEADS * PEER_TOPK
PEER_UNROLL = 4
TABLE_ROWS = 4
FOX_COLS = 3 * WIDTH + LANES
RWKV_COLS = 3 * WIDTH + 2 * LANES + 2 * LANES


def _row_block(n):
    for t in (512, 384, 256, 128):
        if n % t == 0:
            return t
    raise ValueError(f"row count {n} is not a multiple of 128")


def _attn_block(length):
    return 384 if length % 384 == 0 else ATTN_BLOCK


def _cparams(*sem):
    return pltpu.CompilerParams(dimension_semantics=sem, vmem_limit_bytes=VMEM_LIMIT_BYTES)


def _const_spec(shape):
    nd = len(shape)
    return pl.BlockSpec(shape, lambda *_: (0,) * nd)


def _nt(a, b, **kw):
    return lax.dot_general(a, b, (((1,), (1,)), ((), ())), preferred_element_type=f32, **kw)


def _tn(a, b, **kw):
    return lax.dot_general(a, b, (((0,), (0,)), ((), ())), preferred_element_type=f32, **kw)


def _mm(a, b, **kw):
    return jnp.dot(a, b, preferred_element_type=f32, **kw)


def _softplus(y):
    return jnp.maximum(y, 0.0) + jnp.log1p(jnp.exp(-jnp.abs(y)))


def _sigmoid(y):
    return 1.0 / (1.0 + jnp.exp(-y))


def _head_mean(x, bd):
    hi = x.astype(bf16)
    lo = (x - hi.astype(f32)).astype(bf16)
    return _mm(hi, bd) + _mm(lo, bd)


def _proj_kernel(x_ref, g_ref, w_ref, qn_ref, kn_ref, fb_ref, bd_ref,
                 q_ref, k_ref, v_ref, lf_ref, zr_ref, gate_ref):
    x = x_ref[...]
    ms = jnp.mean(x * x, axis=-1, keepdims=True)
    xn = (x * lax.rsqrt(ms + NORM_EPS) * g_ref[...]).astype(bf16)

    def head_norm(z, gain):
        msq = _mm((z * z).astype(bf16), bd_ref[...])
        return z * lax.rsqrt(msq + NORM_EPS) * gain

    zq = _mm(xn, w_ref[:, 0:WIDTH])
    q_ref[...] = (head_norm(zq, qn_ref[...]) * (HEAD_DIM ** -0.5)).astype(bf16)
    zk = _mm(xn, w_ref[:, WIDTH:2 * WIDTH])
    k_ref[...] = head_norm(zk, kn_ref[...]).astype(bf16)
    v_ref[...] = _mm(xn, w_ref[:, 2 * WIDTH:3 * WIDTH]).astype(bf16)
    zf = _mm(xn, w_ref[:, 3 * WIDTH:FOX_COLS]) + fb_ref[...]
    lf_ref[...] = -_softplus(-zf)
    zr_ref[...] = _mm(xn, w_ref[:, FOX_COLS:FOX_COLS + RWKV_COLS])
    zg = _mm(xn, w_ref[:, FOX_COLS + RWKV_COLS:])
    gate_ref[...] = _sigmoid(zg).astype(bf16)


def _proj(hflat, g, w, qn, kn, fb, bd):
    n, d = hflat.shape
    tm = _row_block(n)
    ncol = w.shape[1]
    ngate = ncol - FOX_COLS - RWKV_COLS
    row = lambda c: pl.BlockSpec((tm, c), lambda i: (i, 0))
    return pl.pallas_call(
        _proj_kernel,
        grid=(n // tm,),
        in_specs=[row(d), _const_spec((1, d)), _const_spec((d, ncol)), _const_spec((1, WIDTH)),
                  _const_spec((1, WIDTH)), _const_spec((1, LANES)), _const_spec((WIDTH, WIDTH))],
        out_specs=[row(WIDTH), row(WIDTH), row(WIDTH), row(LANES), row(RWKV_COLS), row(ngate)],
        out_shape=[jax.ShapeDtypeStruct((n, WIDTH), bf16)] * 3
        + [jax.ShapeDtypeStruct((n, LANES), f32), jax.ShapeDtypeStruct((n, RWKV_COLS), f32),
           jax.ShapeDtypeStruct((n, ngate), bf16)],
        compiler_params=_cparams("parallel"),
        name="proj",
    )(hflat, g, w, qn, kn, fb, bd)


def _cumsum_kernel(lf_ref, o_ref, *, pad, ta):
    length = lf_ref.shape[0]
    r = lax.broadcasted_iota(i32, (LANES, LANES), 0)
    c = lax.broadcasted_iota(i32, (LANES, LANES), 1)
    tri = (r >= c).astype(f32)
    lane = lax.broadcasted_iota(i32, (N_HEADS, LANES), 1)
    carry = jnp.zeros((1, LANES), f32)
    per = ta // LANES
    for blk in range(length // LANES):
        cs = _mm(tri, lf_ref[blk * LANES:(blk + 1) * LANES, :], precision=HIGHEST) + carry
        carry = cs[LANES - 1:LANES, :]
        ct = cs.T[0:N_HEADS, :]
        neg = jnp.where(lane + blk * LANES >= pad, -ct, MASK_VALUE)
        j, off = blk // per, (blk % per) * LANES
        for h in range(N_HEADS):
            o_ref[h, j:j + 1, off:off + LANES] = neg[h:h + 1, :]


def _neg_cumsum(lf, pad, ta):
    b, length, _ = lf.shape
    nb = length // ta
    return pl.pallas_call(
        functools.partial(_cumsum_kernel, pad=pad, ta=ta),
        grid=(b,),
        in_specs=[pl.BlockSpec((None, length, LANES), lambda i: (i, 0, 0))],
        out_specs=pl.BlockSpec((None, N_HEADS, nb, ta), lambda i: (i, 0, 0, 0)),
        out_shape=jax.ShapeDtypeStruct((b, N_HEADS, nb, ta), f32),
        compiler_params=_cparams("parallel"),
        name="cumsum",
    )(lf)


def _attn_kernel(q_ref, k_ref, v_ref, nc_ref, o_ref, *, ta):
    qi = pl.program_id(2)
    q = q_ref[...]
    lane = lax.broadcasted_iota(i32, (1, PAIR), 1)
    row = lax.broadcasted_iota(i32, (ta, ta), 0)
    col = lax.broadcasted_iota(i32, (ta, ta), 1)
    outs = []
    for h in range(2):
        head = (lane // HEAD_DIM) == h
        qh = jnp.where(head, q, jnp.zeros_like(q))

        def scores(j):
            start = pl.multiple_of(j * ta, ta)
            ks = k_ref[pl.ds(start, ta), :]
            vs = v_ref[pl.ds(start, ta), :]
            s = _nt(qh, ks) + nc_ref[h, pl.ds(j, 1), :]
            return s, vs

        def update(carry, s, vs):
            m, l, acc = carry
            m_new = jnp.maximum(m, jnp.max(s, axis=-1, keepdims=True))
            p = jnp.exp(s - m_new)
            alpha = jnp.exp(m - m_new)
            l = alpha * l + jnp.sum(p, axis=-1, keepdims=True)
            acc = alpha * acc + _mm(p.astype(bf16), vs)
            return m_new, l, acc

        def body(j, carry):
            s, vs = scores(j)
            return update(carry, s, vs)

        init = (jnp.full((ta, 1), MASK_VALUE, f32), jnp.zeros((ta, 1), f32), jnp.zeros((ta, PAIR), f32))
        carry = lax.fori_loop(0, qi, body, init)
        s, vs = scores(qi)
        s = jnp.where(col <= row, s, MASK_VALUE)
        m, l, acc = update(carry, s, vs)
        outs.append(acc / l)
    o_ref[...] = jnp.where((lane // HEAD_DIM) == 0, outs[0], outs[1]).astype(o_ref.dtype)


def _attention(q, k, v, negc, ta):
    b, length, _ = q.shape
    nb = length // ta
    npair = N_HEADS // 2
    return pl.pallas_call(
        functools.partial(_attn_kernel, ta=ta),
        grid=(b, npair, nb),
        in_specs=[pl.BlockSpec((None, ta, PAIR), lambda i, p, j: (i, j, p)),
                  pl.BlockSpec((None, length, PAIR), lambda i, p, j: (i, 0, p)),
                  pl.BlockSpec((None, length, PAIR), lambda i, p, j: (i, 0, p)),
                  pl.BlockSpec((None, 2, nb, ta), lambda i, p, j: (i, p, 0, 0))],
        out_specs=pl.BlockSpec((None, ta, PAIR), lambda i, p, j: (i, j, p)),
        out_shape=jax.ShapeDtypeStruct((b, length, WIDTH), bf16),
        compiler_params=_cparams("parallel", "parallel", "arbitrary"),
        name="attn",
    )(q, k, v, negc)


def _stack(x, lane_head):
    zero = jnp.zeros_like(x)
    return jnp.concatenate([jnp.where(lane_head == 0, x, zero), jnp.where(lane_head == 1, x, zero)], axis=0)


def _unit_lower_inverse(n, r, c):
    eye = (r == c).astype(f32)
    t = eye - jnp.where(((r ^ c) == 1) & ((r & 1) == 1), n, 0.0)
    s = 2
    while s < CHUNK:
        sel = ((r // (2 * s)) == (c // (2 * s))) & ((r & s) != 0) & ((c & s) == 0)
        ns = jnp.where(sel, n, 0.0).astype(bf16)
        tn = _mm(t.astype(bf16), ns)
        t = t - _mm(tn.astype(bf16), t.astype(bf16))
        s *= 2
    return t


def _rwkv_kernel(z_ref, mu_ref, w0_ref, a0_ref, kk_ref, ka_ref, rk_ref, lng_ref, lnb_ref,
                 wup_ref, aup_ref, gup_ref, bdm_ref, o_ref, s_ref, prev_ref):
    ci = pl.program_id(1)

    @pl.when(ci == 0)
    def _():
        s_ref[...] = jnp.zeros_like(s_ref)
        prev_ref[...] = jnp.zeros_like(prev_ref)

    z = z_ref[...]
    rows = lax.broadcasted_iota(i32, (CHUNK, 1), 0)
    zprev = jnp.where(rows == 0, prev_ref[...], pltpu.roll(z, 1, 0))
    prev_ref[...] = z[CHUNK - 1:CHUNK, :]
    zs = z + mu_ref[...] * (zprev - z)
    r = zs[:, 0:WIDTH]
    k = zs[:, WIDTH:2 * WIDTH]
    v = zs[:, 2 * WIDTH:3 * WIDTH]
    o1 = 3 * WIDTH
    wd = zs[:, o1:o1 + LANES]
    ad = zs[:, o1 + LANES:o1 + 2 * LANES]
    gd = zs[:, o1 + 2 * LANES:o1 + 4 * LANES]

    w = -_softplus(-(w0_ref[...] + _mm(jnp.tanh(wd).astype(bf16), wup_ref[...]))) - 0.5
    ld = -jnp.exp(w)
    a = _sigmoid(a0_ref[...] + _mm(ad.astype(bf16), aup_ref[...]))
    g = _mm(_sigmoid(gd).astype(bf16), gup_ref[...])
    bdm = bdm_ref[...]
    kk = k * kk_ref[...]
    nrm = jnp.sqrt(_head_mean(kk * kk, bdm) * HEAD_DIM)
    kk = kk / jnp.maximum(nrm, 1e-12)
    k2 = k * (1.0 + (a - 1.0) * ka_ref[...])
    b = kk * a

    tr = lax.broadcasted_iota(i32, (CHUNK, CHUNK), 0)
    tc = lax.broadcasted_iota(i32, (CHUNK, CHUNK), 1)
    lc = _mm((tr >= tc).astype(f32), ld, precision=HIGHEST)
    lp = lc - ld
    mid = lc[CHUNK // 2 - 1:CHUNK // 2, :]
    tot = lc[CHUNK - 1:CHUNK, :]
    kq = kk * jnp.exp(lp - mid)
    rr = r * jnp.exp(lc - mid)
    e_after = jnp.exp(mid - lc)
    kh = k2 * e_after
    bh = b * e_after
    e_end = jnp.exp(tot - lc)
    kc = k2 * e_end
    bc = b * e_end
    e_mid = jnp.exp(mid)
    p_end = jnp.exp(tot)

    sr = lax.broadcasted_iota(i32, (PAIR, PAIR), 0)
    sc = lax.broadcasted_iota(i32, (PAIR, PAIR), 1)
    same = (sr // CHUNK) == (sc // CHUNK)
    strict = same & ((sr % CHUNK) > (sc % CHUNK))
    incl = same & ((sr % CHUNK) >= (sc % CHUNK))
    eye = sr == sc
    lane_head = lax.broadcasted_iota(i32, (1, PAIR), 1) // HEAD_DIM

    ys = []
    for p in range(N_HEADS // 2):
        sl = slice(p * PAIR, (p + 1) * PAIR)
        st = lambda x: _stack(x[:, sl], lane_head)
        kq_s, rr_s, kh_s, bh_s, v_s, kc_s, bc_s = (st(x) for x in (kq, rr, kh, bh, v, kc, bc))
        amat = _nt(jnp.concatenate([kq_s, rr_s], axis=0).astype(bf16),
                   jnp.concatenate([kh_s, bh_s], axis=0).astype(bf16))
        a_kk = jnp.where(strict, amat[0:PAIR, 0:PAIR], 0.0).astype(bf16)
        a_kb = jnp.where(strict, amat[0:PAIR, PAIR:], 0.0)
        a_rk = jnp.where(incl, amat[PAIR:, 0:PAIR], 0.0).astype(bf16)
        a_rb = jnp.where(incl, amat[PAIR:, PAIR:], 0.0).astype(bf16)
        t = _unit_lower_inverse(a_kb, sr, sc).astype(bf16)
        vb = v_s.astype(bf16)
        wm = _mm(t, kq_s.astype(bf16))
        uv = _mm(t, _mm(a_kk, vb).astype(bf16))
        rq = rr_s - _mm(a_rb, wm.astype(bf16))
        yv = _mm(a_rk, vb) - _mm(a_rb, uv.astype(bf16))
        wm_t = (wm * e_mid[:, sl]).astype(bf16)
        rq_t = (rq * e_mid[:, sl]).astype(bf16)
        bcb = bc_s.astype(bf16)
        omega = _tn(wm_t, bcb)
        psi = _tn(vb, kc_s.astype(bf16)) - _tn(uv.astype(bf16), bcb)
        phi = jnp.where(eye, p_end[:, sl], 0.0) - omega
        s0 = s_ref[p]
        y = _nt(rq_t, s0.astype(bf16)) + yv
        s_ref[p] = _mm(s0, phi, precision=HIGHEST) + psi
        ys.append(y[0:CHUNK, :] + y[CHUNK:, :])
    y = jnp.concatenate(ys, axis=1)

    mean = _head_mean(y, bdm)
    yc = y - mean
    var = _head_mean(yc * yc, bdm)
    yn = yc * lax.rsqrt(var + RWKV_GN_EPS) * lng_ref[...] + lnb_ref[...]
    bonus = _head_mean(r * k2 * rk_ref[...], bdm) * HEAD_DIM * v
    o_ref[...] = ((yn + bonus) * g).astype(o_ref.dtype)


def _rwkv(zr, mu, w0, a0, k_k, k_a, r_k, ln_g, ln_b, w_up, a_up, g_up, bdm):
    b, length, _ = zr.shape
    vec = _const_spec((1, WIDTH))
    return pl.pallas_call(
        _rwkv_kernel,
        grid=(b, length // CHUNK),
        in_specs=[pl.BlockSpec((None, CHUNK, RWKV_COLS), lambda i, c: (i, c, 0)),
                  _const_spec((1, RWKV_COLS)), vec, vec, vec, vec, vec, vec, vec,
                  _const_spec((LANES, WIDTH)), _const_spec((LANES, WIDTH)), _const_spec((2 * LANES, WIDTH)),
                  _const_spec((WIDTH, WIDTH))],
        out_specs=pl.BlockSpec((None, CHUNK, WIDTH), lambda i, c: (i, c, 0)),
        out_shape=jax.ShapeDtypeStruct((b, length, WIDTH), bf16),
        scratch_shapes=[pltpu.VMEM((N_HEADS // 2, PAIR, PAIR), f32), pltpu.VMEM((1, RWKV_COLS), f32)],
        compiler_params=_cparams("parallel", "arbitrary"),
        name="rwkv",
    )(zr, mu, w0, a0, k_k, k_a, r_k, ln_g, ln_b, w_up, a_up, g_up, bdm)


def _merge_kernel(yf_ref, yr_ref, gate_ref, h_ref, wf_ref, wr_ref, wo_ref, o_ref):
    d = h_ref.shape[1]
    pf = _mm(yf_ref[...], wf_ref[...])
    pr = _mm(yr_ref[...], wr_ref[...])
    mixed = gate_ref[:, 0:d].astype(f32) * pf + gate_ref[:, d:2 * d].astype(f32) * pr
    o_ref[...] = h_ref[...] + _mm(mixed.astype(bf16), wo_ref[...])


def _merge(yf, yr, gates, hflat, wf, wr, wo):
    n, d = hflat.shape
    tm = _row_block(n)
    row = lambda c: pl.BlockSpec((tm, c), lambda i: (i, 0))
    return pl.pallas_call(
        _merge_kernel,
        grid=(n // tm,),
        in_specs=[row(WIDTH), row(WIDTH), row(2 * d), row(d),
                  _const_spec((WIDTH, d)), _const_spec((WIDTH, d)), _const_spec((d, d))],
        out_specs=row(d),
        out_shape=jax.ShapeDtypeStruct((n, d), f32),
        compiler_params=_cparams("parallel"),
        name="merge",
    )(yf, yr, gates, hflat, wf, wr, wo)


def _top_rows(s, k):
    n = s.shape[0]
    pos = lax.broadcasted_iota(i32, s.shape, 0)
    vals, outs = [], []
    for _ in range(k):
        m = jnp.max(s, axis=0, keepdims=True)
        first = jnp.min(jnp.where(s == m, pos, n), axis=0, keepdims=True)
        vals.append(m)
        outs.append(first)
        s = jnp.where(pos == first, -jnp.inf, s)
    return jnp.concatenate(vals, axis=0), jnp.concatenate(outs, axis=0)


def _staircase():
    pairs = [(a, b) for a in range(PEER_TOPK) for b in range(PEER_TOPK) if (a + 1) * (b + 1) <= PEER_TOPK]
    rows = -(-len(pairs) // 8) * 8
    sel = jnp.zeros((2, rows, PEER_TOPK), f32)
    r = jnp.arange(len(pairs))
    sel = sel.at[0, r, jnp.array([a for a, _ in pairs])].set(1.0)
    sel = sel.at[1, r, jnp.array([b for _, b in pairs])].set(1.0)
    return sel, len(pairs)


def _peer_sel_kernel(h_ref, g_ref, wq_ref, k1_ref, k2_ref, sel_ref, xn_ref, idx_ref, gate_ref, *, n_cand):
    x = h_ref[...]
    ms = jnp.mean(x * x, axis=-1, keepdims=True)
    xn = x * lax.rsqrt(ms + NORM_EPS) * g_ref[...]
    xn_ref[...] = xn
    qt = _nt(wq_ref[...], xn.astype(bf16))
    half = PEER_QDIM // 2
    sel_a, sel_b = sel_ref[0], sel_ref[1]
    pick = lambda sel, t: _mm(sel, t, precision=HIGHEST)
    s1 = jnp.concatenate([_mm(k1_ref[...], qt[h * PEER_QDIM:h * PEER_QDIM + half, :].astype(bf16))
                          for h in range(PEER_HEADS)], axis=1)
    s2 = jnp.concatenate([_mm(k2_ref[...], qt[h * PEER_QDIM + half:(h + 1) * PEER_QDIM, :].astype(bf16))
                          for h in range(PEER_HEADS)], axis=1)
    t1, i1 = _top_rows(s1, PEER_TOPK)
    t2, i2 = _top_rows(s2, PEER_TOPK)
    rows = lax.broadcasted_iota(i32, (sel_a.shape[0], PEER_HEADS * PEER_TOK), 0)
    cand_s = jnp.where(rows < n_cand, pick(sel_a, t1) + pick(sel_b, t2), -jnp.inf)
    cand_i = pick(sel_a, i1.astype(f32)) * N_KEYS + pick(sel_b, i2.astype(f32))
    top_s, top_p = _top_rows(cand_s, PEER_TOPK)
    top_i = jnp.concatenate(
        [jnp.max(jnp.where(rows == top_p[j:j + 1, :], cand_i, -1.0), axis=0, keepdims=True)
         for j in range(PEER_TOPK)], axis=0).astype(i32)
    e = jnp.exp(top_s - top_s[0:1, :])
    gate = e / jnp.sum(e, axis=0, keepdims=True)
    by_head = lambda a: jnp.concatenate([a[:, h * PEER_TOK:(h + 1) * PEER_TOK] for h in range(PEER_HEADS)], axis=0)
    idx_ref[...] = by_head(top_i).T * TABLE_ROWS
    gate_ref[...] = by_head(gate).T


def _peer_select(h2, g, wq_t, k1, k2, batch, seq, length):
    d = h2.shape[-1]
    nt = seq // PEER_TOK
    skip = (length - seq) // PEER_TOK
    nblk = batch * nt
    sel, n_cand = _staircase()
    blk = pl.BlockSpec((None, PEER_TOK, PEER_PAIRS), lambda i, j: (i * nt + j, 0, 0))
    return pl.pallas_call(
        functools.partial(_peer_sel_kernel, n_cand=n_cand),
        grid=(batch, nt),
        in_specs=[pl.BlockSpec((None, PEER_TOK, d), lambda i, j: (i, j + skip, 0)),
                  _const_spec((1, d)), _const_spec(wq_t.shape), _const_spec(k1.shape), _const_spec(k2.shape),
                  _const_spec(sel.shape)],
        out_specs=[pl.BlockSpec((PEER_TOK, d), lambda i, j: (i * nt + j, 0)), blk, blk],
        out_shape=[jax.ShapeDtypeStruct((nblk * PEER_TOK, d), f32),
                   jax.ShapeDtypeStruct((nblk, PEER_TOK, PEER_PAIRS), i32),
                   jax.ShapeDtypeStruct((nblk, PEER_TOK, PEER_PAIRS), f32)],
        compiler_params=_cparams("parallel", "parallel"),
        name="peer_sel",
    )(h2.reshape(batch, length, d), g, wq_t, k1, k2, sel)


def _pack_table(tab):
    e, d = tab.shape
    assert d == 2 * TABLE_ROWS * LANES
    t = lax.bitcast_convert_type(tab.astype(bf16), jnp.uint16).astype(jnp.uint32).reshape(e, 2, TABLE_ROWS, LANES)
    return lax.bitcast_convert_type((t[:, 0] << 16) | t[:, 1], i32).reshape(e * TABLE_ROWS, LANES)


def _unpack(word):
    return (lax.bitcast_convert_type(word & jnp.int32(-65536), f32), lax.bitcast_convert_type(word << 16, f32))


def _expert_rows(idx_ref, tab_ref, t, k):
    return tab_ref[pl.ds(pl.multiple_of(idx_ref[t, k], TABLE_ROWS), TABLE_ROWS), :]


def _peer_hid_kernel(idx_ref, x_ref, gate_ref, fold_ref, tab_ref, o_ref, part_a, part_b, dots_ref):
    parts = (part_a, part_b)
    ones = jnp.ones((8, LANES), bf16)

    def token(t, buf):
        xt = x_ref[t]
        xa, xb = xt[0:TABLE_ROWS, :], xt[TABLE_ROWS:, :]
        for k in range(PEER_PAIRS):
            hi, lo = _unpack(_expert_rows(idx_ref, tab_ref, t, k))
            parts[buf][TABLE_ROWS * k:TABLE_ROWS * (k + 1), :] = hi * xa + lo * xb
        lane_sums = _nt(ones, parts[buf][...].astype(bf16))
        dots_ref[pl.ds(t, 1), :] = lane_sums[0:1, :]

    def body(i, carry):
        for u in range(PEER_UNROLL):
            token(PEER_UNROLL * i + u, u % 2)
        return carry

    lax.fori_loop(0, PEER_TOK // PEER_UNROLL, body, 0)
    hid = _mm(dots_ref[...], fold_ref[...], precision=HIGHEST)
    o_ref[...] = gate_ref[...] * (0.5 * hid * (1.0 + lax.erf(hid * (2.0 ** -0.5))))


def _two_term_rows(row):
    hi = row.astype(bf16).astype(f32)
    r = lax.broadcasted_iota(i32, (8, LANES), 0)
    return jnp.where(r == 0, hi, jnp.where(r == 1, row - hi, 0.0)).astype(bf16)


def _peer_out_kernel(idx_ref, c_ref, h_ref, tab_ref, o_ref, stage_a, stage_b):
    stage = (stage_a, stage_b)
    row = lax.broadcasted_iota(i32, (8, LANES), 0)

    def gather(t, buf):
        for k in range(PEER_PAIRS):
            stage[buf][TABLE_ROWS * k:TABLE_ROWS * (k + 1), :] = _expert_rows(idx_ref, tab_ref, t, k)

    def combine(t, buf):
        lhs = _two_term_rows(c_ref[pl.ds(t, 1), :])
        out = h_ref[t]
        for s in range(TABLE_ROWS):
            hi, lo = _unpack(stage[buf][pl.ds(s, PEER_PAIRS, stride=TABLE_ROWS), :])
            a = _mm(lhs, hi.astype(bf16))
            b = _mm(lhs, lo.astype(bf16))
            out = (out + jnp.where(row == s, a[0:1, :] + a[1:2, :], 0.0)
                   + jnp.where(row == TABLE_ROWS + s, b[0:1, :] + b[1:2, :], 0.0))
        o_ref[t] = out

    gather(0, 0)

    def body(i, carry):
        t0 = PEER_UNROLL * i
        for u in range(PEER_UNROLL):
            gather(jnp.minimum(t0 + u + 1, PEER_TOK - 1), (u + 1) % 2)
            combine(t0 + u, u % 2)
        return carry

    lax.fori_loop(0, PEER_TOK // PEER_UNROLL, body, 0)


def _smem_blk(index_map):
    return pl.BlockSpec((None, PEER_TOK, PEER_PAIRS), index_map, memory_space=pltpu.SMEM)


def _table_spec(tab):
    return pl.BlockSpec(tab.shape, lambda *_: (0, 0), pipeline_mode=pl.Buffered(1))


def _peer_hidden(idx, xn, gate, utab):
    nblk = idx.shape[0]
    sub = xn.shape[1] // LANES
    x3 = xn.reshape(xn.shape[0], sub, LANES)
    vblk = pl.BlockSpec((None, PEER_TOK, PEER_PAIRS), lambda i: (i, 0, 0))
    rows = TABLE_ROWS * PEER_PAIRS
    fold = (jnp.arange(rows)[:, None] // TABLE_ROWS == jnp.arange(PEER_PAIRS)[None, :]).astype(f32)
    return pl.pallas_call(
        _peer_hid_kernel,
        grid=(nblk,),
        in_specs=[_smem_blk(lambda i: (i, 0, 0)), pl.BlockSpec((PEER_TOK, sub, LANES), lambda i: (i, 0, 0)), vblk,
                  _const_spec(fold.shape), _table_spec(utab)],
        out_specs=vblk,
        out_shape=jax.ShapeDtypeStruct((nblk, PEER_TOK, PEER_PAIRS), f32),
        scratch_shapes=[pltpu.VMEM((rows, LANES), f32), pltpu.VMEM((rows, LANES), f32),
                        pltpu.VMEM((PEER_TOK, rows), f32)],
        compiler_params=_cparams("arbitrary"),
        name="peer_hid",
    )(idx, x3, gate, fold, utab)


def _peer_output(idx, coef, h2, vtab, batch, seq, length):
    d = h2.shape[-1]
    sub = d // LANES
    nt = seq // PEER_TOK
    skip = (length - seq) // PEER_TOK
    h4 = h2.reshape(batch, length, sub, LANES)
    rows = TABLE_ROWS * PEER_PAIRS
    blk = lambda i, j: (i * nt + j, 0, 0)
    return pl.pallas_call(
        _peer_out_kernel,
        grid=(batch, nt),
        in_specs=[_smem_blk(blk), pl.BlockSpec((None, PEER_TOK, PEER_PAIRS), blk),
                  pl.BlockSpec((None, PEER_TOK, sub, LANES), lambda i, j: (i, j + skip, 0, 0)),
                  _table_spec(vtab)],
        out_specs=pl.BlockSpec((None, PEER_TOK, sub, LANES), lambda i, j: (i, j, 0, 0)),
        out_shape=jax.ShapeDtypeStruct((batch, seq, sub, LANES), f32),
        scratch_shapes=[pltpu.VMEM((rows, LANES), i32), pltpu.VMEM((rows, LANES), i32)],
        compiler_params=_cparams("arbitrary", "arbitrary"),
        name="peer_out",
    )(idx, coef, h4, vtab)


def _pad_cols(w, width):
    return jnp.pad(w, ((0, 0), (0, width - w.shape[1])))


def _pad_rows(w, height):
    return jnp.pad(w, ((0, height - w.shape[0]), (0, 0)))


def kernel(x, meta_tokens, norm1_g, w_in, fox_q_norm, fox_k_norm, fox_f_bias, rwkv_mu, rwkv_w0, rwkv_w_up, rwkv_a0, rwkv_a_up, rwkv_g_up, rwkv_k_k, rwkv_k_a, rwkv_r_k, rwkv_ln_g, rwkv_ln_b, w_branch_fox, w_branch_rwkv, w_out, norm2_g, peer_w_q, peer_sub_k1, peer_sub_k2, peer_u, peer_v):
    batch, seq, d = x.shape
    assert w_in.shape[0] == 1, "one layer"
    assert seq % PEER_TOK == 0 and d % LANES == 0
    t_real = seq + N_META
    pad = (-t_real) % ATTN_BLOCK
    length = t_real + pad
    ta = _attn_block(length)

    meta = jnp.broadcast_to(meta_tokens[None].astype(x.dtype), (batch, N_META, d))
    hpad = jnp.concatenate([jnp.zeros((batch, pad, d), x.dtype), meta, x], axis=1).reshape(batch * length, d)
    w = w_in[0]
    fox_w = 3 * WIDTH + N_HEADS
    o = fox_w
    lora = lambda lo, n, width: _pad_cols(w[:, lo:lo + n], width)
    w_all = jnp.concatenate([
        w[:, 0:3 * WIDTH], _pad_cols(w[:, 3 * WIDTH:fox_w], LANES),
        w[:, o:o + 3 * WIDTH],
        lora(o + 3 * WIDTH, DECAY_LORA, LANES),
        lora(o + 3 * WIDTH + DECAY_LORA, AAA_LORA, LANES),
        lora(o + 3 * WIDTH + DECAY_LORA + AAA_LORA, GATE_LORA, 2 * LANES),
        w[:, o + 3 * WIDTH + DECAY_LORA + AAA_LORA + GATE_LORA:],
    ], axis=1).astype(bf16)
    mu = rwkv_mu[0]
    mu_all = jnp.concatenate([
        mu[0:3 * WIDTH], jnp.pad(mu[3 * WIDTH:3 * WIDTH + DECAY_LORA], (0, LANES - DECAY_LORA)),
        jnp.pad(mu[3 * WIDTH + DECAY_LORA:3 * WIDTH + DECAY_LORA + AAA_LORA], (0, LANES - AAA_LORA)),
        jnp.pad(mu[3 * WIDTH + DECAY_LORA + AAA_LORA:], (0, 2 * LANES - GATE_LORA)),
    ])[None]
    head_of = jnp.arange(WIDTH) // HEAD_DIM
    bdm = (head_of[:, None] == head_of[None, :]).astype(f32) / HEAD_DIM
    vec = lambda p: p[0].reshape(1, -1)

    q, k, v, lf, zr, gates = _proj(hpad, vec(norm1_g), w_all, vec(fox_q_norm), vec(fox_k_norm),
                                   jnp.pad(fox_f_bias[0], (0, LANES - N_HEADS))[None], bdm.astype(bf16))
    negc = _neg_cumsum(lf.reshape(batch, length, LANES), pad, ta)
    three = lambda a: a.reshape(batch, length, a.shape[-1])
    y_fox = _attention(three(q), three(k), three(v), negc, ta)
    y_rwkv = _rwkv(three(zr), mu_all, vec(rwkv_w0), vec(rwkv_a0), vec(rwkv_k_k), vec(rwkv_k_a), vec(rwkv_r_k),
                   vec(rwkv_ln_g), vec(rwkv_ln_b),
                   _pad_rows(rwkv_w_up[0], LANES).astype(bf16), _pad_rows(rwkv_a_up[0], LANES).astype(bf16),
                   _pad_rows(rwkv_g_up[0], 2 * LANES).astype(bf16), bdm.astype(bf16))
    h2 = _merge(y_fox.reshape(-1, WIDTH), y_rwkv.reshape(-1, WIDTH), gates, hpad,
                w_branch_fox[0].astype(bf16), w_branch_rwkv[0].astype(bf16), w_out[0].astype(bf16))

    xn2, idx, gate = _peer_select(h2, vec(norm2_g), peer_w_q[0].T.astype(bf16),
                                  peer_sub_k1[0].astype(bf16), peer_sub_k2[0].astype(bf16), batch, seq, length)
    coef = _peer_hidden(idx, xn2, gate, _pack_table(peer_u[0]))
    out = _peer_output(idx, coef, h2, _pack_table(peer_v[0]), batch, seq, length)
    return out.reshape(batch, seq, d)
```

```python
import functools

import jax
import jax.numpy as jnp
from jax import lax
from jax.experimental import pallas as pl
from jax.experimental.pallas import tpu as pltpu

f32 = jnp.float32
bf16 = jnp.bfloat16
i32 = jnp.int32
HIGHEST = lax.Precision.HIGHEST

N_META = 16
HEAD_DIM = 64
N_HEADS = 8
WIDTH = N_HEADS * HEAD_DIM
DECAY_LORA, AAA_LORA, GATE_LORA = 64, 64, 160
ATTN_BLOCK = 128
PEER_HEADS, PEER_QDIM, N_KEYS, PEER_TOPK = 8, 256, 128, 16
NORM_EPS = 1e-6
RWKV_GN_EPS = 64e-5
MASK_VALUE = -1e30

LANES = 128
VMEM_LIMIT_BYTES = 56 * 1024 * 1024

CHUNK = 64
PAIR = 2 * HEAD_DIM
PEER_TOK = 128
PEER_PAIRS = PEER_HEADS * PEER_TOPK
PEER_UNROLL = 8
TABLE_ROWS = 4
FOX_COLS = 3 * WIDTH + LANES
RWKV_COLS = 3 * WIDTH + 2 * LANES + 2 * LANES


def _row_block(n):
    for t in (512, 384, 256, 128):
        if n % t == 0:
            return t
    raise ValueError(f"row count {n} is not a multiple of 128")


def _attn_block(length):
    return 384 if length % 384 == 0 else ATTN_BLOCK


def _cparams(*sem):
    return pltpu.CompilerParams(dimension_semantics=sem, vmem_limit_bytes=VMEM_LIMIT_BYTES)


def _const_spec(shape):
    nd = len(shape)
    return pl.BlockSpec(shape, lambda *_: (0,) * nd)


def _nt(a, b, **kw):
    return lax.dot_general(a, b, (((1,), (1,)), ((), ())), preferred_element_type=f32, **kw)


def _tn(a, b, **kw):
    return lax.dot_general(a, b, (((0,), (0,)), ((), ())), preferred_element_type=f32, **kw)


def _mm(a, b, **kw):
    return jnp.dot(a, b, preferred_element_type=f32, **kw)


def _softplus(y):
    return jnp.maximum(y, 0.0) + jnp.log1p(jnp.exp(-jnp.abs(y)))


def _sigmoid(y):
    return 1.0 / (1.0 + jnp.exp(-y))


def _head_mean(x, bd):
    hi = x.astype(bf16)
    lo = (x - hi.astype(f32)).astype(bf16)
    return _mm(hi, bd) + _mm(lo, bd)


def _proj_kernel(x_ref, g_ref, w_ref, qn_ref, kn_ref, fb_ref, bd_ref,
                 q_ref, k_ref, v_ref, lf_ref, zr_ref, gate_ref):
    x = x_ref[...]
    ms = jnp.mean(x * x, axis=-1, keepdims=True)
    xn = (x * lax.rsqrt(ms + NORM_EPS) * g_ref[...]).astype(bf16)

    def head_norm(z, gain):
        msq = _mm((z * z).astype(bf16), bd_ref[...])
        return z * lax.rsqrt(msq + NORM_EPS) * gain

    zq = _mm(xn, w_ref[:, 0:WIDTH])
    q_ref[...] = (head_norm(zq, qn_ref[...]) * (HEAD_DIM ** -0.5)).astype(bf16)
    zk = _mm(xn, w_ref[:, WIDTH:2 * WIDTH])
    k_ref[...] = head_norm(zk, kn_ref[...]).astype(bf16)
    v_ref[...] = _mm(xn, w_ref[:, 2 * WIDTH:3 * WIDTH]).astype(bf16)
    zf = _mm(xn, w_ref[:, 3 * WIDTH:FOX_COLS]) + fb_ref[...]
    lf_ref[...] = -_softplus(-zf)
    zr_ref[...] = _mm(xn, w_ref[:, FOX_COLS:FOX_COLS + RWKV_COLS])
    zg = _mm(xn, w_ref[:, FOX_COLS + RWKV_COLS:])
    gate_ref[...] = _sigmoid(zg).astype(bf16)


def _proj(hflat, g, w, qn, kn, fb, bd):
    n, d = hflat.shape
    tm = _row_block(n)
    ncol = w.shape[1]
    ngate = ncol - FOX_COLS - RWKV_COLS
    row = lambda c: pl.BlockSpec((tm, c), lambda i: (i, 0))
    return pl.pallas_call(
        _proj_kernel,
        grid=(n // tm,),
        in_specs=[row(d), _const_spec((1, d)), _const_spec((d, ncol)), _const_spec((1, WIDTH)),
                  _const_spec((1, WIDTH)), _const_spec((1, LANES)), _const_spec((WIDTH, WIDTH))],
        out_specs=[row(WIDTH), row(WIDTH), row(WIDTH), row(LANES), row(RWKV_COLS), row(ngate)],
        out_shape=[jax.ShapeDtypeStruct((n, WIDTH), bf16)] * 3
        + [jax.ShapeDtypeStruct((n, LANES), f32), jax.ShapeDtypeStruct((n, RWKV_COLS), f32),
           jax.ShapeDtypeStruct((n, ngate), bf16)],
        compiler_params=_cparams("parallel"),
        name="proj",
    )(hflat, g, w, qn, kn, fb, bd)


def _cumsum_kernel(lf_ref, o_ref, *, pad, ta):
    length = lf_ref.shape[0]
    r = lax.broadcasted_iota(i32, (LANES, LANES), 0)
    c = lax.broadcasted_iota(i32, (LANES, LANES), 1)
    tri = (r >= c).astype(f32)
    lane = lax.broadcasted_iota(i32, (N_HEADS, LANES), 1)
    carry = jnp.zeros((1, LANES), f32)
    per = ta // LANES
    for blk in range(length // LANES):
        cs = _mm(tri, lf_ref[blk * LANES:(blk + 1) * LANES, :], precision=HIGHEST) + carry
        carry = cs[LANES - 1:LANES, :]
        ct = cs.T[0:N_HEADS, :]
        neg = jnp.where(lane + blk * LANES >= pad, -ct, MASK_VALUE)
        j, off = blk // per, (blk % per) * LANES
        for h in range(N_HEADS):
            o_ref[h, j:j + 1, off:off + LANES] = neg[h:h + 1, :]


def _neg_cumsum(lf, pad, ta):
    b, length, _ = lf.shape
    nb = length // ta
    return pl.pallas_call(
        functools.partial(_cumsum_kernel, pad=pad, ta=ta),
        grid=(b,),
        in_specs=[pl.BlockSpec((None, length, LANES), lambda i: (i, 0, 0))],
        out_specs=pl.BlockSpec((None, N_HEADS, nb, ta), lambda i: (i, 0, 0, 0)),
        out_shape=jax.ShapeDtypeStruct((b, N_HEADS, nb, ta), f32),
        compiler_params=_cparams("parallel"),
        name="cumsum",
    )(lf)


def _attn_kernel(q_ref, k_ref, v_ref, nc_ref, o_ref, *, ta):
    qi = pl.program_id(2)
    q = q_ref[...]
    lane = lax.broadcasted_iota(i32, (1, PAIR), 1)
    row = lax.broadcasted_iota(i32, (ta, ta), 0)
    col = lax.broadcasted_iota(i32, (ta, ta), 1)
    qh = [jnp.where((lane // HEAD_DIM) == h, q, jnp.zeros_like(q)) for h in range(2)]

    def update(carry, s, vs):
        m, l, acc = carry
        m_new = jnp.maximum(m, jnp.max(s, axis=-1, keepdims=True))
        p = jnp.exp(s - m_new)
        alpha = jnp.exp(m - m_new)
        l = alpha * l + jnp.sum(p, axis=-1, keepdims=True)
        acc = alpha * acc + _mm(p.astype(bf16), vs)
        return m_new, l, acc

    def block(j, carries, diagonal):
        start = pl.multiple_of(j * ta, ta)
        ks = k_ref[pl.ds(start, ta), :]
        vs = v_ref[pl.ds(start, ta), :]
        new = []
        for h in range(2):
            s = _nt(qh[h], ks) + nc_ref[h, pl.ds(j, 1), :]
            if diagonal:
                s = jnp.where(col <= row, s, MASK_VALUE)
            new.append(update(carries[h], s, vs))
        return tuple(new)

    init = (jnp.full((ta, 1), MASK_VALUE, f32), jnp.zeros((ta, 1), f32), jnp.zeros((ta, PAIR), f32))
    carries = lax.fori_loop(0, qi, lambda j, c: block(j, c, False), (init, init))
    (_, l0, acc0), (_, l1, acc1) = block(qi, carries, True)
    o_ref[...] = jnp.where((lane // HEAD_DIM) == 0, acc0 / l0, acc1 / l1).astype(o_ref.dtype)


def _attention(q, k, v, negc, ta):
    b, length, _ = q.shape
    nb = length // ta
    npair = N_HEADS // 2
    return pl.pallas_call(
        functools.partial(_attn_kernel, ta=ta),
        grid=(b, npair, nb),
        in_specs=[pl.BlockSpec((None, ta, PAIR), lambda i, p, j: (i, j, p)),
                  pl.BlockSpec((None, length, PAIR), lambda i, p, j: (i, 0, p)),
                  pl.BlockSpec((None, length, PAIR), lambda i, p, j: (i, 0, p)),
                  pl.BlockSpec((None, 2, nb, ta), lambda i, p, j: (i, p, 0, 0))],
        out_specs=pl.BlockSpec((None, ta, PAIR), lambda i, p, j: (i, j, p)),
        out_shape=jax.ShapeDtypeStruct((b, length, WIDTH), bf16),
        compiler_params=_cparams("parallel", "parallel", "arbitrary"),
        name="attn",
    )(q, k, v, negc)


def _stack(x, lane_head):
    zero = jnp.zeros_like(x)
    return jnp.concatenate([jnp.where(lane_head == 0, x, zero), jnp.where(lane_head == 1, x, zero)], axis=0)


def _unit_lower_inverse(ns, r, c):
    eye = (r == c).astype(f32)
    first = ((r ^ c) == 1) & ((r & 1) == 1)
    ts = [(eye - jnp.where(first, n, 0.0)).astype(bf16) for n in ns]
    s = 2
    while s < CHUNK:
        sel = ((r // (2 * s)) == (c // (2 * s))) & ((r & s) != 0) & ((c & s) == 0)
        low = [jnp.where(sel, n, 0.0).astype(bf16) for n in ns]
        tn = [_mm(t, lo).astype(bf16) for t, lo in zip(ts, low)]
        ts = [(t.astype(f32) - _mm(x, t)).astype(bf16) for t, x in zip(ts, tn)]
        s *= 2
    return ts


def _rwkv_kernel(z_ref, mu_ref, w0_ref, a0_ref, kk_ref, ka_ref, rk_ref, lng_ref, lnb_ref,
                 wup_ref, aup_ref, gup_ref, bdm_ref, o_ref, s_ref, prev_ref):
    ci = pl.program_id(1)

    @pl.when(ci == 0)
    def _():
        s_ref[...] = jnp.zeros_like(s_ref)
        prev_ref[...] = jnp.zeros_like(prev_ref)

    z = z_ref[...]
    rows = lax.broadcasted_iota(i32, (CHUNK, 1), 0)
    zprev = jnp.where(rows == 0, prev_ref[...], pltpu.roll(z, 1, 0))
    prev_ref[...] = z[CHUNK - 1:CHUNK, :]
    zs = z + mu_ref[...] * (zprev - z)
    r = zs[:, 0:WIDTH]
    k = zs[:, WIDTH:2 * WIDTH]
    v = zs[:, 2 * WIDTH:3 * WIDTH]
    o1 = 3 * WIDTH
    wd = zs[:, o1:o1 + LANES]
    ad = zs[:, o1 + LANES:o1 + 2 * LANES]
    gd = zs[:, o1 + 2 * LANES:o1 + 4 * LANES]

    w = -_softplus(-(w0_ref[...] + _mm(jnp.tanh(wd).astype(bf16), wup_ref[...]))) - 0.5
    ld = -jnp.exp(w)
    a = _sigmoid(a0_ref[...] + _mm(ad.astype(bf16), aup_ref[...]))
    g = _mm(_sigmoid(gd).astype(bf16), gup_ref[...])
    bdm = bdm_ref[...]
    kk = k * kk_ref[...]
    nrm = jnp.sqrt(_head_mean(kk * kk, bdm) * HEAD_DIM)
    kk = kk / jnp.maximum(nrm, 1e-12)
    k2 = k * (1.0 + (a - 1.0) * ka_ref[...])
    b = kk * a

    tr = lax.broadcasted_iota(i32, (CHUNK, CHUNK), 0)
    tc = lax.broadcasted_iota(i32, (CHUNK, CHUNK), 1)
    lc = _mm((tr >= tc).astype(f32), ld, precision=HIGHEST)
    lp = lc - ld
    mid = lc[CHUNK // 2 - 1:CHUNK // 2, :]
    tot = lc[CHUNK - 1:CHUNK, :]
    kq = kk * jnp.exp(lp - mid)
    rr = r * jnp.exp(lc - mid)
    e_after = jnp.exp(mid - lc)
    kh = k2 * e_after
    bh = b * e_after
    e_end = jnp.exp(tot - lc)
    kc = k2 * e_end
    bc = b * e_end
    e_mid = jnp.exp(mid)
    p_end = jnp.exp(tot)

    sr = lax.broadcasted_iota(i32, (PAIR, PAIR), 0)
    sc = lax.broadcasted_iota(i32, (PAIR, PAIR), 1)
    same = (sr // CHUNK) == (sc // CHUNK)
    strict = same & ((sr % CHUNK) > (sc % CHUNK))
    incl = same & ((sr % CHUNK) >= (sc % CHUNK))
    eye = sr == sc
    lane_head = lax.broadcasted_iota(i32, (1, PAIR), 1) // HEAD_DIM

    pairs = range(N_HEADS // 2)
    sls = [slice(p * PAIR, (p + 1) * PAIR) for p in pairs]
    st = lambda x: [_stack(x[:, sl], lane_head) for sl in sls]
    kq_s, rr_s, kh_s, bh_s, v_s, kc_s, bc_s = (st(x) for x in (kq, rr, kh, bh, v, kc, bc))
    amat = [_nt(jnp.concatenate([kq_s[p], rr_s[p]], axis=0).astype(bf16),
                jnp.concatenate([kh_s[p], bh_s[p]], axis=0).astype(bf16)) for p in pairs]
    a_kk = [jnp.where(strict, amat[p][0:PAIR, 0:PAIR], 0.0).astype(bf16) for p in pairs]
    a_kb = [jnp.where(strict, amat[p][0:PAIR, PAIR:], 0.0) for p in pairs]
    a_rk = [jnp.where(incl, amat[p][PAIR:, 0:PAIR], 0.0).astype(bf16) for p in pairs]
    a_rb = [jnp.where(incl, amat[p][PAIR:, PAIR:], 0.0).astype(bf16) for p in pairs]
    t = _unit_lower_inverse(a_kb, sr, sc)
    vb = [v_s[p].astype(bf16) for p in pairs]
    av = [_mm(a_kk[p], vb[p]).astype(bf16) for p in pairs]
    wm = [_mm(t[p], kq_s[p].astype(bf16)) for p in pairs]
    uv = [_mm(t[p], av[p]) for p in pairs]
    rq = [rr_s[p] - _mm(a_rb[p], wm[p].astype(bf16)) for p in pairs]
    yv = [_mm(a_rk[p], vb[p]) - _mm(a_rb[p], uv[p].astype(bf16)) for p in pairs]
    wm_t = [(wm[p] * e_mid[:, sls[p]]).astype(bf16) for p in pairs]
    rq_t = [(rq[p] * e_mid[:, sls[p]]).astype(bf16) for p in pairs]
    bcb = [bc_s[p].astype(bf16) for p in pairs]
    omega = [_tn(wm_t[p], bcb[p]) for p in pairs]
    psi = [_tn(vb[p], kc_s[p].astype(bf16)) - _tn(uv[p].astype(bf16), bcb[p]) for p in pairs]
    phi = [jnp.where(eye, p_end[:, sls[p]], 0.0) - omega[p] for p in pairs]
    s0 = [s_ref[p] for p in pairs]
    ystack = [_nt(rq_t[p], s0[p].astype(bf16)) + yv[p] for p in pairs]
    for p in pairs:
        s_ref[p] = _mm(s0[p], phi[p], precision=HIGHEST) + psi[p]
    y = jnp.concatenate([ys[0:CHUNK, :] + ys[CHUNK:, :] for ys in ystack], axis=1)

    mean = _head_mean(y, bdm)
    yc = y - mean
    var = _head_mean(yc * yc, bdm)
    yn = yc * lax.rsqrt(var + RWKV_GN_EPS) * lng_ref[...] + lnb_ref[...]
    bonus = _head_mean(r * k2 * rk_ref[...], bdm) * HEAD_DIM * v
    o_ref[...] = ((yn + bonus) * g).astype(o_ref.dtype)


def _rwkv(zr, mu, w0, a0, k_k, k_a, r_k, ln_g, ln_b, w_up, a_up, g_up, bdm):
    b, length, _ = zr.shape
    vec = _const_spec((1, WIDTH))
    return pl.pallas_call(
        _rwkv_kernel,
        grid=(b, length // CHUNK),
        in_specs=[pl.BlockSpec((None, CHUNK, RWKV_COLS), lambda i, c: (i, c, 0)),
                  _const_spec((1, RWKV_COLS)), vec, vec, vec, vec, vec, vec, vec,
                  _const_spec((LANES, WIDTH)), _const_spec((LANES, WIDTH)), _const_spec((2 * LANES, WIDTH)),
                  _const_spec((WIDTH, WIDTH))],
        out_specs=pl.BlockSpec((None, CHUNK, WIDTH), lambda i, c: (i, c, 0)),
        out_shape=jax.ShapeDtypeStruct((b, length, WIDTH), bf16),
        scratch_shapes=[pltpu.VMEM((N_HEADS // 2, PAIR, PAIR), f32), pltpu.VMEM((1, RWKV_COLS), f32)],
        compiler_params=_cparams("parallel", "arbitrary"),
        name="rwkv",
    )(zr, mu, w0, a0, k_k, k_a, r_k, ln_g, ln_b, w_up, a_up, g_up, bdm)


def _merge_kernel(yf_ref, yr_ref, gate_ref, h_ref, wf_ref, wr_ref, wo_ref, o_ref):
    d = h_ref.shape[1]
    pf = _mm(yf_ref[...], wf_ref[...])
    pr = _mm(yr_ref[...], wr_ref[...])
    mixed = gate_ref[:, 0:d].astype(f32) * pf + gate_ref[:, d:2 * d].astype(f32) * pr
    o_ref[...] = h_ref[...] + _mm(mixed.astype(bf16), wo_ref[...])


def _merge(yf, yr, gates, hflat, wf, wr, wo):
    n, d = hflat.shape
    tm = _row_block(n)
    row = lambda c: pl.BlockSpec((tm, c), lambda i: (i, 0))
    return pl.pallas_call(
        _merge_kernel,
        grid=(n // tm,),
        in_specs=[row(WIDTH), row(WIDTH), row(2 * d), row(d),
                  _const_spec((WIDTH, d)), _const_spec((WIDTH, d)), _const_spec((d, d))],
        out_specs=row(d),
        out_shape=jax.ShapeDtypeStruct((n, d), f32),
        compiler_params=_cparams("parallel"),
        name="merge",
    )(yf, yr, gates, hflat, wf, wr, wo)


def _top_rows(s, k):
    n = s.shape[0]
    pos = lax.broadcasted_iota(i32, s.shape, 0)
    vals, outs = [], []
    for _ in range(k):
        m = jnp.max(s, axis=0, keepdims=True)
        first = jnp.min(jnp.where(s == m, pos, n), axis=0, keepdims=True)
        vals.append(m)
        outs.append(first)
        s = jnp.where(pos == first, -jnp.inf, s)
    return jnp.concatenate(vals, axis=0), jnp.concatenate(outs, axis=0)


def _staircase():
    pairs = [(a, b) for a in range(PEER_TOPK) for b in range(PEER_TOPK) if (a + 1) * (b + 1) <= PEER_TOPK]
    rows = -(-len(pairs) // 8) * 8
    sel = jnp.zeros((2, rows, PEER_TOPK), f32)
    r = jnp.arange(len(pairs))
    sel = sel.at[0, r, jnp.array([a for a, _ in pairs])].set(1.0)
    sel = sel.at[1, r, jnp.array([b for _, b in pairs])].set(1.0)
    return sel, len(pairs)


def _peer_sel_kernel(h_ref, g_ref, wq_ref, k1_ref, k2_ref, sel_ref, xn_ref, idx_ref, gate_ref, *, n_cand):
    x = h_ref[...]
    ms = jnp.mean(x * x, axis=-1, keepdims=True)
    xn = x * lax.rsqrt(ms + NORM_EPS) * g_ref[...]
    xn_ref[...] = xn
    qt = _nt(wq_ref[...], xn.astype(bf16))
    half = PEER_QDIM // 2
    sel_a, sel_b = sel_ref[0], sel_ref[1]
    pick = lambda sel, t: _mm(sel, t, precision=HIGHEST)
    s1 = jnp.concatenate([_mm(k1_ref[...], qt[h * PEER_QDIM:h * PEER_QDIM + half, :].astype(bf16))
                          for h in range(PEER_HEADS)], axis=1)
    s2 = jnp.concatenate([_mm(k2_ref[...], qt[h * PEER_QDIM + half:(h + 1) * PEER_QDIM, :].astype(bf16))
                          for h in range(PEER_HEADS)], axis=1)
    t1, i1 = _top_rows(s1, PEER_TOPK)
    t2, i2 = _top_rows(s2, PEER_TOPK)
    rows = lax.broadcasted_iota(i32, (sel_a.shape[0], PEER_HEADS * PEER_TOK), 0)
    cand_s = jnp.where(rows < n_cand, pick(sel_a, t1) + pick(sel_b, t2), -jnp.inf)
    cand_i = pick(sel_a, i1.astype(f32)) * N_KEYS + pick(sel_b, i2.astype(f32))
    top_s, top_p = _top_rows(cand_s, PEER_TOPK)
    top_i = jnp.concatenate(
        [jnp.max(jnp.where(rows == top_p[j:j + 1, :], cand_i, -1.0), axis=0, keepdims=True)
         for j in range(PEER_TOPK)], axis=0).astype(i32)
    e = jnp.exp(top_s - top_s[0:1, :])
    gate = e / jnp.sum(e, axis=0, keepdims=True)
    by_head = lambda a: jnp.concatenate([a[:, h * PEER_TOK:(h + 1) * PEER_TOK] for h in range(PEER_HEADS)], axis=0)
    idx_ref[...] = by_head(top_i).T * TABLE_ROWS
    gate_ref[...] = by_head(gate).T


def _peer_select(h2, g, wq_t, k1, k2, batch, seq, length):
    d = h2.shape[-1]
    nt = seq // PEER_TOK
    skip = (length - seq) // PEER_TOK
    nblk = batch * nt
    sel, n_cand = _staircase()
    blk = pl.BlockSpec((None, PEER_TOK, PEER_PAIRS), lambda i, j: (i * nt + j, 0, 0))
    return pl.pallas_call(
        functools.partial(_peer_sel_kernel, n_cand=n_cand),
        grid=(batch, nt),
        in_specs=[pl.BlockSpec((None, PEER_TOK, d), lambda i, j: (i, j + skip, 0)),
                  _const_spec((1, d)), _const_spec(wq_t.shape), _const_spec(k1.shape), _const_spec(k2.shape),
                  _const_spec(sel.shape)],
        out_specs=[pl.BlockSpec((PEER_TOK, d), lambda i, j: (i * nt + j, 0)), blk, blk],
        out_shape=[jax.ShapeDtypeStruct((nblk * PEER_TOK, d), f32),
                   jax.ShapeDtypeStruct((nblk, PEER_TOK, PEER_PAIRS), i32),
                   jax.ShapeDtypeStruct((nblk, PEER_TOK, PEER_PAIRS), f32)],
        compiler_params=_cparams("parallel", "parallel"),
        name="peer_sel",
    )(h2.reshape(batch, length, d), g, wq_t, k1, k2, sel)


def _pack_table(tab):
    e, d = tab.shape
    assert d == 2 * TABLE_ROWS * LANES
    t = lax.bitcast_convert_type(tab.astype(bf16), jnp.uint16).astype(jnp.uint32).reshape(e, 2, TABLE_ROWS, LANES)
    return lax.bitcast_convert_type((t[:, 0] << 16) | t[:, 1], i32).reshape(e * TABLE_ROWS, LANES)


def _unpack(word):
    return (lax.bitcast_convert_type(word & jnp.int32(-65536), f32), lax.bitcast_convert_type(word << 16, f32))


def _expert_rows(idx_ref, tab_ref, t, k):
    return tab_ref[pl.ds(pl.multiple_of(idx_ref[t, k], TABLE_ROWS), TABLE_ROWS), :]


def _peer_hid_kernel(idx_ref, x_ref, gate_ref, fold_ref, tab_ref, o_ref, part_a, part_b, dots_ref):
    parts = (part_a, part_b)
    ones = jnp.ones((8, LANES), bf16)

    def products(t, buf):
        xt = x_ref[t]
        xa, xb = xt[0:TABLE_ROWS, :], xt[TABLE_ROWS:, :]
        for k in range(PEER_PAIRS):
            hi, lo = _unpack(_expert_rows(idx_ref, tab_ref, t, k))
            parts[buf][TABLE_ROWS * k:TABLE_ROWS * (k + 1), :] = hi * xa + lo * xb

    def reduce(t, buf):
        lane_sums = _nt(ones, parts[buf][...].astype(bf16))
        dots_ref[pl.ds(t, 1), :] = lane_sums[0:1, :]

    part_b[...] = jnp.zeros_like(part_b)

    def body(i, carry):
        t0 = PEER_UNROLL * i
        for u in range(PEER_UNROLL):
            reduce(jnp.maximum(t0 + u - 1, 0), (u + 1) % 2)
            products(t0 + u, u % 2)
        return carry

    lax.fori_loop(0, PEER_TOK // PEER_UNROLL, body, 0)
    reduce(PEER_TOK - 1, (PEER_TOK - 1) % 2)
    hid = _mm(dots_ref[...], fold_ref[...], precision=HIGHEST)
    o_ref[...] = gate_ref[...] * (0.5 * hid * (1.0 + lax.erf(hid * (2.0 ** -0.5))))


def _two_term_rows(row):
    hi = row.astype(bf16).astype(f32)
    r = lax.broadcasted_iota(i32, (8, LANES), 0)
    return jnp.where(r == 0, hi, jnp.where(r == 1, row - hi, 0.0)).astype(bf16)


def _peer_out_kernel(idx_ref, c_ref, h_ref, tab_ref, o_ref, stage_a, stage_b):
    stage = (stage_a, stage_b)
    row = lax.broadcasted_iota(i32, (8, LANES), 0)

    def gather(t, buf):
        for k in range(PEER_PAIRS):
            stage[buf][TABLE_ROWS * k:TABLE_ROWS * (k + 1), :] = _expert_rows(idx_ref, tab_ref, t, k)

    def combine(t, buf):
        lhs = _two_term_rows(c_ref[pl.ds(t, 1), :])
        out = h_ref[t]
        for s in range(TABLE_ROWS):
            hi, lo = _unpack(stage[buf][pl.ds(s, PEER_PAIRS, stride=TABLE_ROWS), :])
            a = _mm(lhs, hi.astype(bf16))
            b = _mm(lhs, lo.astype(bf16))
            out = (out + jnp.where(row == s, a[0:1, :] + a[1:2, :], 0.0)
                   + jnp.where(row == TABLE_ROWS + s, b[0:1, :] + b[1:2, :], 0.0))
        o_ref[t] = out

    gather(0, 0)
    gather(1, 1)

    def body(i, carry):
        t0 = PEER_UNROLL * i
        for u in range(PEER_UNROLL):
            combine(t0 + u, u % 2)
            gather(jnp.minimum(t0 + u + 2, PEER_TOK - 1), u % 2)
        return carry

    lax.fori_loop(0, PEER_TOK // PEER_UNROLL, body, 0)


def _smem_blk(index_map):
    return pl.BlockSpec((None, PEER_TOK, PEER_PAIRS), index_map, memory_space=pltpu.SMEM)


def _table_spec(tab):
    return pl.BlockSpec(tab.shape, lambda *_: (0, 0), pipeline_mode=pl.Buffered(1))


def _peer_hidden(idx, xn, gate, utab):
    nblk = idx.shape[0]
    sub = xn.shape[1] // LANES
    x3 = xn.reshape(xn.shape[0], sub, LANES)
    vblk = pl.BlockSpec((None, PEER_TOK, PEER_PAIRS), lambda i: (i, 0, 0))
    rows = TABLE_ROWS * PEER_PAIRS
    fold = (jnp.arange(rows)[:, None] // TABLE_ROWS == jnp.arange(PEER_PAIRS)[None, :]).astype(f32)
    return pl.pallas_call(
        _peer_hid_kernel,
        grid=(nblk,),
        in_specs=[_smem_blk(lambda i: (i, 0, 0)), pl.BlockSpec((PEER_TOK, sub, LANES), lambda i: (i, 0, 0)), vblk,
                  _const_spec(fold.shape), _table_spec(utab)],
        out_specs=vblk,
        out_shape=jax.ShapeDtypeStruct((nblk, PEER_TOK, PEER_PAIRS), f32),
        scratch_shapes=[pltpu.VMEM((rows, LANES), f32), pltpu.VMEM((rows, LANES), f32),
                        pltpu.VMEM((PEER_TOK, rows), f32)],
        compiler_params=_cparams("arbitrary"),
        name="peer_hid",
    )(idx, x3, gate, fold, utab)


def _peer_output(idx, coef, h2, vtab, batch, seq, length):
    d = h2.shape[-1]
    sub = d // LANES
    nt = seq // PEER_TOK
    skip = (length - seq) // PEER_TOK
    h4 = h2.reshape(batch, length, sub, LANES)
    rows = TABLE_ROWS * PEER_PAIRS
    blk = lambda i, j: (i * nt + j, 0, 0)
    return pl.pallas_call(
        _peer_out_kernel,
        grid=(batch, nt),
        in_specs=[_smem_blk(blk), pl.BlockSpec((None, PEER_TOK, PEER_PAIRS), blk),
                  pl.BlockSpec((None, PEER_TOK, sub, LANES), lambda i, j: (i, j + skip, 0, 0)),
                  _table_spec(vtab)],
        out_specs=pl.BlockSpec((None, PEER_TOK, sub, LANES), lambda i, j: (i, j, 0, 0)),
        out_shape=jax.ShapeDtypeStruct((batch, seq, sub, LANES), f32),
        scratch_shapes=[pltpu.VMEM((rows, LANES), i32), pltpu.VMEM((rows, LANES), i32)],
        compiler_params=_cparams("arbitrary", "arbitrary"),
        name="peer_out",
    )(idx, coef, h4, vtab)


def _pad_cols(w, width):
    return jnp.pad(w, ((0, 0), (0, width - w.shape[1])))


def _pad_rows(w, height):
    return jnp.pad(w, ((0, height - w.shape[0]), (0, 0)))


def kernel(x, meta_tokens, norm1_g, w_in, fox_q_norm, fox_k_norm, fox_f_bias, rwkv_mu, rwkv_w0, rwkv_w_up, rwkv_a0, rwkv_a_up, rwkv_g_up, rwkv_k_k, rwkv_k_a, rwkv_r_k, rwkv_ln_g, rwkv_ln_b, w_branch_fox, w_branch_rwkv, w_out, norm2_g, peer_w_q, peer_sub_k1, peer_sub_k2, peer_u, peer_v):
    batch, seq, d = x.shape
    assert w_in.shape[0] == 1, "one layer"
    assert seq % PEER_TOK == 0 and d % LANES == 0
    t_real = seq + N_META
    pad = (-t_real) % ATTN_BLOCK
    length = t_real + pad
    ta = _attn_block(length)

    meta = jnp.broadcast_to(meta_tokens[None].astype(x.dtype), (batch, N_META, d))
    hpad = jnp.concatenate([jnp.zeros((batch, pad, d), x.dtype), meta, x], axis=1).reshape(batch * length, d)
    w = w_in[0]
    fox_w = 3 * WIDTH + N_HEADS
    o = fox_w
    lora = lambda lo, n, width: _pad_cols(w[:, lo:lo + n], width)
    w_all = jnp.concatenate([
        w[:, 0:3 * WIDTH], _pad_cols(w[:, 3 * WIDTH:fox_w], LANES),
        w[:, o:o + 3 * WIDTH],
        lora(o + 3 * WIDTH, DECAY_LORA, LANES),
        lora(o + 3 * WIDTH + DECAY_LORA, AAA_LORA, LANES),
        lora(o + 3 * WIDTH + DECAY_LORA + AAA_LORA, GATE_LORA, 2 * LANES),
        w[:, o + 3 * WIDTH + DECAY_LORA + AAA_LORA + GATE_LORA:],
    ], axis=1).astype(bf16)
    mu = rwkv_mu[0]
    mu_all = jnp.concatenate([
        mu[0:3 * WIDTH], jnp.pad(mu[3 * WIDTH:3 * WIDTH + DECAY_LORA], (0, LANES - DECAY_LORA)),
        jnp.pad(mu[3 * WIDTH + DECAY_LORA:3 * WIDTH + DECAY_LORA + AAA_LORA], (0, LANES - AAA_LORA)),
        jnp.pad(mu[3 * WIDTH + DECAY_LORA + AAA_LORA:], (0, 2 * LANES - GATE_LORA)),
    ])[None]
    head_of = jnp.arange(WIDTH) // HEAD_DIM
    bdm = (head_of[:, None] == head_of[None, :]).astype(f32) / HEAD_DIM
    vec = lambda p: p[0].reshape(1, -1)

    q, k, v, lf, zr, gates = _proj(hpad, vec(norm1_g), w_all, vec(fox_q_norm), vec(fox_k_norm),
                                   jnp.pad(fox_f_bias[0], (0, LANES - N_HEADS))[None], bdm.astype(bf16))
    negc = _neg_cumsum(lf.reshape(batch, length, LANES), pad, ta)
    three = lambda a: a.reshape(batch, length, a.shape[-1])
    y_fox = _attention(three(q), three(k), three(v), negc, ta)
    y_rwkv = _rwkv(three(zr), mu_all, vec(rwkv_w0), vec(rwkv_a0), vec(rwkv_k_k), vec(rwkv_k_a), vec(rwkv_r_k),
                   vec(rwkv_ln_g), vec(rwkv_ln_b),
                   _pad_rows(rwkv_w_up[0], LANES).astype(bf16), _pad_rows(rwkv_a_up[0], LANES).astype(bf16),
                   _pad_rows(rwkv_g_up[0], 2 * LANES).astype(bf16), bdm.astype(bf16))
    h2 = _merge(y_fox.reshape(-1, WIDTH), y_rwkv.reshape(-1, WIDTH), gates, hpad,
                w_branch_fox[0].astype(bf16), w_branch_rwkv[0].astype(bf16), w_out[0].astype(bf16))

    xn2, idx, gate = _peer_select(h2, vec(norm2_g), peer_w_q[0].T.astype(bf16),
                                  peer_sub_k1[0].astype(bf16), peer_sub_k2[0].astype(bf16), batch, seq, length)
    coef = _peer_hidden(idx, xn2, gate, _pack_table(peer_u[0]))
    out = _peer_output(idx, coef, h2, _pack_table(peer_v[0]), batch, seq, length)
    return out.reshape(batch, seq, d)
```

```python
import functools

import jax
import jax.numpy as jnp
from jax import lax
from jax.experimental import pallas as pl
from jax.experimental.pallas import tpu as pltpu

f32 = jnp.float32
bf16 = jnp.bfloat16
i32 = jnp.int32
HIGHEST = lax.Precision.HIGHEST

N_META = 16
HEAD_DIM = 64
N_HEADS = 8
WIDTH = N_HEADS * HEAD_DIM
DECAY_LORA, AAA_LORA, GATE_LORA = 64, 64, 160
ATTN_BLOCK = 128
PEER_HEADS, PEER_QDIM, N_KEYS, PEER_TOPK = 8, 256, 128, 16
NORM_EPS = 1e-6
RWKV_GN_EPS = 64e-5
MASK_VALUE = -1e30
LOG2_E = 1.4426950408889634

LANES = 128
VMEM_LIMIT_BYTES = 56 * 1024 * 1024

CHUNK = 64
PAIR = 2 * HEAD_DIM
PEER_TOK = 128
PEER_PAIRS = PEER_HEADS * PEER_TOPK
TABLE_ROWS = 4
FOX_COLS = 3 * WIDTH + LANES
RWKV_COLS = 3 * WIDTH + 2 * LANES + 2 * LANES


def _row_block(n):
    for t in (512, 384, 256, 128):
        if n % t == 0:
            return t
    raise ValueError(f"row count {n} is not a multiple of 128")


def _attn_block(length):
    return 384 if length % 384 == 0 else ATTN_BLOCK


def _cparams(*sem):
    return pltpu.CompilerParams(dimension_semantics=sem, vmem_limit_bytes=VMEM_LIMIT_BYTES)


def _const_spec(shape):
    nd = len(shape)
    return pl.BlockSpec(shape, lambda *_: (0,) * nd)


def _nt(a, b, **kw):
    return lax.dot_general(a, b, (((1,), (1,)), ((), ())), preferred_element_type=f32, **kw)


def _tn(a, b, **kw):
    return lax.dot_general(a, b, (((0,), (0,)), ((), ())), preferred_element_type=f32, **kw)


def _mm(a, b, **kw):
    return jnp.dot(a, b, preferred_element_type=f32, **kw)


def _softplus(y):
    return jnp.maximum(y, 0.0) + jnp.log1p(jnp.exp(-jnp.abs(y)))


def _sigmoid(y):
    return 1.0 / (1.0 + jnp.exp(-y))


def _three_term_mm(a, x):
    hi = x.astype(bf16)
    r1 = x - hi.astype(f32)
    mid = r1.astype(bf16)
    lo = (r1 - mid.astype(f32)).astype(bf16)
    return _mm(a, hi) + _mm(a, mid) + _mm(a, lo)


def _head_mean(x, bd):
    hi = x.astype(bf16)
    lo = (x - hi.astype(f32)).astype(bf16)
    return _mm(hi, bd) + _mm(lo, bd)


def _proj_kernel(x_ref, g_ref, w_ref, qn_ref, kn_ref, fb_ref, bd_ref,
                 q_ref, k_ref, v_ref, lf_ref, zr_ref, gate_ref):
    x = x_ref[...]
    ms = jnp.mean(x * x, axis=-1, keepdims=True)
    xn = (x * lax.rsqrt(ms + NORM_EPS) * g_ref[...]).astype(bf16)

    def head_norm(z, gain):
        msq = _mm((z * z).astype(bf16), bd_ref[...])
        return z * lax.rsqrt(msq + NORM_EPS) * gain

    zq = _mm(xn, w_ref[:, 0:WIDTH])
    q_ref[...] = (head_norm(zq, qn_ref[...]) * (HEAD_DIM ** -0.5 * LOG2_E)).astype(bf16)
    zk = _mm(xn, w_ref[:, WIDTH:2 * WIDTH])
    k_ref[...] = head_norm(zk, kn_ref[...]).astype(bf16)
    v_ref[...] = _mm(xn, w_ref[:, 2 * WIDTH:3 * WIDTH]).astype(bf16)
    zf = _mm(xn, w_ref[:, 3 * WIDTH:FOX_COLS]) + fb_ref[...]
    lf_ref[...] = -_softplus(-zf) * LOG2_E
    zr_ref[...] = _mm(xn, w_ref[:, FOX_COLS:FOX_COLS + RWKV_COLS])
    zg = _mm(xn, w_ref[:, FOX_COLS + RWKV_COLS:])
    gate_ref[...] = _sigmoid(zg).astype(bf16)


def _proj(hflat, g, w, qn, kn, fb, bd):
    n, d = hflat.shape
    tm = _row_block(n)
    ncol = w.shape[1]
    ngate = ncol - FOX_COLS - RWKV_COLS
    row = lambda c: pl.BlockSpec((tm, c), lambda i: (i, 0))
    return pl.pallas_call(
        _proj_kernel,
        grid=(n // tm,),
        in_specs=[row(d), _const_spec((1, d)), _const_spec((d, ncol)), _const_spec((1, WIDTH)),
                  _const_spec((1, WIDTH)), _const_spec((1, LANES)), _const_spec((WIDTH, WIDTH))],
        out_specs=[row(WIDTH), row(WIDTH), row(WIDTH), row(LANES), row(RWKV_COLS), row(ngate)],
        out_shape=[jax.ShapeDtypeStruct((n, WIDTH), bf16)] * 3
        + [jax.ShapeDtypeStruct((n, LANES), f32), jax.ShapeDtypeStruct((n, RWKV_COLS), f32),
           jax.ShapeDtypeStruct((n, ngate), bf16)],
        compiler_params=_cparams("parallel"),
        name="proj",
    )(hflat, g, w, qn, kn, fb, bd)


def _cumsum_kernel(lf_ref, o_ref, *, pad, ta):
    length = lf_ref.shape[0]
    r = lax.broadcasted_iota(i32, (LANES, LANES), 0)
    c = lax.broadcasted_iota(i32, (LANES, LANES), 1)
    tri = (r >= c).astype(f32)
    lane = lax.broadcasted_iota(i32, (N_HEADS, LANES), 1)
    carry = jnp.zeros((1, LANES), f32)
    per = ta // LANES
    for blk in range(length // LANES):
        cs = _mm(tri, lf_ref[blk * LANES:(blk + 1) * LANES, :], precision=HIGHEST) + carry
        carry = cs[LANES - 1:LANES, :]
        ct = cs.T[0:N_HEADS, :]
        neg = jnp.where(lane + blk * LANES >= pad, -ct, MASK_VALUE)
        j, off = blk // per, (blk % per) * LANES
        for h in range(N_HEADS):
            o_ref[h, j:j + 1, off:off + LANES] = neg[h:h + 1, :]


def _neg_cumsum(lf, pad, ta):
    b, length, _ = lf.shape
    nb = length // ta
    return pl.pallas_call(
        functools.partial(_cumsum_kernel, pad=pad, ta=ta),
        grid=(b,),
        in_specs=[pl.BlockSpec((None, length, LANES), lambda i: (i, 0, 0))],
        out_specs=pl.BlockSpec((None, N_HEADS, nb, ta), lambda i: (i, 0, 0, 0)),
        out_shape=jax.ShapeDtypeStruct((b, N_HEADS, nb, ta), f32),
        compiler_params=_cparams("parallel"),
        name="cumsum",
    )(lf)


def _attn_kernel(q_ref, k_ref, v_ref, nc_ref, o_ref, *, ta):
    qi = pl.program_id(2)
    lane = lax.broadcasted_iota(i32, (1, PAIR), 1)
    row = lax.broadcasted_iota(i32, (ATTN_BLOCK, ta), 0)
    col = lax.broadcasted_iota(i32, (ATTN_BLOCK, ta), 1)
    nrow = ta // ATTN_BLOCK
    pcs = [(h, r) for h in range(2) for r in range(nrow)]
    qp = []
    for h, r in pcs:
        qr = q_ref[r * ATTN_BLOCK:(r + 1) * ATTN_BLOCK, :]
        qp.append(jnp.where((lane // HEAD_DIM) == h, qr, jnp.zeros_like(qr)))

    def block(j, carries, diagonal):
        start = pl.multiple_of(j * ta, ta)
        ks = k_ref[pl.ds(start, ta), :]
        vs = v_ref[pl.ds(start, ta), :]
        s = [_nt(qp[i], ks) + nc_ref[h, pl.ds(j, 1), :] for i, (h, r) in enumerate(pcs)]
        if diagonal:
            s = [jnp.where(col <= row + r * ATTN_BLOCK, s[i], MASK_VALUE) for i, (h, r) in enumerate(pcs)]
        ids = range(len(pcs))
        m_new = [jnp.maximum(carries[i][0], jnp.max(s[i], axis=-1, keepdims=True)) for i in ids]
        p = [jnp.exp2(s[i] - m_new[i]) for i in ids]
        alpha = [jnp.exp2(carries[i][0] - m_new[i]) for i in ids]
        l = [alpha[i] * carries[i][1] + jnp.sum(p[i], axis=-1, keepdims=True) for i in ids]
        pv = [_mm(p[i].astype(bf16), vs) for i in ids]
        acc = [alpha[i] * carries[i][2] + pv[i] for i in ids]
        return tuple((m_new[i], l[i], acc[i]) for i in ids)

    init = (jnp.full((ATTN_BLOCK, 1), MASK_VALUE, f32), jnp.zeros((ATTN_BLOCK, 1), f32),
            jnp.zeros((ATTN_BLOCK, PAIR), f32))
    carries = lax.fori_loop(0, qi, lambda j, c: block(j, c, False), (init,) * len(pcs))
    final = block(qi, carries, True)
    out = [jnp.concatenate([final[h * nrow + r][2] / final[h * nrow + r][1] for r in range(nrow)], axis=0)
           for h in range(2)]
    o_ref[...] = jnp.where((lane // HEAD_DIM) == 0, out[0], out[1]).astype(o_ref.dtype)


def _attention(q, k, v, negc, ta):
    b, length, _ = q.shape
    nb = length // ta
    npair = N_HEADS // 2
    return pl.pallas_call(
        functools.partial(_attn_kernel, ta=ta),
        grid=(b, npair, nb),
        in_specs=[pl.BlockSpec((None, ta, PAIR), lambda i, p, j: (i, j, p)),
                  pl.BlockSpec((None, length, PAIR), lambda i, p, j: (i, 0, p)),
                  pl.BlockSpec((None, length, PAIR), lambda i, p, j: (i, 0, p)),
                  pl.BlockSpec((None, 2, nb, ta), lambda i, p, j: (i, p, 0, 0))],
        out_specs=pl.BlockSpec((None, ta, PAIR), lambda i, p, j: (i, j, p)),
        out_shape=jax.ShapeDtypeStruct((b, length, WIDTH), bf16),
        compiler_params=_cparams("parallel", "parallel", "arbitrary"),
        name="attn",
    )(q, k, v, negc)


def _stack(x, lane_head):
    zero = jnp.zeros_like(x)
    return jnp.concatenate([jnp.where(lane_head == 0, x, zero), jnp.where(lane_head == 1, x, zero)], axis=0)


def _unit_lower_inverse(ns, r, c):
    eye = (r == c).astype(f32)
    first = ((r ^ c) == 1) & ((r & 1) == 1)
    ts = [(eye - jnp.where(first, n, 0.0)).astype(bf16) for n in ns]
    s = 2
    while s < CHUNK:
        sel = ((r // (2 * s)) == (c // (2 * s))) & ((r & s) != 0) & ((c & s) == 0)
        low = [jnp.where(sel, n, 0.0).astype(bf16) for n in ns]
        tn = [_mm(t, lo).astype(bf16) for t, lo in zip(ts, low)]
        ts = [(t.astype(f32) - _mm(x, t)).astype(bf16) for t, x in zip(ts, tn)]
        s *= 2
    return ts


def _rwkv_kernel(z_ref, mu_ref, w0_ref, a0_ref, kk_ref, ka_ref, rk_ref, lng_ref, lnb_ref,
                 wup_ref, aup_ref, gup_ref, bdm_ref, o_ref, s_ref, prev_ref):
    ci = pl.program_id(1)

    @pl.when(ci == 0)
    def _():
        s_ref[...] = jnp.zeros_like(s_ref)
        prev_ref[...] = jnp.zeros_like(prev_ref)

    z = z_ref[...]
    rows = lax.broadcasted_iota(i32, (CHUNK, 1), 0)
    zprev = jnp.where(rows == 0, prev_ref[...], pltpu.roll(z, 1, 0))
    prev_ref[...] = z[CHUNK - 1:CHUNK, :]
    zs = z + mu_ref[...] * (zprev - z)
    r = zs[:, 0:WIDTH]
    k = zs[:, WIDTH:2 * WIDTH]
    v = zs[:, 2 * WIDTH:3 * WIDTH]
    o1 = 3 * WIDTH
    wd = zs[:, o1:o1 + LANES]
    ad = zs[:, o1 + LANES:o1 + 2 * LANES]
    gd = zs[:, o1 + 2 * LANES:o1 + 4 * LANES]

    w = -_softplus(-(w0_ref[...] + _mm(jnp.tanh(wd).astype(bf16), wup_ref[...]))) - 0.5
    ld = -jnp.exp(w)
    a = _sigmoid(a0_ref[...] + _mm(ad.astype(bf16), aup_ref[...]))
    g = _mm(_sigmoid(gd).astype(bf16), gup_ref[...])
    bdm = bdm_ref[...]
    kk = k * kk_ref[...]
    nrm = jnp.sqrt(_mm((kk * kk).astype(bf16), bdm) * HEAD_DIM)
    kk = kk / jnp.maximum(nrm, 1e-12)
    k2 = k * (1.0 + (a - 1.0) * ka_ref[...])
    b = kk * a

    tr = lax.broadcasted_iota(i32, (CHUNK, CHUNK), 0)
    tc = lax.broadcasted_iota(i32, (CHUNK, CHUNK), 1)
    lc = _three_term_mm((tr >= tc).astype(bf16), ld)
    lp = lc - ld
    mid = lc[CHUNK // 2 - 1:CHUNK // 2, :]
    tot = lc[CHUNK - 1:CHUNK, :]
    kq = kk * jnp.exp(lp - mid)
    rr = r * jnp.exp(lc - mid)
    e_after = jnp.exp(mid - lc)
    kh = k2 * e_after
    bh = b * e_after
    e_end = jnp.exp(tot - lc)
    kc = k2 * e_end
    bc = b * e_end
    e_mid = jnp.exp(mid)
    p_end = jnp.exp(tot)

    sr = lax.broadcasted_iota(i32, (PAIR, PAIR), 0)
    sc = lax.broadcasted_iota(i32, (PAIR, PAIR), 1)
    same = (sr // CHUNK) == (sc // CHUNK)
    strict = same & ((sr % CHUNK) > (sc % CHUNK))
    incl = same & ((sr % CHUNK) >= (sc % CHUNK))
    lane_head = lax.broadcasted_iota(i32, (1, PAIR), 1) // HEAD_DIM

    pairs = range(N_HEADS // 2)
    sls = [slice(p * PAIR, (p + 1) * PAIR) for p in pairs]
    st = lambda x: [_stack(x[:, sl], lane_head) for sl in sls]
    kq_s, rr_s, kh_s, bh_s, v_s, kc_s, bc_s = (st(x) for x in (kq, rr, kh, bh, v, kc, bc))
    amat = [_nt(jnp.concatenate([kq_s[p], rr_s[p]], axis=0).astype(bf16),
                jnp.concatenate([kh_s[p], bh_s[p]], axis=0).astype(bf16)) for p in pairs]
    a_kk = [jnp.where(strict, amat[p][0:PAIR, 0:PAIR], 0.0).astype(bf16) for p in pairs]
    a_kb = [jnp.where(strict, amat[p][0:PAIR, PAIR:], 0.0) for p in pairs]
    a_rk = [jnp.where(incl, amat[p][PAIR:, 0:PAIR], 0.0).astype(bf16) for p in pairs]
    a_rb = [jnp.where(incl, amat[p][PAIR:, PAIR:], 0.0).astype(bf16) for p in pairs]
    t = _unit_lower_inverse(a_kb, sr, sc)
    vb = [v_s[p].astype(bf16) for p in pairs]
    av = [_mm(a_kk[p], vb[p]).astype(bf16) for p in pairs]
    wm = [_mm(t[p], kq_s[p].astype(bf16)) for p in pairs]
    uv = [_mm(t[p], av[p]) for p in pairs]
    rq = [rr_s[p] - _mm(a_rb[p], wm[p].astype(bf16)) for p in pairs]
    yv = [_mm(a_rk[p], vb[p]) - _mm(a_rb[p], uv[p].astype(bf16)) for p in pairs]
    wm_t = [(wm[p] * e_mid[:, sls[p]]).astype(bf16) for p in pairs]
    rq_t = [(rq[p] * e_mid[:, sls[p]]).astype(bf16) for p in pairs]
    bcb = [bc_s[p].astype(bf16) for p in pairs]
    omega = [_tn(wm_t[p], bcb[p]).astype(bf16) for p in pairs]
    psi = [_tn(vb[p], kc_s[p].astype(bf16)) - _tn(uv[p].astype(bf16), bcb[p]) for p in pairs]
    s0 = [s_ref[p] for p in pairs]
    s_hi = [s0[p].astype(bf16) for p in pairs]
    s_lo = [(s0[p] - s_hi[p].astype(f32)).astype(bf16) for p in pairs]
    ystack = [_nt(rq_t[p], s_hi[p]) + yv[p] for p in pairs]
    for p in pairs:
        s_ref[p] = s0[p] * p_end[:, sls[p]] - (_mm(s_hi[p], omega[p]) + _mm(s_lo[p], omega[p])) + psi[p]
    y = jnp.concatenate([ys[0:CHUNK, :] + ys[CHUNK:, :] for ys in ystack], axis=1)

    mean = _head_mean(y, bdm)
    yc = y - mean
    var = _head_mean(yc * yc, bdm)
    yn = yc * lax.rsqrt(var + RWKV_GN_EPS) * lng_ref[...] + lnb_ref[...]
    bonus = _mm((r * k2 * rk_ref[...]).astype(bf16), bdm) * HEAD_DIM * v
    o_ref[...] = ((yn + bonus) * g).astype(o_ref.dtype)


def _rwkv(zr, mu, w0, a0, k_k, k_a, r_k, ln_g, ln_b, w_up, a_up, g_up, bdm):
    b, length, _ = zr.shape
    vec = _const_spec((1, WIDTH))
    return pl.pallas_call(
        _rwkv_kernel,
        grid=(b, length // CHUNK),
        in_specs=[pl.BlockSpec((None, CHUNK, RWKV_COLS), lambda i, c: (i, c, 0)),
                  _const_spec((1, RWKV_COLS)), vec, vec, vec, vec, vec, vec, vec,
                  _const_spec((LANES, WIDTH)), _const_spec((LANES, WIDTH)), _const_spec((2 * LANES, WIDTH)),
                  _const_spec((WIDTH, WIDTH))],
        out_specs=pl.BlockSpec((None, CHUNK, WIDTH), lambda i, c: (i, c, 0)),
        out_shape=jax.ShapeDtypeStruct((b, length, WIDTH), bf16),
        scratch_shapes=[pltpu.VMEM((N_HEADS // 2, PAIR, PAIR), f32), pltpu.VMEM((1, RWKV_COLS), f32)],
        compiler_params=_cparams("parallel", "arbitrary"),
        name="rwkv",
    )(zr, mu, w0, a0, k_k, k_a, r_k, ln_g, ln_b, w_up, a_up, g_up, bdm)


def _merge_kernel(yf_ref, yr_ref, gate_ref, h_ref, wf_ref, wr_ref, wo_ref, o_ref):
    d = h_ref.shape[1]
    pf = _mm(yf_ref[...], wf_ref[...])
    pr = _mm(yr_ref[...], wr_ref[...])
    mixed = gate_ref[:, 0:d].astype(f32) * pf + gate_ref[:, d:2 * d].astype(f32) * pr
    o_ref[...] = h_ref[...] + _mm(mixed.astype(bf16), wo_ref[...])


def _merge(yf, yr, gates, hflat, wf, wr, wo):
    n, d = hflat.shape
    tm = _row_block(n)
    row = lambda c: pl.BlockSpec((tm, c), lambda i: (i, 0))
    return pl.pallas_call(
        _merge_kernel,
        grid=(n // tm,),
        in_specs=[row(WIDTH), row(WIDTH), row(2 * d), row(d),
                  _const_spec((WIDTH, d)), _const_spec((WIDTH, d)), _const_spec((d, d))],
        out_specs=row(d),
        out_shape=jax.ShapeDtypeStruct((n, d), f32),
        compiler_params=_cparams("parallel"),
        name="merge",
    )(yf, yr, gates, hflat, wf, wr, wo)


def _top_rows(s, k):
    n = s.shape[0]
    pos = lax.broadcasted_iota(i32, s.shape, 0)
    vals, outs = [], []
    for _ in range(k):
        m = jnp.max(s, axis=0, keepdims=True)
        first = jnp.min(jnp.where(s == m, pos, n), axis=0, keepdims=True)
        vals.append(m)
        outs.append(first)
        s = jnp.where(pos == first, -jnp.inf, s)
    return jnp.concatenate(vals, axis=0), jnp.concatenate(outs, axis=0)


def _staircase():
    pairs = [(a, b) for a in range(PEER_TOPK) for b in range(PEER_TOPK) if (a + 1) * (b + 1) <= PEER_TOPK]
    rows = -(-len(pairs) // 8) * 8
    sel = jnp.zeros((2, rows, PEER_TOPK), f32)
    r = jnp.arange(len(pairs))
    sel = sel.at[0, r, jnp.array([a for a, _ in pairs])].set(1.0)
    sel = sel.at[1, r, jnp.array([b for _, b in pairs])].set(1.0)
    return sel, len(pairs)


def _peer_sel_kernel(h_ref, g_ref, wq_ref, k1_ref, k2_ref, sel_ref, xn_ref, idx_ref, gate_ref, *, n_cand):
    x = h_ref[...]
    ms = jnp.mean(x * x, axis=-1, keepdims=True)
    xn = x * lax.rsqrt(ms + NORM_EPS) * g_ref[...]
    xn_ref[...] = xn.reshape(xn_ref.shape)
    qt = _nt(wq_ref[...], xn.astype(bf16))
    half = PEER_QDIM // 2
    sel_a, sel_b = sel_ref[0], sel_ref[1]
    pick = lambda sel, t: _mm(sel, t, precision=HIGHEST)
    s1 = jnp.concatenate([_mm(k1_ref[...], qt[h * PEER_QDIM:h * PEER_QDIM + half, :].astype(bf16))
                          for h in range(PEER_HEADS)], axis=1)
    s2 = jnp.concatenate([_mm(k2_ref[...], qt[h * PEER_QDIM + half:(h + 1) * PEER_QDIM, :].astype(bf16))
                          for h in range(PEER_HEADS)], axis=1)
    t1, i1 = _top_rows(s1, PEER_TOPK)
    t2, i2 = _top_rows(s2, PEER_TOPK)
    rows = lax.broadcasted_iota(i32, (sel_a.shape[0], PEER_HEADS * PEER_TOK), 0)
    cand_s = jnp.where(rows < n_cand, pick(sel_a, t1) + pick(sel_b, t2), -jnp.inf)
    cand_i = pick(sel_a, i1.astype(f32)) * N_KEYS + pick(sel_b, i2.astype(f32))
    top_s, top_p = _top_rows(cand_s, PEER_TOPK)
    top_i = jnp.concatenate(
        [jnp.max(jnp.where(rows == top_p[j:j + 1, :], cand_i, -1.0), axis=0, keepdims=True)
         for j in range(PEER_TOPK)], axis=0).astype(i32)
    e = jnp.exp(top_s - top_s[0:1, :])
    gate = e / jnp.sum(e, axis=0, keepdims=True)
    by_head = lambda a: jnp.concatenate([a[:, h * PEER_TOK:(h + 1) * PEER_TOK] for h in range(PEER_HEADS)], axis=0)
    idx_ref[...] = by_head(top_i).T * TABLE_ROWS
    gate_ref[...] = by_head(gate).T


def _peer_select(h2, g, wq_t, k1, k2, batch, seq, length):
    d = h2.shape[-1]
    nt = seq // PEER_TOK
    skip = (length - seq) // PEER_TOK
    nblk = batch * nt
    sel, n_cand = _staircase()
    blk = pl.BlockSpec((None, PEER_TOK, PEER_PAIRS), lambda i, j: (i * nt + j, 0, 0))
    return pl.pallas_call(
        functools.partial(_peer_sel_kernel, n_cand=n_cand),
        grid=(batch, nt),
        in_specs=[pl.BlockSpec((None, PEER_TOK, d), lambda i, j: (i, j + skip, 0)),
                  _const_spec((1, d)), _const_spec(wq_t.shape), _const_spec(k1.shape), _const_spec(k2.shape),
                  _const_spec(sel.shape)],
        out_specs=[pl.BlockSpec((PEER_TOK, d // LANES, LANES), lambda i, j: (i * nt + j, 0, 0)), blk, blk],
        out_shape=[jax.ShapeDtypeStruct((nblk * PEER_TOK, d // LANES, LANES), f32),
                   jax.ShapeDtypeStruct((nblk, PEER_TOK, PEER_PAIRS), i32),
                   jax.ShapeDtypeStruct((nblk, PEER_TOK, PEER_PAIRS), f32)],
        compiler_params=_cparams("parallel", "parallel"),
        name="peer_sel",
    )(h2.reshape(batch, length, d), g, wq_t, k1, k2, sel)


def _pack_table(tab):
    e, d = tab.shape
    assert d == 2 * TABLE_ROWS * LANES
    t = lax.bitcast_convert_type(tab.astype(bf16), jnp.uint16).astype(jnp.uint32).reshape(e, 2, TABLE_ROWS, LANES)
    return lax.bitcast_convert_type((t[:, 0] << 16) | t[:, 1], i32).reshape(e * TABLE_ROWS, LANES)


def _unpack(word):
    return (lax.bitcast_convert_type(word & jnp.int32(-65536), f32), lax.bitcast_convert_type(word << 16, f32))


def _expert_rows(idx_ref, tab_ref, t, k):
    return tab_ref[pl.ds(pl.multiple_of(idx_ref[t, k], TABLE_ROWS), TABLE_ROWS), :]


def _peer_hid_kernel(idx_ref, x_ref, gate_ref, fold_ref, tab_ref, o_ref, part_a, part_b, dots_ref):
    parts = (part_a, part_b)
    ones = jnp.ones((8, LANES), bf16)

    def products(t, buf):
        xt = x_ref[t]
        xa, xb = xt[0:TABLE_ROWS, :], xt[TABLE_ROWS:, :]
        for k in range(PEER_PAIRS):
            hi, lo = _unpack(_expert_rows(idx_ref, tab_ref, t, k))
            parts[buf][TABLE_ROWS * k:TABLE_ROWS * (k + 1), :] = hi * xa + lo * xb

    def reduce(t, buf):
        lane_sums = _nt(ones, parts[buf][...].astype(bf16))
        dots_ref[t:t + 1, :] = lane_sums[0:1, :]

    for t in range(PEER_TOK):
        if t:
            reduce(t - 1, (t - 1) % 2)
        products(t, t % 2)
    reduce(PEER_TOK - 1, (PEER_TOK - 1) % 2)
    hid = _mm(dots_ref[...], fold_ref[...], precision=HIGHEST)
    o_ref[...] = gate_ref[...] * (0.5 * hid * (1.0 + lax.erf(hid * (2.0 ** -0.5))))


def _two_term_rows(row):
    hi = row.astype(bf16).astype(f32)
    r = lax.broadcasted_iota(i32, (8, LANES), 0)
    return jnp.where(r == 0, hi, jnp.where(r == 1, row - hi, 0.0)).astype(bf16)


def _peer_out_kernel(idx_ref, c_ref, h_ref, tab_ref, o_ref, stage_a, stage_b, acc_ref):
    stage = (stage_a, stage_b)
    row = lax.broadcasted_iota(i32, (8, LANES), 0)
    acc_ref[...] = h_ref[...].reshape(acc_ref.shape)

    def gather(t, buf):
        for k in range(PEER_PAIRS):
            stage[buf][TABLE_ROWS * k:TABLE_ROWS * (k + 1), :] = _expert_rows(idx_ref, tab_ref, t, k)

    def combine(t, buf):
        lhs = _two_term_rows(c_ref[t:t + 1, :])
        out = acc_ref[t]
        for s in range(TABLE_ROWS):
            hi, lo = _unpack(stage[buf][pl.ds(s, PEER_PAIRS, stride=TABLE_ROWS), :])
            a = _mm(lhs, hi.astype(bf16))
            b = _mm(lhs, lo.astype(bf16))
            out = (out + jnp.where(row == s, a[0:1, :] + a[1:2, :], 0.0)
                   + jnp.where(row == TABLE_ROWS + s, b[0:1, :] + b[1:2, :], 0.0))
        acc_ref[t] = out

    gather(0, 0)
    gather(1, 1)
    for t in range(PEER_TOK):
        combine(t, t % 2)
        if t + 2 < PEER_TOK:
            gather(t + 2, t % 2)
    o_ref[...] = acc_ref[...].reshape(o_ref.shape)


def _smem_blk(index_map):
    return pl.BlockSpec((None, PEER_TOK, PEER_PAIRS), index_map, memory_space=pltpu.SMEM,
                        pipeline_mode=pl.Buffered(1))


def _table_spec(tab):
    return pl.BlockSpec(tab.shape, lambda *_: (0, 0), pipeline_mode=pl.Buffered(1))


def _peer_hidden(idx, x3, gate, utab):
    nblk = idx.shape[0]
    sub = x3.shape[1]
    vblk = pl.BlockSpec((None, PEER_TOK, PEER_PAIRS), lambda i: (i, 0, 0))
    rows = TABLE_ROWS * PEER_PAIRS
    fold = (jnp.arange(rows)[:, None] // TABLE_ROWS == jnp.arange(PEER_PAIRS)[None, :]).astype(f32)
    return pl.pallas_call(
        _peer_hid_kernel,
        grid=(nblk,),
        in_specs=[_smem_blk(lambda i: (i, 0, 0)), pl.BlockSpec((PEER_TOK, sub, LANES), lambda i: (i, 0, 0)), vblk,
                  _const_spec(fold.shape), _table_spec(utab)],
        out_specs=vblk,
        out_shape=jax.ShapeDtypeStruct((nblk, PEER_TOK, PEER_PAIRS), f32),
        scratch_shapes=[pltpu.VMEM((rows, LANES), f32), pltpu.VMEM((rows, LANES), f32),
                        pltpu.VMEM((PEER_TOK, rows), f32)],
        compiler_params=_cparams("arbitrary"),
        name="peer_hid",
    )(idx, x3, gate, fold, utab)


def _peer_output(idx, coef, h2, vtab, batch, seq, length):
    d = h2.shape[-1]
    sub = d // LANES
    nt = seq // PEER_TOK
    skip = (length - seq) // PEER_TOK
    rows = TABLE_ROWS * PEER_PAIRS
    blk = lambda i, j: (i * nt + j, 0, 0)
    return pl.pallas_call(
        _peer_out_kernel,
        grid=(batch, nt),
        in_specs=[_smem_blk(blk), pl.BlockSpec((None, PEER_TOK, PEER_PAIRS), blk),
                  pl.BlockSpec((None, PEER_TOK, d), lambda i, j: (i, j + skip, 0)),
                  _table_spec(vtab)],
        out_specs=pl.BlockSpec((None, PEER_TOK, d), lambda i, j: (i, j, 0)),
        out_shape=jax.ShapeDtypeStruct((batch, seq, d), f32),
        scratch_shapes=[pltpu.VMEM((rows, LANES), i32), pltpu.VMEM((rows, LANES), i32),
                        pltpu.VMEM((PEER_TOK, sub, LANES), f32)],
        compiler_params=_cparams("arbitrary", "arbitrary"),
        name="peer_out",
    )(idx, coef, h2.reshape(batch, length, d), vtab)


def _pad_cols(w, width):
    return jnp.pad(w, ((0, 0), (0, width - w.shape[1])))


def _pad_rows(w, height):
    return jnp.pad(w, ((0, height - w.shape[0]), (0, 0)))


def kernel(x, meta_tokens, norm1_g, w_in, fox_q_norm, fox_k_norm, fox_f_bias, rwkv_mu, rwkv_w0, rwkv_w_up, rwkv_a0, rwkv_a_up, rwkv_g_up, rwkv_k_k, rwkv_k_a, rwkv_r_k, rwkv_ln_g, rwkv_ln_b, w_branch_fox, w_branch_rwkv, w_out, norm2_g, peer_w_q, peer_sub_k1, peer_sub_k2, peer_u, peer_v):
    batch, seq, d = x.shape
    assert w_in.shape[0] == 1, "one layer"
    assert seq % PEER_TOK == 0 and d % LANES == 0
    t_real = seq + N_META
    pad = (-t_real) % ATTN_BLOCK
    length = t_real + pad
    ta = _attn_block(length)

    meta = jnp.broadcast_to(meta_tokens[None].astype(x.dtype), (batch, N_META, d))
    hpad = jnp.concatenate([jnp.zeros((batch, pad, d), x.dtype), meta, x], axis=1).reshape(batch * length, d)
    w = w_in[0]
    fox_w = 3 * WIDTH + N_HEADS
    o = fox_w
    lora = lambda lo, n, width: _pad_cols(w[:, lo:lo + n], width)
    w_all = jnp.concatenate([
        w[:, 0:3 * WIDTH], _pad_cols(w[:, 3 * WIDTH:fox_w], LANES),
        w[:, o:o + 3 * WIDTH],
        lora(o + 3 * WIDTH, DECAY_LORA, LANES),
        lora(o + 3 * WIDTH + DECAY_LORA, AAA_LORA, LANES),
        lora(o + 3 * WIDTH + DECAY_LORA + AAA_LORA, GATE_LORA, 2 * LANES),
        w[:, o + 3 * WIDTH + DECAY_LORA + AAA_LORA + GATE_LORA:],
    ], axis=1).astype(bf16)
    mu = rwkv_mu[0]
    mu_all = jnp.concatenate([
        mu[0:3 * WIDTH], jnp.pad(mu[3 * WIDTH:3 * WIDTH + DECAY_LORA], (0, LANES - DECAY_LORA)),
        jnp.pad(mu[3 * WIDTH + DECAY_LORA:3 * WIDTH + DECAY_LORA + AAA_LORA], (0, LANES - AAA_LORA)),
        jnp.pad(mu[3 * WIDTH + DECAY_LORA + AAA_LORA:], (0, 2 * LANES - GATE_LORA)),
    ])[None]
    head_of = jnp.arange(WIDTH) // HEAD_DIM
    bdm = (head_of[:, None] == head_of[None, :]).astype(f32) / HEAD_DIM
    vec = lambda p: p[0].reshape(1, -1)

    q, k, v, lf, zr, gates = _proj(hpad, vec(norm1_g), w_all, vec(fox_q_norm), vec(fox_k_norm),
                                   jnp.pad(fox_f_bias[0], (0, LANES - N_HEADS))[None], bdm.astype(bf16))
    negc = _neg_cumsum(lf.reshape(batch, length, LANES), pad, ta)
    three = lambda a: a.reshape(batch, length, a.shape[-1])
    y_fox = _attention(three(q), three(k), three(v), negc, ta)
    y_rwkv = _rwkv(three(zr), mu_all, vec(rwkv_w0), vec(rwkv_a0), vec(rwkv_k_k), vec(rwkv_k_a), vec(rwkv_r_k),
                   vec(rwkv_ln_g), vec(rwkv_ln_b),
                   _pad_rows(rwkv_w_up[0], LANES).astype(bf16), _pad_rows(rwkv_a_up[0], LANES).astype(bf16),
                   _pad_rows(rwkv_g_up[0], 2 * LANES).astype(bf16), bdm.astype(bf16))
    h2 = _merge(y_fox.reshape(-1, WIDTH), y_rwkv.reshape(-1, WIDTH), gates, hpad,
                w_branch_fox[0].astype(bf16), w_branch_rwkv[0].astype(bf16), w_out[0].astype(bf16))

    xn2, idx, gate = _peer_select(h2, vec(norm2_g), peer_w_q[0].T.astype(bf16),
                                  peer_sub_k1[0].astype(bf16), peer_sub_k2[0].astype(bf16), batch, seq, length)
    coef = _peer_hidden(idx, xn2, gate, _pack_table(peer_u[0]))
    out = _peer_output(idx, coef, h2, _pack_table(peer_v[0]), batch, seq, length)
    return out.reshape(batch, seq, d)
```

```python
import functools

import jax
import jax.numpy as jnp
from jax import lax
from jax.experimental import pallas as pl
from jax.experimental.pallas import tpu as pltpu

f32 = jnp.float32
bf16 = jnp.bfloat16
i32 = jnp.int32
HIGHEST = lax.Precision.HIGHEST

N_META = 16
HEAD_DIM = 64
N_HEADS = 8
WIDTH = N_HEADS * HEAD_DIM
DECAY_LORA, AAA_LORA, GATE_LORA = 64, 64, 160
ATTN_BLOCK = 128
PEER_HEADS, PEER_QDIM, N_KEYS, PEER_TOPK = 8, 256, 128, 16
NORM_EPS = 1e-6
RWKV_GN_EPS = 64e-5
MASK_VALUE = -1e30
LOG2_E = 1.4426950408889634

LANES = 128
VMEM_LIMIT_BYTES = 56 * 1024 * 1024

CHUNK = 64
RWKV_ROWS = 2 * CHUNK
PAIR = 2 * HEAD_DIM
PEER_TOK = 128
PEER_PAIRS = PEER_HEADS * PEER_TOPK
HID_UNROLL = 32
TABLE_ROWS = 4
FOX_COLS = 3 * WIDTH + LANES
RWKV_COLS = 3 * WIDTH + 2 * LANES + 2 * LANES


def _row_block(n):
    for t in (512, 384, 256, 128):
        if n % t == 0:
            return t
    raise ValueError(f"row count {n} is not a multiple of 128")


def _attn_block(length):
    return 384 if length % 384 == 0 else ATTN_BLOCK


def _cparams(*sem):
    return pltpu.CompilerParams(dimension_semantics=sem, vmem_limit_bytes=VMEM_LIMIT_BYTES)


def _const_spec(shape):
    nd = len(shape)
    return pl.BlockSpec(shape, lambda *_: (0,) * nd)


def _nt(a, b, **kw):
    return lax.dot_general(a, b, (((1,), (1,)), ((), ())), preferred_element_type=f32, **kw)


def _tn(a, b, **kw):
    return lax.dot_general(a, b, (((0,), (0,)), ((), ())), preferred_element_type=f32, **kw)


def _mm(a, b, **kw):
    return jnp.dot(a, b, preferred_element_type=f32, **kw)


def _softplus(y):
    return jnp.maximum(y, 0.0) + jnp.log1p(jnp.exp(-jnp.abs(y)))


def _sigmoid(y):
    return 1.0 / (1.0 + jnp.exp(-y))


def _three_term_mm(a, x):
    hi = x.astype(bf16)
    r1 = x - hi.astype(f32)
    mid = r1.astype(bf16)
    lo = (r1 - mid.astype(f32)).astype(bf16)
    return _mm(a, hi) + _mm(a, mid) + _mm(a, lo)


def _head_mean(x, bd):
    hi = x.astype(bf16)
    lo = (x - hi.astype(f32)).astype(bf16)
    return _mm(hi, bd) + _mm(lo, bd)


def _proj_kernel(x_ref, g_ref, w_ref, qn_ref, kn_ref, fb_ref, bd_ref,
                 q_ref, k_ref, v_ref, lf_ref, zr_ref, gate_ref):
    x = x_ref[...]
    ms = jnp.mean(x * x, axis=-1, keepdims=True)
    xn = (x * lax.rsqrt(ms + NORM_EPS) * g_ref[...]).astype(bf16)

    def head_norm(z, gain):
        msq = _mm((z * z).astype(bf16), bd_ref[...])
        return z * lax.rsqrt(msq + NORM_EPS) * gain

    zq = _mm(xn, w_ref[:, 0:WIDTH])
    q_ref[...] = (head_norm(zq, qn_ref[...]) * (HEAD_DIM ** -0.5 * LOG2_E)).astype(bf16)
    zk = _mm(xn, w_ref[:, WIDTH:2 * WIDTH])
    k_ref[...] = head_norm(zk, kn_ref[...]).astype(bf16)
    v_ref[...] = _mm(xn, w_ref[:, 2 * WIDTH:3 * WIDTH]).astype(bf16)
    zf = _mm(xn, w_ref[:, 3 * WIDTH:FOX_COLS]) + fb_ref[...]
    lf_ref[...] = -_softplus(-zf) * LOG2_E
    zr_ref[...] = _mm(xn, w_ref[:, FOX_COLS:FOX_COLS + RWKV_COLS])
    zg = _mm(xn, w_ref[:, FOX_COLS + RWKV_COLS:])
    gate_ref[...] = _sigmoid(zg).astype(bf16)


def _proj(hflat, g, w, qn, kn, fb, bd):
    n, d = hflat.shape
    tm = _row_block(n)
    ncol = w.shape[1]
    ngate = ncol - FOX_COLS - RWKV_COLS
    row = lambda c: pl.BlockSpec((tm, c), lambda i: (i, 0))
    return pl.pallas_call(
        _proj_kernel,
        grid=(n // tm,),
        in_specs=[row(d), _const_spec((1, d)), _const_spec((d, ncol)), _const_spec((1, WIDTH)),
                  _const_spec((1, WIDTH)), _const_spec((1, LANES)), _const_spec((WIDTH, WIDTH))],
        out_specs=[row(WIDTH), row(WIDTH), row(WIDTH), row(LANES), row(RWKV_COLS), row(ngate)],
        out_shape=[jax.ShapeDtypeStruct((n, WIDTH), bf16)] * 3
        + [jax.ShapeDtypeStruct((n, LANES), f32), jax.ShapeDtypeStruct((n, RWKV_COLS), f32),
           jax.ShapeDtypeStruct((n, ngate), bf16)],
        compiler_params=_cparams("parallel"),
        name="proj",
    )(hflat, g, w, qn, kn, fb, bd)


def _cumsum_kernel(lf_ref, o_ref, *, pad, ta):
    length = lf_ref.shape[0]
    r = lax.broadcasted_iota(i32, (LANES, LANES), 0)
    c = lax.broadcasted_iota(i32, (LANES, LANES), 1)
    tri = (r >= c).astype(f32)
    lane = lax.broadcasted_iota(i32, (N_HEADS, LANES), 1)
    carry = jnp.zeros((1, LANES), f32)
    per = ta // LANES
    for blk in range(length // LANES):
        cs = _mm(tri, lf_ref[blk * LANES:(blk + 1) * LANES, :], precision=HIGHEST) + carry
        carry = cs[LANES - 1:LANES, :]
        ct = cs.T[0:N_HEADS, :]
        neg = jnp.where(lane + blk * LANES >= pad, -ct, MASK_VALUE)
        j, off = blk // per, (blk % per) * LANES
        for h in range(N_HEADS):
            o_ref[h, j:j + 1, off:off + LANES] = neg[h:h + 1, :]


def _neg_cumsum(lf, pad, ta):
    b, length, _ = lf.shape
    nb = length // ta
    return pl.pallas_call(
        functools.partial(_cumsum_kernel, pad=pad, ta=ta),
        grid=(b,),
        in_specs=[pl.BlockSpec((None, length, LANES), lambda i: (i, 0, 0))],
        out_specs=pl.BlockSpec((None, N_HEADS, nb, ta), lambda i: (i, 0, 0, 0)),
        out_shape=jax.ShapeDtypeStruct((b, N_HEADS, nb, ta), f32),
        compiler_params=_cparams("parallel"),
        name="cumsum",
    )(lf)


def _attn_kernel(q_ref, k_ref, v_ref, nc_ref, o_ref, *, ta):
    qi = pl.program_id(2)
    lane = lax.broadcasted_iota(i32, (1, PAIR), 1)
    row = lax.broadcasted_iota(i32, (ATTN_BLOCK, ta), 0)
    col = lax.broadcasted_iota(i32, (ATTN_BLOCK, ta), 1)
    nrow = ta // ATTN_BLOCK
    pcs = [(h, r) for h in range(2) for r in range(nrow)]
    qp = []
    for h, r in pcs:
        qr = q_ref[r * ATTN_BLOCK:(r + 1) * ATTN_BLOCK, :]
        qp.append(jnp.where((lane // HEAD_DIM) == h, qr, jnp.zeros_like(qr)))

    def block(j, carries, diagonal):
        start = pl.multiple_of(j * ta, ta)
        ks = k_ref[pl.ds(start, ta), :]
        vs = v_ref[pl.ds(start, ta), :]
        s = [_nt(qp[i], ks) + nc_ref[h, pl.ds(j, 1), :] for i, (h, r) in enumerate(pcs)]
        if diagonal:
            s = [jnp.where(col <= row + r * ATTN_BLOCK, s[i], MASK_VALUE) for i, (h, r) in enumerate(pcs)]
        ids = range(len(pcs))
        m_new = [jnp.maximum(carries[i][0], jnp.max(s[i], axis=-1, keepdims=True)) for i in ids]
        p = [jnp.exp2(s[i] - m_new[i]) for i in ids]
        alpha = [jnp.exp2(carries[i][0] - m_new[i]) for i in ids]
        l = [alpha[i] * carries[i][1] + jnp.sum(p[i], axis=-1, keepdims=True) for i in ids]
        pv = [_mm(p[i].astype(bf16), vs) for i in ids]
        acc = [alpha[i] * carries[i][2] + pv[i] for i in ids]
        return tuple((m_new[i], l[i], acc[i]) for i in ids)

    init = (jnp.full((ATTN_BLOCK, 1), MASK_VALUE, f32), jnp.zeros((ATTN_BLOCK, 1), f32),
            jnp.zeros((ATTN_BLOCK, PAIR), f32))
    carries = lax.fori_loop(0, qi, lambda j, c: block(j, c, False), (init,) * len(pcs))
    final = block(qi, carries, True)
    out = [jnp.concatenate([final[h * nrow + r][2] / final[h * nrow + r][1] for r in range(nrow)], axis=0)
           for h in range(2)]
    o_ref[...] = jnp.where((lane // HEAD_DIM) == 0, out[0], out[1]).astype(o_ref.dtype)


def _attention(q, k, v, negc, ta):
    b, length, _ = q.shape
    nb = length // ta
    npair = N_HEADS // 2
    return pl.pallas_call(
        functools.partial(_attn_kernel, ta=ta),
        grid=(b, npair, nb),
        in_specs=[pl.BlockSpec((None, ta, PAIR), lambda i, p, j: (i, j, p)),
                  pl.BlockSpec((None, length, PAIR), lambda i, p, j: (i, 0, p)),
                  pl.BlockSpec((None, length, PAIR), lambda i, p, j: (i, 0, p)),
                  pl.BlockSpec((None, 2, nb, ta), lambda i, p, j: (i, p, 0, 0))],
        out_specs=pl.BlockSpec((None, ta, PAIR), lambda i, p, j: (i, j, p)),
        out_shape=jax.ShapeDtypeStruct((b, length, WIDTH), bf16),
        compiler_params=_cparams("parallel", "parallel", "arbitrary"),
        name="attn",
    )(q, k, v, negc)


def _stack(x, lane_head):
    zero = jnp.zeros_like(x)
    return jnp.concatenate([jnp.where(lane_head == 0, x, zero), jnp.where(lane_head == 1, x, zero)], axis=0)


def _unit_lower_inverse(ns, r, c):
    eye = (r == c).astype(f32)
    first = ((r ^ c) == 1) & ((r & 1) == 1)
    ts = [(eye - jnp.where(first, n, 0.0)).astype(bf16) for n in ns]
    s = 2
    while s < CHUNK:
        sel = ((r // (2 * s)) == (c // (2 * s))) & ((r & s) != 0) & ((c & s) == 0)
        low = [jnp.where(sel, n, 0.0).astype(bf16) for n in ns]
        tn = [_mm(t, lo).astype(bf16) for t, lo in zip(ts, low)]
        ts = [(t.astype(f32) - _mm(x, t)).astype(bf16) for t, x in zip(ts, tn)]
        s *= 2
    return ts


def _rwkv_kernel(z_ref, mu_ref, w0_ref, a0_ref, kk_ref, ka_ref, rk_ref, lng_ref, lnb_ref,
                 wup_ref, aup_ref, gup_ref, bdm_ref, o_ref, s_ref, prev_ref):
    ci = pl.program_id(1)

    @pl.when(ci == 0)
    def _():
        s_ref[...] = jnp.zeros_like(s_ref)
        prev_ref[...] = jnp.zeros_like(prev_ref)

    z = z_ref[...]
    rows = lax.broadcasted_iota(i32, (RWKV_ROWS, 1), 0)
    zprev = jnp.where(rows == 0, prev_ref[...], pltpu.roll(z, 1, 0))
    prev_ref[...] = z[RWKV_ROWS - 1:RWKV_ROWS, :]
    zs = z + mu_ref[...] * (zprev - z)
    r = zs[:, 0:WIDTH]
    k = zs[:, WIDTH:2 * WIDTH]
    v = zs[:, 2 * WIDTH:3 * WIDTH]
    o1 = 3 * WIDTH
    wd = zs[:, o1:o1 + LANES]
    ad = zs[:, o1 + LANES:o1 + 2 * LANES]
    gd = zs[:, o1 + 2 * LANES:o1 + 4 * LANES]

    w = -_softplus(-(w0_ref[...] + _mm(jnp.tanh(wd).astype(bf16), wup_ref[...]))) - 0.5
    ld = -jnp.exp(w)
    a = _sigmoid(a0_ref[...] + _mm(ad.astype(bf16), aup_ref[...]))
    g = _mm(_sigmoid(gd).astype(bf16), gup_ref[...])
    bdm = bdm_ref[...]
    kk = k * kk_ref[...]
    nrm = jnp.sqrt(_mm((kk * kk).astype(bf16), bdm) * HEAD_DIM)
    kk = kk / jnp.maximum(nrm, 1e-12)
    k2 = k * (1.0 + (a - 1.0) * ka_ref[...])
    b = kk * a

    tr = lax.broadcasted_iota(i32, (RWKV_ROWS, RWKV_ROWS), 0)
    tc = lax.broadcasted_iota(i32, (RWKV_ROWS, RWKV_ROWS), 1)
    tri = ((tr >= tc) & ((tr // CHUNK) == (tc // CHUNK))).astype(bf16)
    lc = _three_term_mm(tri, ld)
    lp = lc - ld
    chunks = range(RWKV_ROWS // CHUNK)
    mids = [lc[c * CHUNK + CHUNK // 2 - 1:c * CHUNK + CHUNK // 2, :] for c in chunks]
    tots = [lc[(c + 1) * CHUNK - 1:(c + 1) * CHUNK, :] for c in chunks]
    mid, tot = mids[-1], tots[-1]
    for c in reversed(chunks[:-1]):
        mid = jnp.where(rows < (c + 1) * CHUNK, mids[c], mid)
        tot = jnp.where(rows < (c + 1) * CHUNK, tots[c], tot)
    kq = kk * jnp.exp(lp - mid)
    rr = r * jnp.exp(lc - mid)
    e_after = jnp.exp(mid - lc)
    kh = k2 * e_after
    bh = b * e_after
    e_end = jnp.exp(tot - lc)
    kc = k2 * e_end
    bc = b * e_end
    e_mid = [jnp.exp(m) for m in mids]
    p_end = [jnp.exp(t) for t in tots]

    sr = lax.broadcasted_iota(i32, (PAIR, PAIR), 0)
    sc = lax.broadcasted_iota(i32, (PAIR, PAIR), 1)
    same = (sr // CHUNK) == (sc // CHUNK)
    strict = same & ((sr % CHUNK) > (sc % CHUNK))
    incl = same & ((sr % CHUNK) >= (sc % CHUNK))
    lane_head = lax.broadcasted_iota(i32, (1, PAIR), 1) // HEAD_DIM

    npair = N_HEADS // 2
    units = [(c, p) for c in chunks for p in range(npair)]
    ids = range(len(units))
    lanes = [slice(p * PAIR, (p + 1) * PAIR) for _, p in units]
    st = lambda x: [_stack(x[c * CHUNK:(c + 1) * CHUNK, p * PAIR:(p + 1) * PAIR], lane_head) for c, p in units]
    kq_s, rr_s, kh_s, bh_s, v_s, kc_s, bc_s = (st(x) for x in (kq, rr, kh, bh, v, kc, bc))
    amat = [_nt(jnp.concatenate([kq_s[i], rr_s[i]], axis=0).astype(bf16),
                jnp.concatenate([kh_s[i], bh_s[i]], axis=0).astype(bf16)) for i in ids]
    a_kk = [jnp.where(strict, amat[i][0:PAIR, 0:PAIR], 0.0).astype(bf16) for i in ids]
    a_kb = [jnp.where(strict, amat[i][0:PAIR, PAIR:], 0.0) for i in ids]
    a_rk = [jnp.where(incl, amat[i][PAIR:, 0:PAIR], 0.0).astype(bf16) for i in ids]
    a_rb = [jnp.where(incl, amat[i][PAIR:, PAIR:], 0.0).astype(bf16) for i in ids]
    t = _unit_lower_inverse(a_kb, sr, sc)
    vb = [v_s[i].astype(bf16) for i in ids]
    av = [_mm(a_kk[i], vb[i]).astype(bf16) for i in ids]
    wm = [_mm(t[i], kq_s[i].astype(bf16)) for i in ids]
    uv = [_mm(t[i], av[i]) for i in ids]
    rq = [rr_s[i] - _mm(a_rb[i], wm[i].astype(bf16)) for i in ids]
    yv = [_mm(a_rk[i], vb[i]) - _mm(a_rb[i], uv[i].astype(bf16)) for i in ids]
    wm_t = [(wm[i] * e_mid[units[i][0]][:, lanes[i]]).astype(bf16) for i in ids]
    rq_t = [(rq[i] * e_mid[units[i][0]][:, lanes[i]]).astype(bf16) for i in ids]
    bcb = [bc_s[i].astype(bf16) for i in ids]
    omega = [_tn(wm_t[i], bcb[i]).astype(bf16) for i in ids]
    psi = [_tn(vb[i], kc_s[i].astype(bf16)) - _tn(uv[i].astype(bf16), bcb[i]) for i in ids]
    state = [s_ref[p] for p in range(npair)]
    ystack = []
    for c in chunks:
        new = []
        for p in range(npair):
            i = c * npair + p
            s_hi = state[p].astype(bf16)
            s_lo = (state[p] - s_hi.astype(f32)).astype(bf16)
            ystack.append(_nt(rq_t[i], s_hi) + yv[i])
            new.append(state[p] * p_end[c][:, lanes[i]] - (_mm(s_hi, omega[i]) + _mm(s_lo, omega[i])) + psi[i])
        state = new
    for p in range(npair):
        s_ref[p] = state[p]
    y = jnp.concatenate(
        [jnp.concatenate([ystack[c * npair + p][0:CHUNK, :] + ystack[c * npair + p][CHUNK:, :] for p in range(npair)],
                         axis=1) for c in chunks], axis=0)

    mean = _head_mean(y, bdm)
    yc = y - mean
    var = _head_mean(yc * yc, bdm)
    yn = yc * lax.rsqrt(var + RWKV_GN_EPS) * lng_ref[...] + lnb_ref[...]
    bonus = _mm((r * k2 * rk_ref[...]).astype(bf16), bdm) * HEAD_DIM * v
    o_ref[...] = ((yn + bonus) * g).astype(o_ref.dtype)


def _rwkv(zr, mu, w0, a0, k_k, k_a, r_k, ln_g, ln_b, w_up, a_up, g_up, bdm):
    b, length, _ = zr.shape
    vec = _const_spec((1, WIDTH))
    return pl.pallas_call(
        _rwkv_kernel,
        grid=(b, length // RWKV_ROWS),
        in_specs=[pl.BlockSpec((None, RWKV_ROWS, RWKV_COLS), lambda i, c: (i, c, 0)),
                  _const_spec((1, RWKV_COLS)), vec, vec, vec, vec, vec, vec, vec,
                  _const_spec((LANES, WIDTH)), _const_spec((LANES, WIDTH)), _const_spec((2 * LANES, WIDTH)),
                  _const_spec((WIDTH, WIDTH))],
        out_specs=pl.BlockSpec((None, RWKV_ROWS, WIDTH), lambda i, c: (i, c, 0)),
        out_shape=jax.ShapeDtypeStruct((b, length, WIDTH), bf16),
        scratch_shapes=[pltpu.VMEM((N_HEADS // 2, PAIR, PAIR), f32), pltpu.VMEM((1, RWKV_COLS), f32)],
        compiler_params=_cparams("parallel", "arbitrary"),
        name="rwkv",
    )(zr, mu, w0, a0, k_k, k_a, r_k, ln_g, ln_b, w_up, a_up, g_up, bdm)


def _merge_kernel(yf_ref, yr_ref, gate_ref, h_ref, wf_ref, wr_ref, wo_ref, o_ref):
    d = h_ref.shape[1]
    pf = _mm(yf_ref[...], wf_ref[...])
    pr = _mm(yr_ref[...], wr_ref[...])
    mixed = gate_ref[:, 0:d].astype(f32) * pf + gate_ref[:, d:2 * d].astype(f32) * pr
    o_ref[...] = h_ref[...] + _mm(mixed.astype(bf16), wo_ref[...])


def _merge(yf, yr, gates, hflat, wf, wr, wo):
    n, d = hflat.shape
    tm = _row_block(n)
    row = lambda c: pl.BlockSpec((tm, c), lambda i: (i, 0))
    return pl.pallas_call(
        _merge_kernel,
        grid=(n // tm,),
        in_specs=[row(WIDTH), row(WIDTH), row(2 * d), row(d),
                  _const_spec((WIDTH, d)), _const_spec((WIDTH, d)), _const_spec((d, d))],
        out_specs=row(d),
        out_shape=jax.ShapeDtypeStruct((n, d), f32),
        compiler_params=_cparams("parallel"),
        name="merge",
    )(yf, yr, gates, hflat, wf, wr, wo)


def _top_rows(s, k):
    n = s.shape[0]
    pos = lax.broadcasted_iota(i32, s.shape, 0).astype(f32)
    vals, outs = [], []
    for _ in range(k):
        m = jnp.max(s, axis=0, keepdims=True)
        first = jnp.min(jnp.where(s == m, pos, float(n)), axis=0, keepdims=True)
        vals.append(m)
        outs.append(first)
        s = jnp.where(pos == first, -jnp.inf, s)
    return jnp.concatenate(vals, axis=0), jnp.concatenate(outs, axis=0)


def _staircase():
    pairs = [(a, b) for a in range(PEER_TOPK) for b in range(PEER_TOPK) if (a + 1) * (b + 1) <= PEER_TOPK]
    rows = -(-len(pairs) // 8) * 8
    sel = jnp.zeros((2, rows, PEER_TOPK), f32)
    r = jnp.arange(len(pairs))
    sel = sel.at[0, r, jnp.array([a for a, _ in pairs])].set(1.0)
    sel = sel.at[1, r, jnp.array([b for _, b in pairs])].set(1.0)
    return sel, len(pairs)


def _peer_sel_kernel(h_ref, g_ref, wq_ref, k1_ref, k2_ref, sel_ref, xn_ref, idx_ref, gate_ref, *, n_cand):
    x = h_ref[...]
    ms = jnp.mean(x * x, axis=-1, keepdims=True)
    xn = x * lax.rsqrt(ms + NORM_EPS) * g_ref[...]
    xn_ref[...] = xn.reshape(xn_ref.shape)
    qt = _nt(wq_ref[...], xn.astype(bf16))
    half = PEER_QDIM // 2
    sel_a, sel_b = sel_ref[0], sel_ref[1]
    pick = lambda sel, t: _mm(sel, t, precision=HIGHEST)
    s1 = jnp.concatenate([_mm(k1_ref[...], qt[h * PEER_QDIM:h * PEER_QDIM + half, :].astype(bf16))
                          for h in range(PEER_HEADS)], axis=1)
    s2 = jnp.concatenate([_mm(k2_ref[...], qt[h * PEER_QDIM + half:(h + 1) * PEER_QDIM, :].astype(bf16))
                          for h in range(PEER_HEADS)], axis=1)
    t1, i1 = _top_rows(s1, PEER_TOPK)
    t2, i2 = _top_rows(s2, PEER_TOPK)
    rows = lax.broadcasted_iota(i32, (sel_a.shape[0], PEER_HEADS * PEER_TOK), 0).astype(f32)
    cand_s = jnp.where(rows < n_cand, pick(sel_a, t1) + pick(sel_b, t2), -jnp.inf)
    cand_i = pick(sel_a, i1) * N_KEYS + pick(sel_b, i2)
    top_s, top_p = _top_rows(cand_s, PEER_TOPK)
    top_i = jnp.concatenate(
        [jnp.max(jnp.where(rows == top_p[j:j + 1, :], cand_i, -1.0), axis=0, keepdims=True)
         for j in range(PEER_TOPK)], axis=0).astype(i32)
    e = jnp.exp(top_s - top_s[0:1, :])
    gate = e / jnp.sum(e, axis=0, keepdims=True)
    by_head = lambda a: jnp.concatenate([a[:, h * PEER_TOK:(h + 1) * PEER_TOK] for h in range(PEER_HEADS)], axis=0)
    idx_ref[...] = by_head(top_i).T * TABLE_ROWS
    gate_ref[...] = by_head(gate).T


def _peer_select(h2, g, wq_t, k1, k2, batch, seq, length):
    d = h2.shape[-1]
    nt = seq // PEER_TOK
    skip = (length - seq) // PEER_TOK
    nblk = batch * nt
    sel, n_cand = _staircase()
    blk = pl.BlockSpec((None, PEER_TOK, PEER_PAIRS), lambda i, j: (i * nt + j, 0, 0))
    return pl.pallas_call(
        functools.partial(_peer_sel_kernel, n_cand=n_cand),
        grid=(batch, nt),
        in_specs=[pl.BlockSpec((None, PEER_TOK, d), lambda i, j: (i, j + skip, 0)),
                  _const_spec((1, d)), _const_spec(wq_t.shape), _const_spec(k1.shape), _const_spec(k2.shape),
                  _const_spec(sel.shape)],
        out_specs=[pl.BlockSpec((PEER_TOK, d // LANES, LANES), lambda i, j: (i * nt + j, 0, 0)), blk, blk],
        out_shape=[jax.ShapeDtypeStruct((nblk * PEER_TOK, d // LANES, LANES), f32),
                   jax.ShapeDtypeStruct((nblk, PEER_TOK, PEER_PAIRS), i32),
                   jax.ShapeDtypeStruct((nblk, PEER_TOK, PEER_PAIRS), f32)],
        compiler_params=_cparams("parallel", "parallel"),
        name="peer_sel",
    )(h2.reshape(batch, length, d), g, wq_t, k1, k2, sel)


def _pack_table(tab):
    e, d = tab.shape
    assert d == 2 * TABLE_ROWS * LANES
    t = lax.bitcast_convert_type(tab.astype(bf16), jnp.uint16).astype(jnp.uint32).reshape(e, 2, TABLE_ROWS, LANES)
    return lax.bitcast_convert_type((t[:, 0] << 16) | t[:, 1], i32).reshape(e * TABLE_ROWS, LANES)


def _unpack(word):
    return (lax.bitcast_convert_type(word & jnp.int32(-65536), f32), lax.bitcast_convert_type(word << 16, f32))


def _expert_rows(idx_ref, tab_ref, t, k):
    return tab_ref[pl.ds(pl.multiple_of(idx_ref[t, k], TABLE_ROWS), TABLE_ROWS), :]


def _peer_hid_kernel(idx_ref, x_ref, gate_ref, fold_ref, tab_ref, o_ref, part_a, part_b, dots_ref):
    parts = (part_a, part_b)
    ones = jnp.ones((8, LANES), bf16)

    def products(t, buf):
        xt = x_ref[t]
        xa, xb = xt[0:TABLE_ROWS, :], xt[TABLE_ROWS:, :]
        for k in range(PEER_PAIRS):
            hi, lo = _unpack(_expert_rows(idx_ref, tab_ref, t, k))
            parts[buf][TABLE_ROWS * k:TABLE_ROWS * (k + 1), :] = hi * xa + lo * xb

    def reduce(t, buf):
        lane_sums = _nt(ones, parts[buf][...].astype(bf16))
        dots_ref[pl.ds(t, 1), :] = lane_sums[0:1, :]

    part_b[...] = jnp.zeros_like(part_b)

    def body(i, carry):
        t0 = HID_UNROLL * i
        for u in range(HID_UNROLL):
            reduce(jnp.maximum(t0 + u - 1, 0), (u + 1) % 2)
            products(t0 + u, u % 2)
        return carry

    lax.fori_loop(0, PEER_TOK // HID_UNROLL, body, 0)
    reduce(PEER_TOK - 1, (PEER_TOK - 1) % 2)
    hid = _mm(dots_ref[...], fold_ref[...], precision=HIGHEST)
    o_ref[...] = gate_ref[...] * (0.5 * hid * (1.0 + lax.erf(hid * (2.0 ** -0.5))))


def _two_term_rows(row):
    hi = row.astype(bf16).astype(f32)
    r = lax.broadcasted_iota(i32, (8, LANES), 0)
    return jnp.where(r == 0, hi, jnp.where(r == 1, row - hi, 0.0)).astype(bf16)


def _peer_out_kernel(idx_ref, c_ref, h_ref, tab_ref, o_ref, stage_a, stage_b, acc_ref):
    stage = (stage_a, stage_b)
    row = lax.broadcasted_iota(i32, (8, LANES), 0)
    acc_ref[...] = h_ref[...].reshape(acc_ref.shape)

    def gather(t, buf):
        for k in range(PEER_PAIRS):
            stage[buf][TABLE_ROWS * k:TABLE_ROWS * (k + 1), :] = _expert_rows(idx_ref, tab_ref, t, k)

    def combine(t, buf):
        lhs = _two_term_rows(c_ref[t:t + 1, :])
        out = acc_ref[t]
        for s in range(TABLE_ROWS):
            hi, lo = _unpack(stage[buf][pl.ds(s, PEER_PAIRS, stride=TABLE_ROWS), :])
            a = _mm(lhs, hi.astype(bf16))
            b = _mm(lhs, lo.astype(bf16))
            out = (out + jnp.where(row == s, a[0:1, :] + a[1:2, :], 0.0)
                   + jnp.where(row == TABLE_ROWS + s, b[0:1, :] + b[1:2, :], 0.0))
        acc_ref[t] = out

    gather(0, 0)
    gather(1, 1)
    for t in range(PEER_TOK):
        combine(t, t % 2)
        if t + 2 < PEER_TOK:
            gather(t + 2, t % 2)
    o_ref[...] = acc_ref[...].reshape(o_ref.shape)


def _smem_blk(index_map, fixed_address):
    mode = dict(pipeline_mode=pl.Buffered(1)) if fixed_address else {}
    return pl.BlockSpec((None, PEER_TOK, PEER_PAIRS), index_map, memory_space=pltpu.SMEM, **mode)


def _table_spec(tab):
    return pl.BlockSpec(tab.shape, lambda *_: (0, 0), pipeline_mode=pl.Buffered(1))


def _peer_hidden(idx, x3, gate, utab):
    nblk = idx.shape[0]
    sub = x3.shape[1]
    vblk = pl.BlockSpec((None, PEER_TOK, PEER_PAIRS), lambda i: (i, 0, 0))
    rows = TABLE_ROWS * PEER_PAIRS
    fold = (jnp.arange(rows)[:, None] // TABLE_ROWS == jnp.arange(PEER_PAIRS)[None, :]).astype(f32)
    return pl.pallas_call(
        _peer_hid_kernel,
        grid=(nblk,),
        in_specs=[_smem_blk(lambda i: (i, 0, 0), False), pl.BlockSpec((PEER_TOK, sub, LANES), lambda i: (i, 0, 0)), vblk,
                  _const_spec(fold.shape), _table_spec(utab)],
        out_specs=vblk,
        out_shape=jax.ShapeDtypeStruct((nblk, PEER_TOK, PEER_PAIRS), f32),
        scratch_shapes=[pltpu.VMEM((rows, LANES), f32), pltpu.VMEM((rows, LANES), f32),
                        pltpu.VMEM((PEER_TOK, rows), f32)],
        compiler_params=_cparams("arbitrary"),
        name="peer_hid",
    )(idx, x3, gate, fold, utab)


def _peer_output(idx, coef, h2, vtab, batch, seq, length):
    d = h2.shape[-1]
    sub = d // LANES
    nt = seq // PEER_TOK
    skip = (length - seq) // PEER_TOK
    rows = TABLE_ROWS * PEER_PAIRS
    blk = lambda i, j: (i * nt + j, 0, 0)
    return pl.pallas_call(
        _peer_out_kernel,
        grid=(batch, nt),
        in_specs=[_smem_blk(blk, True), pl.BlockSpec((None, PEER_TOK, PEER_PAIRS), blk),
                  pl.BlockSpec((None, PEER_TOK, d), lambda i, j: (i, j + skip, 0)),
                  _table_spec(vtab)],
        out_specs=pl.BlockSpec((None, PEER_TOK, d), lambda i, j: (i, j, 0)),
        out_shape=jax.ShapeDtypeStruct((batch, seq, d), f32),
        scratch_shapes=[pltpu.VMEM((rows, LANES), i32), pltpu.VMEM((rows, LANES), i32),
                        pltpu.VMEM((PEER_TOK, sub, LANES), f32)],
        compiler_params=_cparams("arbitrary", "arbitrary"),
        name="peer_out",
    )(idx, coef, h2.reshape(batch, length, d), vtab)


def _pad_cols(w, width):
    return jnp.pad(w, ((0, 0), (0, width - w.shape[1])))


def _pad_rows(w, height):
    return jnp.pad(w, ((0, height - w.shape[0]), (0, 0)))


def kernel(x, meta_tokens, norm1_g, w_in, fox_q_norm, fox_k_norm, fox_f_bias, rwkv_mu, rwkv_w0, rwkv_w_up, rwkv_a0, rwkv_a_up, rwkv_g_up, rwkv_k_k, rwkv_k_a, rwkv_r_k, rwkv_ln_g, rwkv_ln_b, w_branch_fox, w_branch_rwkv, w_out, norm2_g, peer_w_q, peer_sub_k1, peer_sub_k2, peer_u, peer_v):
    batch, seq, d = x.shape
    assert w_in.shape[0] == 1, "one layer"
    assert seq % PEER_TOK == 0 and d % LANES == 0
    t_real = seq + N_META
    pad = (-t_real) % ATTN_BLOCK
    length = t_real + pad
    ta = _attn_block(length)

    meta = jnp.broadcast_to(meta_tokens[None].astype(x.dtype), (batch, N_META, d))
    hpad = jnp.concatenate([jnp.zeros((batch, pad, d), x.dtype), meta, x], axis=1).reshape(batch * length, d)
    w = w_in[0]
    fox_w = 3 * WIDTH + N_HEADS
    o = fox_w
    lora = lambda lo, n, width: _pad_cols(w[:, lo:lo + n], width)
    w_all = jnp.concatenate([
        w[:, 0:3 * WIDTH], _pad_cols(w[:, 3 * WIDTH:fox_w], LANES),
        w[:, o:o + 3 * WIDTH],
        lora(o + 3 * WIDTH, DECAY_LORA, LANES),
        lora(o + 3 * WIDTH + DECAY_LORA, AAA_LORA, LANES),
        lora(o + 3 * WIDTH + DECAY_LORA + AAA_LORA, GATE_LORA, 2 * LANES),
        w[:, o + 3 * WIDTH + DECAY_LORA + AAA_LORA + GATE_LORA:],
    ], axis=1).astype(bf16)
    mu = rwkv_mu[0]
    mu_all = jnp.concatenate([
        mu[0:3 * WIDTH], jnp.pad(mu[3 * WIDTH:3 * WIDTH + DECAY_LORA], (0, LANES - DECAY_LORA)),
        jnp.pad(mu[3 * WIDTH + DECAY_LORA:3 * WIDTH + DECAY_LORA + AAA_LORA], (0, LANES - AAA_LORA)),
        jnp.pad(mu[3 * WIDTH + DECAY_LORA + AAA_LORA:], (0, 2 * LANES - GATE_LORA)),
    ])[None]
    head_of = jnp.arange(WIDTH) // HEAD_DIM
    bdm = (head_of[:, None] == head_of[None, :]).astype(f32) / HEAD_DIM
    vec = lambda p: p[0].reshape(1, -1)

    q, k, v, lf, zr, gates = _proj(hpad, vec(norm1_g), w_all, vec(fox_q_norm), vec(fox_k_norm),
                                   jnp.pad(fox_f_bias[0], (0, LANES - N_HEADS))[None], bdm.astype(bf16))
    negc = _neg_cumsum(lf.reshape(batch, length, LANES), pad, ta)
    three = lambda a: a.reshape(batch, length, a.shape[-1])
    y_fox = _attention(three(q), three(k), three(v), negc, ta)
    y_rwkv = _rwkv(three(zr), mu_all, vec(rwkv_w0), vec(rwkv_a0), vec(rwkv_k_k), vec(rwkv_k_a), vec(rwkv_r_k),
                   vec(rwkv_ln_g), vec(rwkv_ln_b),
                   _pad_rows(rwkv_w_up[0], LANES).astype(bf16), _pad_rows(rwkv_a_up[0], LANES).astype(bf16),
                   _pad_rows(rwkv_g_up[0], 2 * LANES).astype(bf16), bdm.astype(bf16))
    h2 = _merge(y_fox.reshape(-1, WIDTH), y_rwkv.reshape(-1, WIDTH), gates, hpad,
                w_branch_fox[0].astype(bf16), w_branch_rwkv[0].astype(bf16), w_out[0].astype(bf16))

    xn2, idx, gate = _peer_select(h2, vec(norm2_g), peer_w_q[0].T.astype(bf16),
                                  peer_sub_k1[0].astype(bf16), peer_sub_k2[0].astype(bf16), batch, seq, length)
    coef = _peer_hidden(idx, xn2, gate, _pack_table(peer_u[0]))
    out = _peer_output(idx, coef, h2, _pack_table(peer_v[0]), batch, seq, length)
    return out.reshape(batch, seq, d)
```

```python
import functools

import jax
import jax.numpy as jnp
from jax import lax
from jax.experimental import pallas as pl
from jax.experimental.pallas import tpu as pltpu

f32 = jnp.float32
bf16 = jnp.bfloat16
i32 = jnp.int32
HIGHEST = lax.Precision.HIGHEST

N_META = 16
HEAD_DIM = 64
N_HEADS = 8
WIDTH = N_HEADS * HEAD_DIM
DECAY_LORA, AAA_LORA, GATE_LORA = 64, 64, 160
ATTN_BLOCK = 128
PEER_HEADS, PEER_QDIM, N_KEYS, PEER_TOPK = 8, 256, 128, 16
NORM_EPS = 1e-6
RWKV_GN_EPS = 64e-5
MASK_VALUE = -1e30
LOG2_E = 1.4426950408889634

LANES = 128
VMEM_LIMIT_BYTES = 56 * 1024 * 1024

CHUNK = 64
RWKV_ROWS = 2 * CHUNK
PAIR = 2 * HEAD_DIM
PEER_TOK = 128
PEER_PAIRS = PEER_HEADS * PEER_TOPK
TABLE_ROWS = 4
FOX_COLS = 3 * WIDTH + LANES
RWKV_COLS = 3 * WIDTH + 2 * LANES + 2 * LANES


def _row_block(n):
    for t in (512, 384, 256, 128):
        if n % t == 0:
            return t
    raise ValueError(f"row count {n} is not a multiple of 128")


def _attn_block(length):
    return 384 if length % 384 == 0 else ATTN_BLOCK


def _cparams(*sem):
    return pltpu.CompilerParams(dimension_semantics=sem, vmem_limit_bytes=VMEM_LIMIT_BYTES)


def _const_spec(shape):
    nd = len(shape)
    return pl.BlockSpec(shape, lambda *_: (0,) * nd)


def _nt(a, b, **kw):
    return lax.dot_general(a, b, (((1,), (1,)), ((), ())), preferred_element_type=f32, **kw)


def _tn(a, b, **kw):
    return lax.dot_general(a, b, (((0,), (0,)), ((), ())), preferred_element_type=f32, **kw)


def _mm(a, b, **kw):
    return jnp.dot(a, b, preferred_element_type=f32, **kw)


def _softplus(y):
    return jnp.maximum(y, 0.0) + jnp.log1p(jnp.exp(-jnp.abs(y)))


def _sigmoid(y):
    return 1.0 / (1.0 + jnp.exp(-y))


def _three_term_mm(a, x):
    hi = x.astype(bf16)
    r1 = x - hi.astype(f32)
    mid = r1.astype(bf16)
    lo = (r1 - mid.astype(f32)).astype(bf16)
    return _mm(a, hi) + _mm(a, mid) + _mm(a, lo)


def _head_mean(x, bd):
    hi = x.astype(bf16)
    lo = (x - hi.astype(f32)).astype(bf16)
    return _mm(hi, bd) + _mm(lo, bd)


def _proj_kernel(x_ref, g_ref, w_ref, qn_ref, kn_ref, fb_ref, bd_ref,
                 q_ref, k_ref, v_ref, lf_ref, zr_ref, gate_ref):
    x = x_ref[...]
    ms = jnp.mean(x * x, axis=-1, keepdims=True)
    xn = (x * lax.rsqrt(ms + NORM_EPS) * g_ref[...]).astype(bf16)

    def head_norm(z, gain):
        msq = _mm((z * z).astype(bf16), bd_ref[...])
        return z * lax.rsqrt(msq + NORM_EPS) * gain

    zq = _mm(xn, w_ref[:, 0:WIDTH])
    q_ref[...] = (head_norm(zq, qn_ref[...]) * (HEAD_DIM ** -0.5 * LOG2_E)).astype(bf16)
    zk = _mm(xn, w_ref[:, WIDTH:2 * WIDTH])
    k_ref[...] = head_norm(zk, kn_ref[...]).astype(bf16)
    v_ref[...] = _mm(xn, w_ref[:, 2 * WIDTH:3 * WIDTH]).astype(bf16)
    zf = _mm(xn, w_ref[:, 3 * WIDTH:FOX_COLS]) + fb_ref[...]
    lf_ref[...] = -_softplus(-zf) * LOG2_E
    zr_ref[...] = _mm(xn, w_ref[:, FOX_COLS:FOX_COLS + RWKV_COLS])
    zg = _mm(xn, w_ref[:, FOX_COLS + RWKV_COLS:])
    gate_ref[...] = _sigmoid(zg).astype(bf16)


def _proj(hflat, g, w, qn, kn, fb, bd):
    n, d = hflat.shape
    tm = _row_block(n)
    ncol = w.shape[1]
    ngate = ncol - FOX_COLS - RWKV_COLS
    row = lambda c: pl.BlockSpec((tm, c), lambda i: (i, 0))
    return pl.pallas_call(
        _proj_kernel,
        grid=(n // tm,),
        in_specs=[row(d), _const_spec((1, d)), _const_spec((d, ncol)), _const_spec((1, WIDTH)),
                  _const_spec((1, WIDTH)), _const_spec((1, LANES)), _const_spec((WIDTH, WIDTH))],
        out_specs=[row(WIDTH), row(WIDTH), row(WIDTH), row(LANES), row(RWKV_COLS), row(ngate)],
        out_shape=[jax.ShapeDtypeStruct((n, WIDTH), bf16)] * 3
        + [jax.ShapeDtypeStruct((n, LANES), f32), jax.ShapeDtypeStruct((n, RWKV_COLS), f32),
           jax.ShapeDtypeStruct((n, ngate), bf16)],
        compiler_params=_cparams("parallel"),
        name="proj",
    )(hflat, g, w, qn, kn, fb, bd)


def _cumsum_kernel(lf_ref, o_ref, *, pad, ta):
    length = lf_ref.shape[0]
    r = lax.broadcasted_iota(i32, (LANES, LANES), 0)
    c = lax.broadcasted_iota(i32, (LANES, LANES), 1)
    tri = (r >= c).astype(f32)
    lane = lax.broadcasted_iota(i32, (N_HEADS, LANES), 1)
    carry = jnp.zeros((1, LANES), f32)
    per = ta // LANES
    for blk in range(length // LANES):
        cs = _mm(tri, lf_ref[blk * LANES:(blk + 1) * LANES, :], precision=HIGHEST) + carry
        carry = cs[LANES - 1:LANES, :]
        ct = cs.T[0:N_HEADS, :]
        neg = jnp.where(lane + blk * LANES >= pad, -ct, MASK_VALUE)
        j, off = blk // per, (blk % per) * LANES
        for h in range(N_HEADS):
            o_ref[h, j:j + 1, off:off + LANES] = neg[h:h + 1, :]


def _neg_cumsum(lf, pad, ta):
    b, length, _ = lf.shape
    nb = length // ta
    return pl.pallas_call(
        functools.partial(_cumsum_kernel, pad=pad, ta=ta),
        grid=(b,),
        in_specs=[pl.BlockSpec((None, length, LANES), lambda i: (i, 0, 0))],
        out_specs=pl.BlockSpec((None, N_HEADS, nb, ta), lambda i: (i, 0, 0, 0)),
        out_shape=jax.ShapeDtypeStruct((b, N_HEADS, nb, ta), f32),
        compiler_params=_cparams("parallel"),
        name="cumsum",
    )(lf)


def _attn_kernel(q_ref, k_ref, v_ref, nc_ref, o_ref, *, ta):
    qi = pl.program_id(2)
    lane = lax.broadcasted_iota(i32, (1, PAIR), 1)
    row = lax.broadcasted_iota(i32, (ATTN_BLOCK, ta), 0)
    col = lax.broadcasted_iota(i32, (ATTN_BLOCK, ta), 1)
    nrow = ta // ATTN_BLOCK
    pcs = [(h, r) for h in range(2) for r in range(nrow)]
    qp = []
    for h, r in pcs:
        qr = q_ref[r * ATTN_BLOCK:(r + 1) * ATTN_BLOCK, :]
        qp.append(jnp.where((lane // HEAD_DIM) == h, qr, jnp.zeros_like(qr)))

    def block(j, carries, diagonal):
        start = pl.multiple_of(j * ta, ta)
        ks = k_ref[pl.ds(start, ta), :]
        vs = v_ref[pl.ds(start, ta), :]
        s = [_nt(qp[i], ks) + nc_ref[h, pl.ds(j, 1), :] for i, (h, r) in enumerate(pcs)]
        if diagonal:
            s = [jnp.where(col <= row + r * ATTN_BLOCK, s[i], MASK_VALUE) for i, (h, r) in enumerate(pcs)]
        ids = range(len(pcs))
        m_new = [jnp.maximum(carries[i][0], jnp.max(s[i], axis=-1, keepdims=True)) for i in ids]
        p = [jnp.exp2(s[i] - m_new[i]) for i in ids]
        alpha = [jnp.exp2(carries[i][0] - m_new[i]) for i in ids]
        l = [alpha[i] * carries[i][1] + jnp.sum(p[i], axis=-1, keepdims=True) for i in ids]
        pv = [_mm(p[i].astype(bf16), vs) for i in ids]
        acc = [alpha[i] * carries[i][2] + pv[i] for i in ids]
        return tuple((m_new[i], l[i], acc[i]) for i in ids)

    init = (jnp.full((ATTN_BLOCK, 1), MASK_VALUE, f32), jnp.zeros((ATTN_BLOCK, 1), f32),
            jnp.zeros((ATTN_BLOCK, PAIR), f32))
    carries = lax.fori_loop(0, qi, lambda j, c: block(j, c, False), (init,) * len(pcs))
    final = block(qi, carries, True)
    out = [jnp.concatenate([final[h * nrow + r][2] / final[h * nrow + r][1] for r in range(nrow)], axis=0)
           for h in range(2)]
    o_ref[...] = jnp.where((lane // HEAD_DIM) == 0, out[0], out[1]).astype(o_ref.dtype)


def _attention(q, k, v, negc, ta):
    b, length, _ = q.shape
    nb = length // ta
    npair = N_HEADS // 2
    return pl.pallas_call(
        functools.partial(_attn_kernel, ta=ta),
        grid=(b, npair, nb),
        in_specs=[pl.BlockSpec((None, ta, PAIR), lambda i, p, j: (i, j, p)),
                  pl.BlockSpec((None, length, PAIR), lambda i, p, j: (i, 0, p)),
                  pl.BlockSpec((None, length, PAIR), lambda i, p, j: (i, 0, p)),
                  pl.BlockSpec((None, 2, nb, ta), lambda i, p, j: (i, p, 0, 0))],
        out_specs=pl.BlockSpec((None, ta, PAIR), lambda i, p, j: (i, j, p)),
        out_shape=jax.ShapeDtypeStruct((b, length, WIDTH), bf16),
        compiler_params=_cparams("parallel", "parallel", "arbitrary"),
        name="attn",
    )(q, k, v, negc)


def _stack(x, lane_head):
    zero = jnp.zeros_like(x)
    return jnp.concatenate([jnp.where(lane_head == 0, x, zero), jnp.where(lane_head == 1, x, zero)], axis=0)


def _unit_lower_inverse(ns, r, c):
    eye = (r == c).astype(f32)
    first = ((r ^ c) == 1) & ((r & 1) == 1)
    ts = [(eye - jnp.where(first, n, 0.0)).astype(bf16) for n in ns]
    s = 2
    while s < CHUNK:
        sel = ((r // (2 * s)) == (c // (2 * s))) & ((r & s) != 0) & ((c & s) == 0)
        low = [jnp.where(sel, n, 0.0).astype(bf16) for n in ns]
        tn = [_mm(t, lo).astype(bf16) for t, lo in zip(ts, low)]
        ts = [(t.astype(f32) - _mm(x, t)).astype(bf16) for t, x in zip(ts, tn)]
        s *= 2
    return ts


def _rwkv_kernel(z_ref, mu_ref, w0_ref, a0_ref, kk_ref, ka_ref, rk_ref, lng_ref, lnb_ref,
                 wup_ref, aup_ref, gup_ref, bdm_ref, o_ref, s_ref, prev_ref):
    ci = pl.program_id(1)

    @pl.when(ci == 0)
    def _():
        s_ref[...] = jnp.zeros_like(s_ref)
        prev_ref[...] = jnp.zeros_like(prev_ref)

    z = z_ref[...]
    rows = lax.broadcasted_iota(i32, (RWKV_ROWS, 1), 0)
    zprev = jnp.where(rows == 0, prev_ref[...], pltpu.roll(z, 1, 0))
    prev_ref[...] = z[RWKV_ROWS - 1:RWKV_ROWS, :]
    zs = z + mu_ref[...] * (zprev - z)
    r = zs[:, 0:WIDTH]
    k = zs[:, WIDTH:2 * WIDTH]
    v = zs[:, 2 * WIDTH:3 * WIDTH]
    o1 = 3 * WIDTH
    wd = zs[:, o1:o1 + LANES]
    ad = zs[:, o1 + LANES:o1 + 2 * LANES]
    gd = zs[:, o1 + 2 * LANES:o1 + 4 * LANES]

    w = -_softplus(-(w0_ref[...] + _mm(jnp.tanh(wd).astype(bf16), wup_ref[...]))) - 0.5
    ld = -jnp.exp(w)
    a = _sigmoid(a0_ref[...] + _mm(ad.astype(bf16), aup_ref[...]))
    g = _mm(_sigmoid(gd).astype(bf16), gup_ref[...])
    bdm = bdm_ref[...]
    kk = k * kk_ref[...]
    nrm = jnp.sqrt(_mm((kk * kk).astype(bf16), bdm) * HEAD_DIM)
    kk = kk / jnp.maximum(nrm, 1e-12)
    k2 = k * (1.0 + (a - 1.0) * ka_ref[...])
    b = kk * a

    tr = lax.broadcasted_iota(i32, (RWKV_ROWS, RWKV_ROWS), 0)
    tc = lax.broadcasted_iota(i32, (RWKV_ROWS, RWKV_ROWS), 1)
    tri = ((tr >= tc) & ((tr // CHUNK) == (tc // CHUNK))).astype(bf16)
    lc = _three_term_mm(tri, ld)
    lp = lc - ld
    chunks = range(RWKV_ROWS // CHUNK)
    mids = [lc[c * CHUNK + CHUNK // 2 - 1:c * CHUNK + CHUNK // 2, :] for c in chunks]
    tots = [lc[(c + 1) * CHUNK - 1:(c + 1) * CHUNK, :] for c in chunks]
    mid, tot = mids[-1], tots[-1]
    for c in reversed(chunks[:-1]):
        mid = jnp.where(rows < (c + 1) * CHUNK, mids[c], mid)
        tot = jnp.where(rows < (c + 1) * CHUNK, tots[c], tot)
    kq = kk * jnp.exp(lp - mid)
    rr = r * jnp.exp(lc - mid)
    e_after = jnp.exp(mid - lc)
    kh = k2 * e_after
    bh = b * e_after
    e_end = jnp.exp(tot - lc)
    kc = k2 * e_end
    bc = b * e_end
    e_mid = [jnp.exp(m) for m in mids]
    p_end = [jnp.exp(t) for t in tots]

    sr = lax.broadcasted_iota(i32, (PAIR, PAIR), 0)
    sc = lax.broadcasted_iota(i32, (PAIR, PAIR), 1)
    same = (sr // CHUNK) == (sc // CHUNK)
    strict = same & ((sr % CHUNK) > (sc % CHUNK))
    incl = same & ((sr % CHUNK) >= (sc % CHUNK))
    lane_head = lax.broadcasted_iota(i32, (1, PAIR), 1) // HEAD_DIM

    npair = N_HEADS // 2
    units = [(c, p) for c in chunks for p in range(npair)]
    ids = range(len(units))
    lanes = [slice(p * PAIR, (p + 1) * PAIR) for _, p in units]
    st = lambda x: [_stack(x[c * CHUNK:(c + 1) * CHUNK, p * PAIR:(p + 1) * PAIR], lane_head) for c, p in units]
    kq_s, rr_s, kh_s, bh_s, v_s, kc_s, bc_s = (st(x) for x in (kq, rr, kh, bh, v, kc, bc))
    amat = [_nt(jnp.concatenate([kq_s[i], rr_s[i]], axis=0).astype(bf16),
                jnp.concatenate([kh_s[i], bh_s[i]], axis=0).astype(bf16)) for i in ids]
    a_kk = [jnp.where(strict, amat[i][0:PAIR, 0:PAIR], 0.0).astype(bf16) for i in ids]
    a_kb = [jnp.where(strict, amat[i][0:PAIR, PAIR:], 0.0) for i in ids]
    a_rk = [jnp.where(incl, amat[i][PAIR:, 0:PAIR], 0.0).astype(bf16) for i in ids]
    a_rb = [jnp.where(incl, amat[i][PAIR:, PAIR:], 0.0).astype(bf16) for i in ids]
    t = _unit_lower_inverse(a_kb, sr, sc)
    vb = [v_s[i].astype(bf16) for i in ids]
    av = [_mm(a_kk[i], vb[i]).astype(bf16) for i in ids]
    wm = [_mm(t[i], kq_s[i].astype(bf16)) for i in ids]
    uv = [_mm(t[i], av[i]) for i in ids]
    rq = [rr_s[i] - _mm(a_rb[i], wm[i].astype(bf16)) for i in ids]
    yv = [_mm(a_rk[i], vb[i]) - _mm(a_rb[i], uv[i].astype(bf16)) for i in ids]
    wm_t = [(wm[i] * e_mid[units[i][0]][:, lanes[i]]).astype(bf16) for i in ids]
    rq_t = [(rq[i] * e_mid[units[i][0]][:, lanes[i]]).astype(bf16) for i in ids]
    bcb = [bc_s[i].astype(bf16) for i in ids]
    omega = [_tn(wm_t[i], bcb[i]).astype(bf16) for i in ids]
    psi = [_tn(vb[i], kc_s[i].astype(bf16)) - _tn(uv[i].astype(bf16), bcb[i]) for i in ids]
    state = [s_ref[p] for p in range(npair)]
    ystack = []
    for c in chunks:
        new = []
        for p in range(npair):
            i = c * npair + p
            s_hi = state[p].astype(bf16)
            s_lo = (state[p] - s_hi.astype(f32)).astype(bf16)
            ystack.append(_nt(rq_t[i], s_hi) + yv[i])
            new.append(state[p] * p_end[c][:, lanes[i]] - (_mm(s_hi, omega[i]) + _mm(s_lo, omega[i])) + psi[i])
        state = new
    for p in range(npair):
        s_ref[p] = state[p]
    y = jnp.concatenate(
        [jnp.concatenate([ystack[c * npair + p][0:CHUNK, :] + ystack[c * npair + p][CHUNK:, :] for p in range(npair)],
                         axis=1) for c in chunks], axis=0)

    mean = _head_mean(y, bdm)
    yc = y - mean
    var = _head_mean(yc * yc, bdm)
    yn = yc * lax.rsqrt(var + RWKV_GN_EPS) * lng_ref[...] + lnb_ref[...]
    bonus = _mm((r * k2 * rk_ref[...]).astype(bf16), bdm) * HEAD_DIM * v
    o_ref[...] = ((yn + bonus) * g).astype(o_ref.dtype)


def _rwkv(zr, mu, w0, a0, k_k, k_a, r_k, ln_g, ln_b, w_up, a_up, g_up, bdm):
    b, length, _ = zr.shape
    vec = _const_spec((1, WIDTH))
    return pl.pallas_call(
        _rwkv_kernel,
        grid=(b, length // RWKV_ROWS),
        in_specs=[pl.BlockSpec((None, RWKV_ROWS, RWKV_COLS), lambda i, c: (i, c, 0)),
                  _const_spec((1, RWKV_COLS)), vec, vec, vec, vec, vec, vec, vec,
                  _const_spec((LANES, WIDTH)), _const_spec((LANES, WIDTH)), _const_spec((2 * LANES, WIDTH)),
                  _const_spec((WIDTH, WIDTH))],
        out_specs=pl.BlockSpec((None, RWKV_ROWS, WIDTH), lambda i, c: (i, c, 0)),
        out_shape=jax.ShapeDtypeStruct((b, length, WIDTH), bf16),
        scratch_shapes=[pltpu.VMEM((N_HEADS // 2, PAIR, PAIR), f32), pltpu.VMEM((1, RWKV_COLS), f32)],
        compiler_params=_cparams("parallel", "arbitrary"),
        name="rwkv",
    )(zr, mu, w0, a0, k_k, k_a, r_k, ln_g, ln_b, w_up, a_up, g_up, bdm)


def _merge_kernel(yf_ref, yr_ref, gate_ref, h_ref, wf_ref, wr_ref, wo_ref, o_ref):
    d = h_ref.shape[1]
    pf = _mm(yf_ref[...], wf_ref[...])
    pr = _mm(yr_ref[...], wr_ref[...])
    mixed = gate_ref[:, 0:d].astype(f32) * pf + gate_ref[:, d:2 * d].astype(f32) * pr
    o_ref[...] = h_ref[...] + _mm(mixed.astype(bf16), wo_ref[...])


def _merge(yf, yr, gates, hflat, wf, wr, wo):
    n, d = hflat.shape
    tm = _row_block(n)
    row = lambda c: pl.BlockSpec((tm, c), lambda i: (i, 0))
    return pl.pallas_call(
        _merge_kernel,
        grid=(n // tm,),
        in_specs=[row(WIDTH), row(WIDTH), row(2 * d), row(d),
                  _const_spec((WIDTH, d)), _const_spec((WIDTH, d)), _const_spec((d, d))],
        out_specs=row(d),
        out_shape=jax.ShapeDtypeStruct((n, d), f32),
        compiler_params=_cparams("parallel"),
        name="merge",
    )(yf, yr, gates, hflat, wf, wr, wo)


def _top_rows(s, k):
    n = s.shape[0]
    pos = lax.broadcasted_iota(i32, s.shape, 0).astype(f32)
    vals, outs = [], []
    for _ in range(k):
        m = jnp.max(s, axis=0, keepdims=True)
        first = jnp.min(jnp.where(s == m, pos, float(n)), axis=0, keepdims=True)
        vals.append(m)
        outs.append(first)
        s = jnp.where(pos == first, -jnp.inf, s)
    return jnp.concatenate(vals, axis=0), jnp.concatenate(outs, axis=0)


def _staircase():
    pairs = [(a, b) for a in range(PEER_TOPK) for b in range(PEER_TOPK) if (a + 1) * (b + 1) <= PEER_TOPK]
    rows = -(-len(pairs) // 8) * 8
    sel = jnp.zeros((2, rows, PEER_TOPK), f32)
    r = jnp.arange(len(pairs))
    sel = sel.at[0, r, jnp.array([a for a, _ in pairs])].set(1.0)
    sel = sel.at[1, r, jnp.array([b for _, b in pairs])].set(1.0)
    return sel, len(pairs)


def _peer_sel_kernel(h_ref, g_ref, wq_ref, k1_ref, k2_ref, sel_ref, xn_ref, idx_ref, gate_ref, *, n_cand):
    x = h_ref[...]
    ms = jnp.mean(x * x, axis=-1, keepdims=True)
    xn = x * lax.rsqrt(ms + NORM_EPS) * g_ref[...]
    xn_ref[...] = xn.reshape(xn_ref.shape)
    qt = _nt(wq_ref[...], xn.astype(bf16))
    half = PEER_QDIM // 2
    sel_a, sel_b = sel_ref[0], sel_ref[1]
    pick = lambda sel, t: _mm(sel, t, precision=HIGHEST)
    s1 = jnp.concatenate([_mm(k1_ref[...], qt[h * PEER_QDIM:h * PEER_QDIM + half, :].astype(bf16))
                          for h in range(PEER_HEADS)], axis=1)
    s2 = jnp.concatenate([_mm(k2_ref[...], qt[h * PEER_QDIM + half:(h + 1) * PEER_QDIM, :].astype(bf16))
                          for h in range(PEER_HEADS)], axis=1)
    t1, i1 = _top_rows(s1, PEER_TOPK)
    t2, i2 = _top_rows(s2, PEER_TOPK)
    rows = lax.broadcasted_iota(i32, (sel_a.shape[0], PEER_HEADS * PEER_TOK), 0).astype(f32)
    cand_s = jnp.where(rows < n_cand, pick(sel_a, t1) + pick(sel_b, t2), -jnp.inf)
    cand_i = pick(sel_a, i1) * N_KEYS + pick(sel_b, i2)
    top_s, top_p = _top_rows(cand_s, PEER_TOPK)
    top_i = jnp.concatenate(
        [jnp.max(jnp.where(rows == top_p[j:j + 1, :], cand_i, -1.0), axis=0, keepdims=True)
         for j in range(PEER_TOPK)], axis=0).astype(i32)
    e = jnp.exp(top_s - top_s[0:1, :])
    gate = e / jnp.sum(e, axis=0, keepdims=True)
    by_head = lambda a: jnp.concatenate([a[:, h * PEER_TOK:(h + 1) * PEER_TOK] for h in range(PEER_HEADS)], axis=0)
    idx_ref[...] = by_head(top_i).T * TABLE_ROWS
    gate_ref[...] = by_head(gate).T


def _peer_select(h2, g, wq_t, k1, k2, batch, seq, length):
    d = h2.shape[-1]
    nt = seq // PEER_TOK
    skip = (length - seq) // PEER_TOK
    nblk = batch * nt
    sel, n_cand = _staircase()
    blk = pl.BlockSpec((None, PEER_TOK, PEER_PAIRS), lambda i, j: (i * nt + j, 0, 0))
    return pl.pallas_call(
        functools.partial(_peer_sel_kernel, n_cand=n_cand),
        grid=(batch, nt),
        in_specs=[pl.BlockSpec((None, PEER_TOK, d), lambda i, j: (i, j + skip, 0)),
                  _const_spec((1, d)), _const_spec(wq_t.shape), _const_spec(k1.shape), _const_spec(k2.shape),
                  _const_spec(sel.shape)],
        out_specs=[pl.BlockSpec((PEER_TOK, d // LANES, LANES), lambda i, j: (i * nt + j, 0, 0)), blk, blk],
        out_shape=[jax.ShapeDtypeStruct((nblk * PEER_TOK, d // LANES, LANES), f32),
                   jax.ShapeDtypeStruct((nblk, PEER_TOK, PEER_PAIRS), i32),
                   jax.ShapeDtypeStruct((nblk, PEER_TOK, PEER_PAIRS), f32)],
        compiler_params=_cparams("parallel", "parallel"),
        name="peer_sel",
    )(h2.reshape(batch, length, d), g, wq_t, k1, k2, sel)


def _pack_table(tab):
    e, d = tab.shape
    assert d == 2 * TABLE_ROWS * LANES
    t = lax.bitcast_convert_type(tab.astype(bf16), jnp.uint16).astype(jnp.uint32).reshape(e, 2, TABLE_ROWS, LANES)
    return lax.bitcast_convert_type((t[:, 0] << 16) | t[:, 1], i32).reshape(e * TABLE_ROWS, LANES)


def _unpack(word):
    return (lax.bitcast_convert_type(word & jnp.int32(-65536), f32), lax.bitcast_convert_type(word << 16, f32))


def _expert_rows(idx_ref, tab_ref, t, k):
    return tab_ref[pl.ds(pl.multiple_of(idx_ref[t, k], TABLE_ROWS), TABLE_ROWS), :]


def _gather(idx_ref, tab_ref, stage_ref, t):
    for k in range(PEER_PAIRS):
        stage_ref[TABLE_ROWS * k:TABLE_ROWS * (k + 1), :] = _expert_rows(idx_ref, tab_ref, t, k)


def _staged(stage_ref, s):
    return _unpack(stage_ref[pl.ds(s, PEER_PAIRS, stride=TABLE_ROWS), :])


def _peer_hid_kernel(idx_ref, x_ref, gate_ref, tab_ref, o_ref, stage_a, stage_b, dots_ref):
    stage = (stage_a, stage_b)
    ones = jnp.ones((8, LANES), bf16)

    def dots(t, buf):
        xt = x_ref[t]
        acc = jnp.zeros((PEER_PAIRS, LANES), f32)
        for s in range(TABLE_ROWS):
            hi, lo = _staged(stage[buf], s)
            acc = acc + hi * xt[s:s + 1, :] + lo * xt[TABLE_ROWS + s:TABLE_ROWS + s + 1, :]
        a_hi = acc.astype(bf16)
        a_lo = (acc - a_hi.astype(f32)).astype(bf16)
        sums = _nt(ones, a_hi) + _nt(ones, a_lo)
        dots_ref[t:t + 1, :] = sums[0:1, :]

    _gather(idx_ref, tab_ref, stage[0], 0)
    _gather(idx_ref, tab_ref, stage[1], 1)
    for t in range(PEER_TOK):
        dots(t, t % 2)
        if t + 2 < PEER_TOK:
            _gather(idx_ref, tab_ref, stage[t % 2], t + 2)
    hid = dots_ref[...]
    o_ref[...] = gate_ref[...] * (0.5 * hid * (1.0 + lax.erf(hid * (2.0 ** -0.5))))


def _two_term_rows(row):
    hi = row.astype(bf16).astype(f32)
    r = lax.broadcasted_iota(i32, (8, LANES), 0)
    return jnp.where(r == 0, hi, jnp.where(r == 1, row - hi, 0.0)).astype(bf16)


def _peer_out_kernel(idx_ref, c_ref, h_ref, tab_ref, o_ref, stage_a, stage_b, acc_ref):
    stage = (stage_a, stage_b)
    row = lax.broadcasted_iota(i32, (8, LANES), 0)
    acc_ref[...] = h_ref[...].reshape(acc_ref.shape)

    def gather(t, buf):
        _gather(idx_ref, tab_ref, stage[buf], t)

    def combine(t, buf):
        lhs = _two_term_rows(c_ref[t:t + 1, :])
        out = acc_ref[t]
        for s in range(TABLE_ROWS):
            hi, lo = _staged(stage[buf], s)
            a = _mm(lhs, hi.astype(bf16))
            b = _mm(lhs, lo.astype(bf16))
            out = (out + jnp.where(row == s, a[0:1, :] + a[1:2, :], 0.0)
                   + jnp.where(row == TABLE_ROWS + s, b[0:1, :] + b[1:2, :], 0.0))
        acc_ref[t] = out

    gather(0, 0)
    gather(1, 1)
    for t in range(PEER_TOK):
        combine(t, t % 2)
        if t + 2 < PEER_TOK:
            gather(t + 2, t % 2)
    o_ref[...] = acc_ref[...].reshape(o_ref.shape)


def _smem_blk(index_map):
    return pl.BlockSpec((None, PEER_TOK, PEER_PAIRS), index_map, memory_space=pltpu.SMEM,
                        pipeline_mode=pl.Buffered(1))


def _table_spec(tab):
    return pl.BlockSpec(tab.shape, lambda *_: (0, 0), pipeline_mode=pl.Buffered(1))


def _peer_hidden(idx, x3, gate, utab):
    nblk = idx.shape[0]
    sub = x3.shape[1]
    vblk = pl.BlockSpec((None, PEER_TOK, PEER_PAIRS), lambda i: (i, 0, 0))
    rows = TABLE_ROWS * PEER_PAIRS
    return pl.pallas_call(
        _peer_hid_kernel,
        grid=(nblk,),
        in_specs=[_smem_blk(lambda i: (i, 0, 0)), pl.BlockSpec((PEER_TOK, sub, LANES), lambda i: (i, 0, 0)), vblk,
                  _table_spec(utab)],
        out_specs=vblk,
        out_shape=jax.ShapeDtypeStruct((nblk, PEER_TOK, PEER_PAIRS), f32),
        scratch_shapes=[pltpu.VMEM((rows, LANES), i32), pltpu.VMEM((rows, LANES), i32),
                        pltpu.VMEM((PEER_TOK, PEER_PAIRS), f32)],
        compiler_params=_cparams("arbitrary"),
        name="peer_hid",
    )(idx, x3, gate, utab)


def _peer_output(idx, coef, h2, vtab, batch, seq, length):
    d = h2.shape[-1]
    sub = d // LANES
    nt = seq // PEER_TOK
    skip = (length - seq) // PEER_TOK
    rows = TABLE_ROWS * PEER_PAIRS
    blk = lambda i, j: (i * nt + j, 0, 0)
    return pl.pallas_call(
        _peer_out_kernel,
        grid=(batch, nt),
        in_specs=[_smem_blk(blk), pl.BlockSpec((None, PEER_TOK, PEER_PAIRS), blk),
                  pl.BlockSpec((None, PEER_TOK, d), lambda i, j: (i, j + skip, 0)),
                  _table_spec(vtab)],
        out_specs=pl.BlockSpec((None, PEER_TOK, d), lambda i, j: (i, j, 0)),
        out_shape=jax.ShapeDtypeStruct((batch, seq, d), f32),
        scratch_shapes=[pltpu.VMEM((rows, LANES), i32), pltpu.VMEM((rows, LANES), i32),
                        pltpu.VMEM((PEER_TOK, sub, LANES), f32)],
        compiler_params=_cparams("arbitrary", "arbitrary"),
        name="peer_out",
    )(idx, coef, h2.reshape(batch, length, d), vtab)


def _pad_cols(w, width):
    return jnp.pad(w, ((0, 0), (0, width - w.shape[1])))


def _pad_rows(w, height):
    return jnp.pad(w, ((0, height - w.shape[0]), (0, 0)))


def kernel(x, meta_tokens, norm1_g, w_in, fox_q_norm, fox_k_norm, fox_f_bias, rwkv_mu, rwkv_w0, rwkv_w_up, rwkv_a0, rwkv_a_up, rwkv_g_up, rwkv_k_k, rwkv_k_a, rwkv_r_k, rwkv_ln_g, rwkv_ln_b, w_branch_fox, w_branch_rwkv, w_out, norm2_g, peer_w_q, peer_sub_k1, peer_sub_k2, peer_u, peer_v):
    batch, seq, d = x.shape
    assert w_in.shape[0] == 1, "one layer"
    assert seq % PEER_TOK == 0 and d % LANES == 0
    t_real = seq + N_META
    pad = (-t_real) % ATTN_BLOCK
    length = t_real + pad
    ta = _attn_block(length)

    meta = jnp.broadcast_to(meta_tokens[None].astype(x.dtype), (batch, N_META, d))
    hpad = jnp.concatenate([jnp.zeros((batch, pad, d), x.dtype), meta, x], axis=1).reshape(batch * length, d)
    w = w_in[0]
    fox_w = 3 * WIDTH + N_HEADS
    o = fox_w
    lora = lambda lo, n, width: _pad_cols(w[:, lo:lo + n], width)
    w_all = jnp.concatenate([
        w[:, 0:3 * WIDTH], _pad_cols(w[:, 3 * WIDTH:fox_w], LANES),
        w[:, o:o + 3 * WIDTH],
        lora(o + 3 * WIDTH, DECAY_LORA, LANES),
        lora(o + 3 * WIDTH + DECAY_LORA, AAA_LORA, LANES),
        lora(o + 3 * WIDTH + DECAY_LORA + AAA_LORA, GATE_LORA, 2 * LANES),
        w[:, o + 3 * WIDTH + DECAY_LORA + AAA_LORA + GATE_LORA:],
    ], axis=1).astype(bf16)
    mu = rwkv_mu[0]
    mu_all = jnp.concatenate([
        mu[0:3 * WIDTH], jnp.pad(mu[3 * WIDTH:3 * WIDTH + DECAY_LORA], (0, LANES - DECAY_LORA)),
        jnp.pad(mu[3 * WIDTH + DECAY_LORA:3 * WIDTH + DECAY_LORA + AAA_LORA], (0, LANES - AAA_LORA)),
        jnp.pad(mu[3 * WIDTH + DECAY_LORA + AAA_LORA:], (0, 2 * LANES - GATE_LORA)),
    ])[None]
    head_of = jnp.arange(WIDTH) // HEAD_DIM
    bdm = (head_of[:, None] == head_of[None, :]).astype(f32) / HEAD_DIM
    vec = lambda p: p[0].reshape(1, -1)

    q, k, v, lf, zr, gates = _proj(hpad, vec(norm1_g), w_all, vec(fox_q_norm), vec(fox_k_norm),
                                   jnp.pad(fox_f_bias[0], (0, LANES - N_HEADS))[None], bdm.astype(bf16))
    negc = _neg_cumsum(lf.reshape(batch, length, LANES), pad, ta)
    three = lambda a: a.reshape(batch, length, a.shape[-1])
    y_fox = _attention(three(q), three(k), three(v), negc, ta)
    y_rwkv = _rwkv(three(zr), mu_all, vec(rwkv_w0), vec(rwkv_a0), vec(rwkv_k_k), vec(rwkv_k_a), vec(rwkv_r_k),
                   vec(rwkv_ln_g), vec(rwkv_ln_b),
                   _pad_rows(rwkv_w_up[0], LANES).astype(bf16), _pad_rows(rwkv_a_up[0], LANES).astype(bf16),
                   _pad_rows(rwkv_g_up[0], 2 * LANES).astype(bf16), bdm.astype(bf16))
    h2 = _merge(y_fox.reshape(-1, WIDTH), y_rwkv.reshape(-1, WIDTH), gates, hpad,
                w_branch_fox[0].astype(bf16), w_branch_rwkv[0].astype(bf16), w_out[0].astype(bf16))

    xn2, idx, gate = _peer_select(h2, vec(norm2_g), peer_w_q[0].T.astype(bf16),
                                  peer_sub_k1[0].astype(bf16), peer_sub_k2[0].astype(bf16), batch, seq, length)
    coef = _peer_hidden(idx, xn2, gate, _pack_table(peer_u[0]))
    out = _peer_output(idx, coef, h2, _pack_table(peer_v[0]), batch, seq, length)
    return out.reshape(batch, seq, d)
```

```python
import functools

import jax
import jax.numpy as jnp
from jax import lax
from jax.experimental import pallas as pl
from jax.experimental.pallas import tpu as pltpu

f32 = jnp.float32
bf16 = jnp.bfloat16
i32 = jnp.int32
HIGHEST = lax.Precision.HIGHEST

N_META = 16
HEAD_DIM = 64
N_HEADS = 8
WIDTH = N_HEADS * HEAD_DIM
DECAY_LORA, AAA_LORA, GATE_LORA = 64, 64, 160
ATTN_BLOCK = 128
PEER_HEADS, PEER_QDIM, N_KEYS, PEER_TOPK = 8, 256, 128, 16
NORM_EPS = 1e-6
RWKV_GN_EPS = 64e-5
MASK_VALUE = -1e30
LOG2_E = 1.4426950408889634

LANES = 128
VMEM_LIMIT_BYTES = 56 * 1024 * 1024

CHUNK = 64
RWKV_ROWS = 2 * CHUNK
PAIR = 2 * HEAD_DIM
PEER_TOK = 128
PEER_PAIRS = PEER_HEADS * PEER_TOPK
TABLE_ROWS = 4
FOX_COLS = 3 * WIDTH + LANES
RWKV_COLS = 3 * WIDTH + 2 * LANES + 2 * LANES


def _row_block(n):
    for t in (512, 384, 256, 128):
        if n % t == 0:
            return t
    raise ValueError(f"row count {n} is not a multiple of 128")


def _attn_block(length):
    return 384 if length % 384 == 0 else ATTN_BLOCK


def _cparams(*sem):
    return pltpu.CompilerParams(dimension_semantics=sem, vmem_limit_bytes=VMEM_LIMIT_BYTES)


def _const_spec(shape):
    nd = len(shape)
    return pl.BlockSpec(shape, lambda *_: (0,) * nd)


def _nt(a, b, **kw):
    return lax.dot_general(a, b, (((1,), (1,)), ((), ())), preferred_element_type=f32, **kw)


def _tn(a, b, **kw):
    return lax.dot_general(a, b, (((0,), (0,)), ((), ())), preferred_element_type=f32, **kw)


def _mm(a, b, **kw):
    return jnp.dot(a, b, preferred_element_type=f32, **kw)


def _softplus(y):
    return jnp.maximum(y, 0.0) + jnp.log1p(jnp.exp(-jnp.abs(y)))


def _sigmoid(y):
    return 1.0 / (1.0 + jnp.exp(-y))


def _three_term_mm(a, x):
    hi = x.astype(bf16)
    r1 = x - hi.astype(f32)
    mid = r1.astype(bf16)
    lo = (r1 - mid.astype(f32)).astype(bf16)
    return _mm(a, hi) + _mm(a, mid) + _mm(a, lo)


def _head_mean(x, bd):
    hi = x.astype(bf16)
    lo = (x - hi.astype(f32)).astype(bf16)
    return _mm(hi, bd) + _mm(lo, bd)


def _proj_kernel(x_ref, g_ref, w_ref, qn_ref, kn_ref, fb_ref, bd_ref,
                 q_ref, k_ref, v_ref, lf_ref, zr_ref, gate_ref):
    x = x_ref[...]
    ms = jnp.mean(x * x, axis=-1, keepdims=True)
    xn = (x * lax.rsqrt(ms + NORM_EPS) * g_ref[...]).astype(bf16)

    def head_norm(z, gain):
        msq = _mm((z * z).astype(bf16), bd_ref[...])
        return z * lax.rsqrt(msq + NORM_EPS) * gain

    zq = _mm(xn, w_ref[:, 0:WIDTH])
    q_ref[...] = (head_norm(zq, qn_ref[...]) * (HEAD_DIM ** -0.5 * LOG2_E)).astype(bf16)
    zk = _mm(xn, w_ref[:, WIDTH:2 * WIDTH])
    k_ref[...] = head_norm(zk, kn_ref[...]).astype(bf16)
    v_ref[...] = _mm(xn, w_ref[:, 2 * WIDTH:3 * WIDTH]).astype(bf16)
    zf = _mm(xn, w_ref[:, 3 * WIDTH:FOX_COLS]) + fb_ref[...]
    lf_ref[...] = -_softplus(-zf) * LOG2_E
    zr_ref[...] = _mm(xn, w_ref[:, FOX_COLS:FOX_COLS + RWKV_COLS])
    zg = _mm(xn, w_ref[:, FOX_COLS + RWKV_COLS:])
    gate_ref[...] = _sigmoid(zg).astype(bf16)


def _proj(hflat, g, w, qn, kn, fb, bd):
    n, d = hflat.shape
    tm = _row_block(n)
    ncol = w.shape[1]
    ngate = ncol - FOX_COLS - RWKV_COLS
    row = lambda c: pl.BlockSpec((tm, c), lambda i: (i, 0))
    return pl.pallas_call(
        _proj_kernel,
        grid=(n // tm,),
        in_specs=[row(d), _const_spec((1, d)), _const_spec((d, ncol)), _const_spec((1, WIDTH)),
                  _const_spec((1, WIDTH)), _const_spec((1, LANES)), _const_spec((WIDTH, WIDTH))],
        out_specs=[row(WIDTH), row(WIDTH), row(WIDTH), row(LANES), row(RWKV_COLS), row(ngate)],
        out_shape=[jax.ShapeDtypeStruct((n, WIDTH), bf16)] * 3
        + [jax.ShapeDtypeStruct((n, LANES), f32), jax.ShapeDtypeStruct((n, RWKV_COLS), f32),
           jax.ShapeDtypeStruct((n, ngate), bf16)],
        compiler_params=_cparams("parallel"),
        name="proj",
    )(hflat, g, w, qn, kn, fb, bd)


def _cumsum_kernel(lf_ref, o_ref, *, pad):
    length = lf_ref.shape[0]
    r = lax.broadcasted_iota(i32, (LANES, LANES), 0)
    c = lax.broadcasted_iota(i32, (LANES, LANES), 1)
    tri = (r >= c).astype(f32)
    carry = jnp.zeros((1, LANES), f32)
    for blk in range(length // LANES):
        cs = _mm(tri, lf_ref[blk * LANES:(blk + 1) * LANES, :], precision=HIGHEST) + carry
        carry = cs[LANES - 1:LANES, :]
        neg = jnp.where(r + blk * LANES >= pad, -cs, MASK_VALUE)
        for h in range(N_HEADS):
            o_ref[h, blk * LANES:(blk + 1) * LANES, :] = jnp.broadcast_to(neg[:, h:h + 1], (LANES, LANES))


def _key_bias(lf, pad):
    b, length, _ = lf.shape
    return pl.pallas_call(
        functools.partial(_cumsum_kernel, pad=pad),
        grid=(b,),
        in_specs=[pl.BlockSpec((None, length, LANES), lambda i: (i, 0, 0))],
        out_specs=pl.BlockSpec((None, N_HEADS, length, LANES), lambda i: (i, 0, 0, 0)),
        out_shape=jax.ShapeDtypeStruct((b, N_HEADS, length, LANES), f32),
        compiler_params=_cparams("parallel"),
        name="cumsum",
    )(lf)


def _attn_kernel(q_ref, k_ref, kb_ref, v_ref, o_ref, *, ta):
    qi = pl.program_id(2)
    lane = lax.broadcasted_iota(i32, (1, PAIR), 1)
    key = lax.broadcasted_iota(i32, (ta, ATTN_BLOCK), 0)
    qry = lax.broadcasted_iota(i32, (ta, ATTN_BLOCK), 1)
    nrow = ta // ATTN_BLOCK
    pcs = [(h, r) for h in range(2) for r in range(nrow)]
    qa = []
    for h, r in pcs:
        qr = q_ref[r * ATTN_BLOCK:(r + 1) * ATTN_BLOCK, :]
        qa.append(jnp.where((lane // HEAD_DIM) == h, qr, jnp.zeros_like(qr)))

    def block(j, carries, diagonal):
        start = pl.multiple_of(j * ta, ta)
        ks = k_ref[pl.ds(start, ta), :]
        vt = v_ref[pl.ds(start, ta), :].T
        bias = [kb_ref[h, pl.ds(start, ta), :] for h in range(2)]
        ids = range(len(pcs))
        s = [_nt(ks, qa[i]) + bias[pcs[i][0]] for i in ids]
        if diagonal:
            s = [jnp.where(key <= qry + r * ATTN_BLOCK, s[i], MASK_VALUE) for i, (h, r) in enumerate(pcs)]
        m_new = [jnp.maximum(carries[i][0], jnp.max(s[i], axis=0, keepdims=True)) for i in ids]
        p = [jnp.exp2(s[i] - m_new[i]) for i in ids]
        alpha = [jnp.exp2(carries[i][0] - m_new[i]) for i in ids]
        l = [alpha[i] * carries[i][1] + jnp.sum(p[i], axis=0, keepdims=True) for i in ids]
        pv = [_mm(vt, p[i].astype(bf16)) for i in ids]
        acc = [alpha[i] * carries[i][2] + pv[i] for i in ids]
        return tuple((m_new[i], l[i], acc[i]) for i in ids)

    init = (jnp.full((1, ATTN_BLOCK), MASK_VALUE, f32), jnp.zeros((1, ATTN_BLOCK), f32),
            jnp.zeros((PAIR, ATTN_BLOCK), f32))
    carries = lax.fori_loop(0, qi, lambda j, c: block(j, c, False), (init,) * len(pcs))
    final = block(qi, carries, True)
    vrow = lax.broadcasted_iota(i32, (PAIR, 1), 0)
    for r in range(nrow):
        o0 = final[r][2] / final[r][1]
        o1 = final[nrow + r][2] / final[nrow + r][1]
        o_ref[r * ATTN_BLOCK:(r + 1) * ATTN_BLOCK, :] = jnp.where(vrow < HEAD_DIM, o0, o1).T.astype(o_ref.dtype)


def _attention(q, k, kbias, v, ta):
    b, length, _ = q.shape
    nb = length // ta
    npair = N_HEADS // 2
    full = pl.BlockSpec((None, length, PAIR), lambda i, p, j: (i, 0, p))
    return pl.pallas_call(
        functools.partial(_attn_kernel, ta=ta),
        grid=(b, npair, nb),
        in_specs=[pl.BlockSpec((None, ta, PAIR), lambda i, p, j: (i, j, p)), full,
                  pl.BlockSpec((None, 2, length, LANES), lambda i, p, j: (i, p, 0, 0)), full],
        out_specs=pl.BlockSpec((None, ta, PAIR), lambda i, p, j: (i, j, p)),
        out_shape=jax.ShapeDtypeStruct((b, length, WIDTH), bf16),
        compiler_params=_cparams("parallel", "parallel", "arbitrary"),
        name="attn",
    )(q, k, kbias, v)


def _stack(x, lane_head):
    zero = jnp.zeros_like(x)
    return jnp.concatenate([jnp.where(lane_head == 0, x, zero), jnp.where(lane_head == 1, x, zero)], axis=0)


def _unit_lower_inverse(ns, r, c):
    eye = (r == c).astype(f32)
    first = ((r ^ c) == 1) & ((r & 1) == 1)
    ts = [(eye - jnp.where(first, n, 0.0)).astype(bf16) for n in ns]
    s = 2
    while s < CHUNK:
        sel = ((r // (2 * s)) == (c // (2 * s))) & ((r & s) != 0) & ((c & s) == 0)
        low = [jnp.where(sel, n, 0.0).astype(bf16) for n in ns]
        tn = [_mm(t, lo).astype(bf16) for t, lo in zip(ts, low)]
        ts = [(t.astype(f32) - _mm(x, t)).astype(bf16) for t, x in zip(ts, tn)]
        s *= 2
    return ts


def _rwkv_kernel(z_ref, mu_ref, w0_ref, a0_ref, kk_ref, ka_ref, rk_ref, lng_ref, lnb_ref,
                 wup_ref, aup_ref, gup_ref, bdm_ref, o_ref, s_ref, prev_ref):
    ci = pl.program_id(1)

    @pl.when(ci == 0)
    def _():
        s_ref[...] = jnp.zeros_like(s_ref)
        prev_ref[...] = jnp.zeros_like(prev_ref)

    z = z_ref[...]
    rows = lax.broadcasted_iota(i32, (RWKV_ROWS, 1), 0)
    zprev = jnp.where(rows == 0, prev_ref[...], pltpu.roll(z, 1, 0))
    prev_ref[...] = z[RWKV_ROWS - 1:RWKV_ROWS, :]
    zs = z + mu_ref[...] * (zprev - z)
    r = zs[:, 0:WIDTH]
    k = zs[:, WIDTH:2 * WIDTH]
    v = zs[:, 2 * WIDTH:3 * WIDTH]
    o1 = 3 * WIDTH
    wd = zs[:, o1:o1 + LANES]
    ad = zs[:, o1 + LANES:o1 + 2 * LANES]
    gd = zs[:, o1 + 2 * LANES:o1 + 4 * LANES]

    w = -_softplus(-(w0_ref[...] + _mm(jnp.tanh(wd).astype(bf16), wup_ref[...]))) - 0.5
    ld = -jnp.exp(w)
    a = _sigmoid(a0_ref[...] + _mm(ad.astype(bf16), aup_ref[...]))
    g = _mm(_sigmoid(gd).astype(bf16), gup_ref[...])
    bdm = bdm_ref[...]
    kk = k * kk_ref[...]
    nrm = jnp.sqrt(_mm((kk * kk).astype(bf16), bdm) * HEAD_DIM)
    kk = kk / jnp.maximum(nrm, 1e-12)
    k2 = k * (1.0 + (a - 1.0) * ka_ref[...])
    b = kk * a

    tr = lax.broadcasted_iota(i32, (RWKV_ROWS, RWKV_ROWS), 0)
    tc = lax.broadcasted_iota(i32, (RWKV_ROWS, RWKV_ROWS), 1)
    tri = ((tr >= tc) & ((tr // CHUNK) == (tc // CHUNK))).astype(bf16)
    lc = _three_term_mm(tri, ld)
    lp = lc - ld
    chunks = range(RWKV_ROWS // CHUNK)
    mids = [lc[c * CHUNK + CHUNK // 2 - 1:c * CHUNK + CHUNK // 2, :] for c in chunks]
    tots = [lc[(c + 1) * CHUNK - 1:(c + 1) * CHUNK, :] for c in chunks]
    mid, tot = mids[-1], tots[-1]
    for c in reversed(chunks[:-1]):
        mid = jnp.where(rows < (c + 1) * CHUNK, mids[c], mid)
        tot = jnp.where(rows < (c + 1) * CHUNK, tots[c], tot)
    kq = kk * jnp.exp(lp - mid)
    rr = r * jnp.exp(lc - mid)
    e_after = jnp.exp(mid - lc)
    kh = k2 * e_after
    bh = b * e_after
    e_end = jnp.exp(tot - lc)
    kc = k2 * e_end
    bc = b * e_end
    e_mid = [jnp.exp(m) for m in mids]
    p_end = [jnp.exp(t) for t in tots]

    sr = lax.broadcasted_iota(i32, (PAIR, PAIR), 0)
    sc = lax.broadcasted_iota(i32, (PAIR, PAIR), 1)
    same = (sr // CHUNK) == (sc // CHUNK)
    strict = same & ((sr % CHUNK) > (sc % CHUNK))
    incl = same & ((sr % CHUNK) >= (sc % CHUNK))
    lane_head = lax.broadcasted_iota(i32, (1, PAIR), 1) // HEAD_DIM

    npair = N_HEADS // 2
    units = [(c, p) for c in chunks for p in range(npair)]
    ids = range(len(units))
    lanes = [slice(p * PAIR, (p + 1) * PAIR) for _, p in units]
    st = lambda x: [_stack(x[c * CHUNK:(c + 1) * CHUNK, p * PAIR:(p + 1) * PAIR], lane_head) for c, p in units]
    kq_s, rr_s, kh_s, bh_s, v_s, kc_s, bc_s = (st(x) for x in (kq, rr, kh, bh, v, kc, bc))
    amat = [_nt(jnp.concatenate([kq_s[i], rr_s[i]], axis=0).astype(bf16),
                jnp.concatenate([kh_s[i], bh_s[i]], axis=0).astype(bf16)) for i in ids]
    a_kk = [jnp.where(strict, amat[i][0:PAIR, 0:PAIR], 0.0).astype(bf16) for i in ids]
    a_kb = [jnp.where(strict, amat[i][0:PAIR, PAIR:], 0.0) for i in ids]
    a_rk = [jnp.where(incl, amat[i][PAIR:, 0:PAIR], 0.0).astype(bf16) for i in ids]
    a_rb = [jnp.where(incl, amat[i][PAIR:, PAIR:], 0.0).astype(bf16) for i in ids]
    t = _unit_lower_inverse(a_kb, sr, sc)
    vb = [v_s[i].astype(bf16) for i in ids]
    av = [_mm(a_kk[i], vb[i]).astype(bf16) for i in ids]
    wm = [_mm(t[i], kq_s[i].astype(bf16)) for i in ids]
    uv = [_mm(t[i], av[i]) for i in ids]
    rq = [rr_s[i] - _mm(a_rb[i], wm[i].astype(bf16)) for i in ids]
    yv = [_mm(a_rk[i], vb[i]) - _mm(a_rb[i], uv[i].astype(bf16)) for i in ids]
    wm_t = [(wm[i] * e_mid[units[i][0]][:, lanes[i]]).astype(bf16) for i in ids]
    rq_t = [(rq[i] * e_mid[units[i][0]][:, lanes[i]]).astype(bf16) for i in ids]
    bcb = [bc_s[i].astype(bf16) for i in ids]
    omega = [_tn(wm_t[i], bcb[i]).astype(bf16) for i in ids]
    psi = [_tn(vb[i], kc_s[i].astype(bf16)) - _tn(uv[i].astype(bf16), bcb[i]) for i in ids]
    state = [s_ref[p] for p in range(npair)]
    ystack = []
    for c in chunks:
        new = []
        for p in range(npair):
            i = c * npair + p
            s_hi = state[p].astype(bf16)
            s_lo = (state[p] - s_hi.astype(f32)).astype(bf16)
            ystack.append(_nt(rq_t[i], s_hi) + yv[i])
            new.append(state[p] * p_end[c][:, lanes[i]] - (_mm(s_hi, omega[i]) + _mm(s_lo, omega[i])) + psi[i])
        state = new
    for p in range(npair):
        s_ref[p] = state[p]
    y = jnp.concatenate(
        [jnp.concatenate([ystack[c * npair + p][0:CHUNK, :] + ystack[c * npair + p][CHUNK:, :] for p in range(npair)],
                         axis=1) for c in chunks], axis=0)

    mean = _head_mean(y, bdm)
    yc = y - mean
    var = _head_mean(yc * yc, bdm)
    yn = yc * lax.rsqrt(var + RWKV_GN_EPS) * lng_ref[...] + lnb_ref[...]
    bonus = _mm((r * k2 * rk_ref[...]).astype(bf16), bdm) * HEAD_DIM * v
    o_ref[...] = ((yn + bonus) * g).astype(o_ref.dtype)


def _rwkv(zr, mu, w0, a0, k_k, k_a, r_k, ln_g, ln_b, w_up, a_up, g_up, bdm):
    b, length, _ = zr.shape
    vec = _const_spec((1, WIDTH))
    return pl.pallas_call(
        _rwkv_kernel,
        grid=(b, length // RWKV_ROWS),
        in_specs=[pl.BlockSpec((None, RWKV_ROWS, RWKV_COLS), lambda i, c: (i, c, 0)),
                  _const_spec((1, RWKV_COLS)), vec, vec, vec, vec, vec, vec, vec,
                  _const_spec((LANES, WIDTH)), _const_spec((LANES, WIDTH)), _const_spec((2 * LANES, WIDTH)),
                  _const_spec((WIDTH, WIDTH))],
        out_specs=pl.BlockSpec((None, RWKV_ROWS, WIDTH), lambda i, c: (i, c, 0)),
        out_shape=jax.ShapeDtypeStruct((b, length, WIDTH), bf16),
        scratch_shapes=[pltpu.VMEM((N_HEADS // 2, PAIR, PAIR), f32), pltpu.VMEM((1, RWKV_COLS), f32)],
        compiler_params=_cparams("parallel", "arbitrary"),
        name="rwkv",
    )(zr, mu, w0, a0, k_k, k_a, r_k, ln_g, ln_b, w_up, a_up, g_up, bdm)


def _merge_kernel(yf_ref, yr_ref, gate_ref, h_ref, wf_ref, wr_ref, wo_ref, o_ref):
    d = h_ref.shape[1]
    pf = _mm(yf_ref[...], wf_ref[...])
    pr = _mm(yr_ref[...], wr_ref[...])
    mixed = gate_ref[:, 0:d].astype(f32) * pf + gate_ref[:, d:2 * d].astype(f32) * pr
    o_ref[...] = h_ref[...] + _mm(mixed.astype(bf16), wo_ref[...])


def _merge(yf, yr, gates, hflat, wf, wr, wo):
    n, d = hflat.shape
    tm = _row_block(n)
    row = lambda c: pl.BlockSpec((tm, c), lambda i: (i, 0))
    return pl.pallas_call(
        _merge_kernel,
        grid=(n // tm,),
        in_specs=[row(WIDTH), row(WIDTH), row(2 * d), row(d),
                  _const_spec((WIDTH, d)), _const_spec((WIDTH, d)), _const_spec((d, d))],
        out_specs=row(d),
        out_shape=jax.ShapeDtypeStruct((n, d), f32),
        compiler_params=_cparams("parallel"),
        name="merge",
    )(yf, yr, gates, hflat, wf, wr, wo)


def _top_rows(s, k):
    n = s.shape[0]
    pos = lax.broadcasted_iota(i32, s.shape, 0).astype(f32)
    vals, outs = [], []
    for _ in range(k):
        m = jnp.max(s, axis=0, keepdims=True)
        first = jnp.min(jnp.where(s == m, pos, float(n)), axis=0, keepdims=True)
        vals.append(m)
        outs.append(first)
        s = jnp.where(pos == first, -jnp.inf, s)
    return jnp.concatenate(vals, axis=0), jnp.concatenate(outs, axis=0)


def _staircase():
    pairs = [(a, b) for a in range(PEER_TOPK) for b in range(PEER_TOPK) if (a + 1) * (b + 1) <= PEER_TOPK]
    rows = -(-len(pairs) // 8) * 8
    sel = jnp.zeros((2, rows, PEER_TOPK), f32)
    r = jnp.arange(len(pairs))
    sel = sel.at[0, r, jnp.array([a for a, _ in pairs])].set(1.0)
    sel = sel.at[1, r, jnp.array([b for _, b in pairs])].set(1.0)
    return sel, len(pairs)


def _peer_sel_kernel(h_ref, g_ref, wq_ref, k1_ref, k2_ref, sel_ref, xn_ref, idx_ref, gate_ref, *, n_cand):
    x = h_ref[...]
    ms = jnp.mean(x * x, axis=-1, keepdims=True)
    xn = x * lax.rsqrt(ms + NORM_EPS) * g_ref[...]
    xn_ref[...] = xn.reshape(xn_ref.shape)
    qt = _nt(wq_ref[...], xn.astype(bf16))
    half = PEER_QDIM // 2
    sel_a, sel_b = sel_ref[0], sel_ref[1]
    pick = lambda sel, t: _mm(sel, t, precision=HIGHEST)
    s1 = jnp.concatenate([_mm(k1_ref[...], qt[h * PEER_QDIM:h * PEER_QDIM + half, :].astype(bf16))
                          for h in range(PEER_HEADS)], axis=1)
    s2 = jnp.concatenate([_mm(k2_ref[...], qt[h * PEER_QDIM + half:(h + 1) * PEER_QDIM, :].astype(bf16))
                          for h in range(PEER_HEADS)], axis=1)
    t1, i1 = _top_rows(s1, PEER_TOPK)
    t2, i2 = _top_rows(s2, PEER_TOPK)
    rows = lax.broadcasted_iota(i32, (sel_a.shape[0], PEER_HEADS * PEER_TOK), 0).astype(f32)
    cand_s = jnp.where(rows < n_cand, pick(sel_a, t1) + pick(sel_b, t2), -jnp.inf)
    cand_i = pick(sel_a, i1) * N_KEYS + pick(sel_b, i2)
    top_s, top_p = _top_rows(cand_s, PEER_TOPK)
    top_i = jnp.concatenate(
        [jnp.max(jnp.where(rows == top_p[j:j + 1, :], cand_i, -1.0), axis=0, keepdims=True)
         for j in range(PEER_TOPK)], axis=0).astype(i32)
    e = jnp.exp(top_s - top_s[0:1, :])
    gate = e / jnp.sum(e, axis=0, keepdims=True)
    by_head = lambda a: jnp.concatenate([a[:, h * PEER_TOK:(h + 1) * PEER_TOK] for h in range(PEER_HEADS)], axis=0)
    idx_ref[...] = by_head(top_i).T * TABLE_ROWS
    gate_ref[...] = by_head(gate).T


def _peer_select(h2, g, wq_t, k1, k2, batch, seq, length):
    d = h2.shape[-1]
    nt = seq // PEER_TOK
    skip = (length - seq) // PEER_TOK
    nblk = batch * nt
    sel, n_cand = _staircase()
    blk = pl.BlockSpec((None, PEER_TOK, PEER_PAIRS), lambda i, j: (i * nt + j, 0, 0))
    return pl.pallas_call(
        functools.partial(_peer_sel_kernel, n_cand=n_cand),
        grid=(batch, nt),
        in_specs=[pl.BlockSpec((None, PEER_TOK, d), lambda i, j: (i, j + skip, 0)),
                  _const_spec((1, d)), _const_spec(wq_t.shape), _const_spec(k1.shape), _const_spec(k2.shape),
                  _const_spec(sel.shape)],
        out_specs=[pl.BlockSpec((PEER_TOK, d // LANES, LANES), lambda i, j: (i * nt + j, 0, 0)), blk, blk],
        out_shape=[jax.ShapeDtypeStruct((nblk * PEER_TOK, d // LANES, LANES), f32),
                   jax.ShapeDtypeStruct((nblk, PEER_TOK, PEER_PAIRS), i32),
                   jax.ShapeDtypeStruct((nblk, PEER_TOK, PEER_PAIRS), f32)],
        compiler_params=_cparams("parallel", "parallel"),
        name="peer_sel",
    )(h2.reshape(batch, length, d), g, wq_t, k1, k2, sel)


def _pack_table(tab):
    e, d = tab.shape
    assert d == 2 * TABLE_ROWS * LANES
    t = lax.bitcast_convert_type(tab.astype(bf16), jnp.uint16).astype(jnp.uint32).reshape(e, 2, TABLE_ROWS, LANES)
    return lax.bitcast_convert_type((t[:, 0] << 16) | t[:, 1], i32).reshape(e * TABLE_ROWS, LANES)


def _unpack(word):
    return (lax.bitcast_convert_type(word & jnp.int32(-65536), f32), lax.bitcast_convert_type(word << 16, f32))


def _expert_rows(idx_ref, tab_ref, t, k):
    return tab_ref[pl.ds(pl.multiple_of(idx_ref[t, k], TABLE_ROWS), TABLE_ROWS), :]


def _gather(idx_ref, tab_ref, stage_ref, t):
    for k in range(PEER_PAIRS):
        stage_ref[TABLE_ROWS * k:TABLE_ROWS * (k + 1), :] = _expert_rows(idx_ref, tab_ref, t, k)


def _staged(stage_ref, s):
    return _unpack(stage_ref[pl.ds(s, PEER_PAIRS, stride=TABLE_ROWS), :])


def _peer_hid_kernel(idx_ref, x_ref, gate_ref, tab_ref, o_ref, stage_a, stage_b, dots_ref):
    stage = (stage_a, stage_b)
    ones = jnp.ones((8, LANES), bf16)

    def dots(t, buf):
        xt = x_ref[t]
        acc = jnp.zeros((PEER_PAIRS, LANES), f32)
        for s in range(TABLE_ROWS):
            hi, lo = _staged(stage[buf], s)
            acc = acc + hi * xt[s:s + 1, :] + lo * xt[TABLE_ROWS + s:TABLE_ROWS + s + 1, :]
        a_hi = acc.astype(bf16)
        a_lo = (acc - a_hi.astype(f32)).astype(bf16)
        sums = _nt(ones, a_hi) + _nt(ones, a_lo)
        dots_ref[t:t + 1, :] = sums[0:1, :]

    _gather(idx_ref, tab_ref, stage[0], 0)
    _gather(idx_ref, tab_ref, stage[1], 1)
    for t in range(PEER_TOK):
        dots(t, t % 2)
        if t + 2 < PEER_TOK:
            _gather(idx_ref, tab_ref, stage[t % 2], t + 2)
    hid = dots_ref[...]
    o_ref[...] = gate_ref[...] * (0.5 * hid * (1.0 + lax.erf(hid * (2.0 ** -0.5))))


def _two_term_rows(row):
    hi = row.astype(bf16).astype(f32)
    r = lax.broadcasted_iota(i32, (8, LANES), 0)
    return jnp.where(r == 0, hi, jnp.where(r == 1, row - hi, 0.0)).astype(bf16)


def _peer_out_kernel(idx_ref, c_ref, h_ref, tab_ref, o_ref, stage_a, stage_b, acc_ref):
    stage = (stage_a, stage_b)
    row = lax.broadcasted_iota(i32, (8, LANES), 0)
    acc_ref[...] = h_ref[...].reshape(acc_ref.shape)

    def gather(t, buf):
        _gather(idx_ref, tab_ref, stage[buf], t)

    def combine(t, buf):
        lhs = _two_term_rows(c_ref[t:t + 1, :])
        out = acc_ref[t]
        for s in range(TABLE_ROWS):
            hi, lo = _staged(stage[buf], s)
            a = _mm(lhs, hi.astype(bf16))
            b = _mm(lhs, lo.astype(bf16))
            out = (out + jnp.where(row == s, a[0:1, :] + a[1:2, :], 0.0)
                   + jnp.where(row == TABLE_ROWS + s, b[0:1, :] + b[1:2, :], 0.0))
        acc_ref[t] = out

    gather(0, 0)
    gather(1, 1)
    for t in range(PEER_TOK):
        combine(t, t % 2)
        if t + 2 < PEER_TOK:
            gather(t + 2, t % 2)
    o_ref[...] = acc_ref[...].reshape(o_ref.shape)


def _smem_blk(index_map):
    return pl.BlockSpec((None, PEER_TOK, PEER_PAIRS), index_map, memory_space=pltpu.SMEM,
                        pipeline_mode=pl.Buffered(1))


def _table_spec(tab):
    return pl.BlockSpec(tab.shape, lambda *_: (0, 0), pipeline_mode=pl.Buffered(1))


def _peer_hidden(idx, x3, gate, utab):
    nblk = idx.shape[0]
    sub = x3.shape[1]
    vblk = pl.BlockSpec((None, PEER_TOK, PEER_PAIRS), lambda i: (i, 0, 0))
    rows = TABLE_ROWS * PEER_PAIRS
    return pl.pallas_call(
        _peer_hid_kernel,
        grid=(nblk,),
        in_specs=[_smem_blk(lambda i: (i, 0, 0)), pl.BlockSpec((PEER_TOK, sub, LANES), lambda i: (i, 0, 0)), vblk,
                  _table_spec(utab)],
        out_specs=vblk,
        out_shape=jax.ShapeDtypeStruct((nblk, PEER_TOK, PEER_PAIRS), f32),
        scratch_shapes=[pltpu.VMEM((rows, LANES), i32), pltpu.VMEM((rows, LANES), i32),
                        pltpu.VMEM((PEER_TOK, PEER_PAIRS), f32)],
        compiler_params=_cparams("arbitrary"),
        name="peer_hid",
    )(idx, x3, gate, utab)


def _peer_output(idx, coef, h2, vtab, batch, seq, length):
    d = h2.shape[-1]
    sub = d // LANES
    nt = seq // PEER_TOK
    skip = (length - seq) // PEER_TOK
    rows = TABLE_ROWS * PEER_PAIRS
    blk = lambda i, j: (i * nt + j, 0, 0)
    return pl.pallas_call(
        _peer_out_kernel,
        grid=(batch, nt),
        in_specs=[_smem_blk(blk), pl.BlockSpec((None, PEER_TOK, PEER_PAIRS), blk),
                  pl.BlockSpec((None, PEER_TOK, d), lambda i, j: (i, j + skip, 0)),
                  _table_spec(vtab)],
        out_specs=pl.BlockSpec((None, PEER_TOK, d), lambda i, j: (i, j, 0)),
        out_shape=jax.ShapeDtypeStruct((batch, seq, d), f32),
        scratch_shapes=[pltpu.VMEM((rows, LANES), i32), pltpu.VMEM((rows, LANES), i32),
                        pltpu.VMEM((PEER_TOK, sub, LANES), f32)],
        compiler_params=_cparams("arbitrary", "arbitrary"),
        name="peer_out",
    )(idx, coef, h2.reshape(batch, length, d), vtab)


def _pad_cols(w, width):
    return jnp.pad(w, ((0, 0), (0, width - w.shape[1])))


def _pad_rows(w, height):
    return jnp.pad(w, ((0, height - w.shape[0]), (0, 0)))


def kernel(x, meta_tokens, norm1_g, w_in, fox_q_norm, fox_k_norm, fox_f_bias, rwkv_mu, rwkv_w0, rwkv_w_up, rwkv_a0, rwkv_a_up, rwkv_g_up, rwkv_k_k, rwkv_k_a, rwkv_r_k, rwkv_ln_g, rwkv_ln_b, w_branch_fox, w_branch_rwkv, w_out, norm2_g, peer_w_q, peer_sub_k1, peer_sub_k2, peer_u, peer_v):
    batch, seq, d = x.shape
    assert w_in.shape[0] == 1, "one layer"
    assert seq % PEER_TOK == 0 and d % LANES == 0
    t_real = seq + N_META
    pad = (-t_real) % ATTN_BLOCK
    length = t_real + pad
    ta = _attn_block(length)

    meta = jnp.broadcast_to(meta_tokens[None].astype(x.dtype), (batch, N_META, d))
    hpad = jnp.concatenate([jnp.zeros((batch, pad, d), x.dtype), meta, x], axis=1).reshape(batch * length, d)
    w = w_in[0]
    fox_w = 3 * WIDTH + N_HEADS
    o = fox_w
    lora = lambda lo, n, width: _pad_cols(w[:, lo:lo + n], width)
    w_all = jnp.concatenate([
        w[:, 0:3 * WIDTH], _pad_cols(w[:, 3 * WIDTH:fox_w], LANES),
        w[:, o:o + 3 * WIDTH],
        lora(o + 3 * WIDTH, DECAY_LORA, LANES),
        lora(o + 3 * WIDTH + DECAY_LORA, AAA_LORA, LANES),
        lora(o + 3 * WIDTH + DECAY_LORA + AAA_LORA, GATE_LORA, 2 * LANES),
        w[:, o + 3 * WIDTH + DECAY_LORA + AAA_LORA + GATE_LORA:],
    ], axis=1).astype(bf16)
    mu = rwkv_mu[0]
    mu_all = jnp.concatenate([
        mu[0:3 * WIDTH], jnp.pad(mu[3 * WIDTH:3 * WIDTH + DECAY_LORA], (0, LANES - DECAY_LORA)),
        jnp.pad(mu[3 * WIDTH + DECAY_LORA:3 * WIDTH + DECAY_LORA + AAA_LORA], (0, LANES - AAA_LORA)),
        jnp.pad(mu[3 * WIDTH + DECAY_LORA + AAA_LORA:], (0, 2 * LANES - GATE_LORA)),
    ])[None]
    head_of = jnp.arange(WIDTH) // HEAD_DIM
    bdm = (head_of[:, None] == head_of[None, :]).astype(f32) / HEAD_DIM
    vec = lambda p: p[0].reshape(1, -1)

    q, k, v, lf, zr, gates = _proj(hpad, vec(norm1_g), w_all, vec(fox_q_norm), vec(fox_k_norm),
                                   jnp.pad(fox_f_bias[0], (0, LANES - N_HEADS))[None], bdm.astype(bf16))
    three = lambda a: a.reshape(batch, length, a.shape[-1])
    y_fox = _attention(three(q), three(k), _key_bias(three(lf), pad), three(v), ta)
    y_rwkv = _rwkv(three(zr), mu_all, vec(rwkv_w0), vec(rwkv_a0), vec(rwkv_k_k), vec(rwkv_k_a), vec(rwkv_r_k),
                   vec(rwkv_ln_g), vec(rwkv_ln_b),
                   _pad_rows(rwkv_w_up[0], LANES).astype(bf16), _pad_rows(rwkv_a_up[0], LANES).astype(bf16),
                   _pad_rows(rwkv_g_up[0], 2 * LANES).astype(bf16), bdm.astype(bf16))
    h2 = _merge(y_fox.reshape(-1, WIDTH), y_rwkv.reshape(-1, WIDTH), gates, hpad,
                w_branch_fox[0].astype(bf16), w_branch_rwkv[0].astype(bf16), w_out[0].astype(bf16))

    xn2, idx, gate = _peer_select(h2, vec(norm2_g), peer_w_q[0].T.astype(bf16),
                                  peer_sub_k1[0].astype(bf16), peer_sub_k2[0].astype(bf16), batch, seq, length)
    coef = _peer_hidden(idx, xn2, gate, _pack_table(peer_u[0]))
    out = _peer_output(idx, coef, h2, _pack_table(peer_v[0]), batch, seq, length)
    return out.reshape(batch, seq, d)
```

```python
import functools

import jax
import jax.numpy as jnp
from jax import lax
from jax.experimental import pallas as pl
from jax.experimental.pallas import tpu as pltpu

f32 = jnp.float32
bf16 = jnp.bfloat16
i32 = jnp.int32
HIGHEST = lax.Precision.HIGHEST

N_META = 16
HEAD_DIM = 64
N_HEADS = 8
WIDTH = N_HEADS * HEAD_DIM
DECAY_LORA, AAA_LORA, GATE_LORA = 64, 64, 160
ATTN_BLOCK = 128
PEER_HEADS, PEER_QDIM, N_KEYS, PEER_TOPK = 8, 256, 128, 16
NORM_EPS = 1e-6
RWKV_GN_EPS = 64e-5
MASK_VALUE = -1e30
LOG2_E = 1.4426950408889634

LANES = 128
VMEM_LIMIT_BYTES = 56 * 1024 * 1024

CHUNK = 64
RWKV_ROWS = 2 * CHUNK
PAIR = 2 * HEAD_DIM
PEER_TOK = 128
PEER_PAIRS = PEER_HEADS * PEER_TOPK
PEER_STAGES = 2
TABLE_ROWS = 4
FOX_COLS = 3 * WIDTH + LANES
RWKV_COLS = 3 * WIDTH + 2 * LANES + 2 * LANES


def _row_block(n):
    for t in (512, 384, 256, 128):
        if n % t == 0:
            return t
    raise ValueError(f"row count {n} is not a multiple of 128")


def _attn_block(length):
    return 384 if length % 384 == 0 else ATTN_BLOCK


def _cparams(*sem):
    return pltpu.CompilerParams(dimension_semantics=sem, vmem_limit_bytes=VMEM_LIMIT_BYTES)


def _const_spec(shape):
    nd = len(shape)
    return pl.BlockSpec(shape, lambda *_: (0,) * nd)


def _nt(a, b, **kw):
    return lax.dot_general(a, b, (((1,), (1,)), ((), ())), preferred_element_type=f32, **kw)


def _tn(a, b, **kw):
    return lax.dot_general(a, b, (((0,), (0,)), ((), ())), preferred_element_type=f32, **kw)


def _mm(a, b, **kw):
    return jnp.dot(a, b, preferred_element_type=f32, **kw)


def _softplus(y):
    return jnp.maximum(y, 0.0) + jnp.log1p(jnp.exp(-jnp.abs(y)))


def _sigmoid(y):
    return 1.0 / (1.0 + jnp.exp(-y))


def _three_term_mm(a, x):
    hi = x.astype(bf16)
    r1 = x - hi.astype(f32)
    mid = r1.astype(bf16)
    lo = (r1 - mid.astype(f32)).astype(bf16)
    return _mm(a, hi) + _mm(a, mid) + _mm(a, lo)


def _head_mean(x, bd):
    hi = x.astype(bf16)
    lo = (x - hi.astype(f32)).astype(bf16)
    return _mm(hi, bd) + _mm(lo, bd)


def _proj_kernel(x_ref, g_ref, w_ref, qn_ref, kn_ref, fb_ref, bd_ref,
                 q_ref, k_ref, v_ref, lf_ref, zr_ref, gate_ref):
    x = x_ref[...]
    ms = jnp.mean(x * x, axis=-1, keepdims=True)
    xn = (x * lax.rsqrt(ms + NORM_EPS) * g_ref[...]).astype(bf16)

    def head_norm(z, gain):
        msq = _mm((z * z).astype(bf16), bd_ref[...])
        return z * lax.rsqrt(msq + NORM_EPS) * gain

    zq = _mm(xn, w_ref[:, 0:WIDTH])
    q_ref[...] = (head_norm(zq, qn_ref[...]) * (HEAD_DIM ** -0.5 * LOG2_E)).astype(bf16)
    zk = _mm(xn, w_ref[:, WIDTH:2 * WIDTH])
    k_ref[...] = head_norm(zk, kn_ref[...]).astype(bf16)
    v_ref[...] = _mm(xn, w_ref[:, 2 * WIDTH:3 * WIDTH]).astype(bf16)
    zf = _mm(xn, w_ref[:, 3 * WIDTH:FOX_COLS]) + fb_ref[...]
    lf_ref[...] = -_softplus(-zf) * LOG2_E
    zr_ref[...] = _mm(xn, w_ref[:, FOX_COLS:FOX_COLS + RWKV_COLS])
    zg = _mm(xn, w_ref[:, FOX_COLS + RWKV_COLS:])
    gate_ref[...] = _sigmoid(zg).astype(bf16)


def _proj(hflat, g, w, qn, kn, fb, bd):
    n, d = hflat.shape
    tm = _row_block(n)
    ncol = w.shape[1]
    ngate = ncol - FOX_COLS - RWKV_COLS
    row = lambda c: pl.BlockSpec((tm, c), lambda i: (i, 0))
    return pl.pallas_call(
        _proj_kernel,
        grid=(n // tm,),
        in_specs=[row(d), _const_spec((1, d)), _const_spec((d, ncol)), _const_spec((1, WIDTH)),
                  _const_spec((1, WIDTH)), _const_spec((1, LANES)), _const_spec((WIDTH, WIDTH))],
        out_specs=[row(WIDTH), row(WIDTH), row(WIDTH), row(LANES), row(RWKV_COLS), row(ngate)],
        out_shape=[jax.ShapeDtypeStruct((n, WIDTH), bf16)] * 3
        + [jax.ShapeDtypeStruct((n, LANES), f32), jax.ShapeDtypeStruct((n, RWKV_COLS), f32),
           jax.ShapeDtypeStruct((n, ngate), bf16)],
        compiler_params=_cparams("parallel"),
        name="proj",
    )(hflat, g, w, qn, kn, fb, bd)


def _cumsum_kernel(lf_ref, o_ref, *, pad):
    length = lf_ref.shape[0]
    r = lax.broadcasted_iota(i32, (LANES, LANES), 0)
    c = lax.broadcasted_iota(i32, (LANES, LANES), 1)
    tri = (r >= c).astype(f32)
    carry = jnp.zeros((1, LANES), f32)
    for blk in range(length // LANES):
        cs = _mm(tri, lf_ref[blk * LANES:(blk + 1) * LANES, :], precision=HIGHEST) + carry
        carry = cs[LANES - 1:LANES, :]
        neg = jnp.where(r + blk * LANES >= pad, -cs, MASK_VALUE)
        for h in range(N_HEADS):
            o_ref[h, blk * LANES:(blk + 1) * LANES, :] = jnp.broadcast_to(neg[:, h:h + 1], (LANES, LANES))


def _key_bias(lf, pad):
    b, length, _ = lf.shape
    return pl.pallas_call(
        functools.partial(_cumsum_kernel, pad=pad),
        grid=(b,),
        in_specs=[pl.BlockSpec((None, length, LANES), lambda i: (i, 0, 0))],
        out_specs=pl.BlockSpec((None, N_HEADS, length, LANES), lambda i: (i, 0, 0, 0)),
        out_shape=jax.ShapeDtypeStruct((b, N_HEADS, length, LANES), f32),
        compiler_params=_cparams("parallel"),
        name="cumsum",
    )(lf)


def _attn_kernel(q_ref, k_ref, kb_ref, v_ref, o_ref, *, ta):
    qi = pl.program_id(2)
    lane = lax.broadcasted_iota(i32, (1, PAIR), 1)
    key = lax.broadcasted_iota(i32, (ta, ATTN_BLOCK), 0)
    qry = lax.broadcasted_iota(i32, (ta, ATTN_BLOCK), 1)
    nrow = ta // ATTN_BLOCK
    pcs = [(h, r) for h in range(2) for r in range(nrow)]
    qa = []
    for h, r in pcs:
        qr = q_ref[r * ATTN_BLOCK:(r + 1) * ATTN_BLOCK, :]
        qa.append(jnp.where((lane // HEAD_DIM) == h, qr, jnp.zeros_like(qr)))

    def block(j, carries, diagonal):
        start = pl.multiple_of(j * ta, ta)
        ks = k_ref[pl.ds(start, ta), :]
        vt = v_ref[pl.ds(start, ta), :].T
        bias = [kb_ref[h, pl.ds(start, ta), :] for h in range(2)]
        ids = range(len(pcs))
        s = [_nt(ks, qa[i]) + bias[pcs[i][0]] for i in ids]
        if diagonal:
            s = [jnp.where(key <= qry + r * ATTN_BLOCK, s[i], MASK_VALUE) for i, (h, r) in enumerate(pcs)]
        m_new = [jnp.maximum(carries[i][0], jnp.max(s[i], axis=0, keepdims=True)) for i in ids]
        p = [jnp.exp2(s[i] - m_new[i]) for i in ids]
        alpha = [jnp.exp2(carries[i][0] - m_new[i]) for i in ids]
        l = [alpha[i] * carries[i][1] + jnp.sum(p[i], axis=0, keepdims=True) for i in ids]
        pv = [_mm(vt, p[i].astype(bf16)) for i in ids]
        acc = [alpha[i] * carries[i][2] + pv[i] for i in ids]
        return tuple((m_new[i], l[i], acc[i]) for i in ids)

    init = (jnp.full((1, ATTN_BLOCK), MASK_VALUE, f32), jnp.zeros((1, ATTN_BLOCK), f32),
            jnp.zeros((PAIR, ATTN_BLOCK), f32))
    carries = lax.fori_loop(0, qi // 2, lambda j, c: block(2 * j + 1, block(2 * j, c, False), False),
                            (init,) * len(pcs))
    carries = lax.cond(qi % 2 == 1, lambda c: block(qi - 1, c, False), lambda c: c, carries)
    final = block(qi, carries, True)
    vrow = lax.broadcasted_iota(i32, (PAIR, 1), 0)
    for r in range(nrow):
        o0 = final[r][2] / final[r][1]
        o1 = final[nrow + r][2] / final[nrow + r][1]
        o_ref[r * ATTN_BLOCK:(r + 1) * ATTN_BLOCK, :] = jnp.where(vrow < HEAD_DIM, o0, o1).T.astype(o_ref.dtype)


def _attention(q, k, kbias, v, ta):
    b, length, _ = q.shape
    nb = length // ta
    npair = N_HEADS // 2
    full = pl.BlockSpec((None, length, PAIR), lambda i, p, j: (i, 0, p))
    return pl.pallas_call(
        functools.partial(_attn_kernel, ta=ta),
        grid=(b, npair, nb),
        in_specs=[pl.BlockSpec((None, ta, PAIR), lambda i, p, j: (i, j, p)), full,
                  pl.BlockSpec((None, 2, length, LANES), lambda i, p, j: (i, p, 0, 0)), full],
        out_specs=pl.BlockSpec((None, ta, PAIR), lambda i, p, j: (i, j, p)),
        out_shape=jax.ShapeDtypeStruct((b, length, WIDTH), bf16),
        compiler_params=_cparams("parallel", "parallel", "arbitrary"),
        name="attn",
    )(q, k, kbias, v)


def _stack(x, lane_head):
    zero = jnp.zeros_like(x)
    return jnp.concatenate([jnp.where(lane_head == 0, x, zero), jnp.where(lane_head == 1, x, zero)], axis=0)


def _unit_lower_inverse(ns, r, c):
    eye = (r == c).astype(f32)
    first = ((r ^ c) == 1) & ((r & 1) == 1)
    ts = [(eye - jnp.where(first, n, 0.0)).astype(bf16) for n in ns]
    s = 2
    while s < CHUNK:
        sel = ((r // (2 * s)) == (c // (2 * s))) & ((r & s) != 0) & ((c & s) == 0)
        low = [jnp.where(sel, n, 0.0).astype(bf16) for n in ns]
        tn = [_mm(t, lo).astype(bf16) for t, lo in zip(ts, low)]
        ts = [(t.astype(f32) - _mm(x, t)).astype(bf16) for t, x in zip(ts, tn)]
        s *= 2
    return ts


def _rwkv_kernel(z_ref, mu_ref, w0_ref, a0_ref, kk_ref, ka_ref, rk_ref, lng_ref, lnb_ref,
                 wup_ref, aup_ref, gup_ref, bdm_ref, o_ref, s_ref, prev_ref):
    ci = pl.program_id(1)

    @pl.when(ci == 0)
    def _():
        s_ref[...] = jnp.zeros_like(s_ref)
        prev_ref[...] = jnp.zeros_like(prev_ref)

    z = z_ref[...]
    rows = lax.broadcasted_iota(i32, (RWKV_ROWS, 1), 0)
    zprev = jnp.where(rows == 0, prev_ref[...], pltpu.roll(z, 1, 0))
    prev_ref[...] = z[RWKV_ROWS - 1:RWKV_ROWS, :]
    zs = z + mu_ref[...] * (zprev - z)
    r = zs[:, 0:WIDTH]
    k = zs[:, WIDTH:2 * WIDTH]
    v = zs[:, 2 * WIDTH:3 * WIDTH]
    o1 = 3 * WIDTH
    wd = zs[:, o1:o1 + LANES]
    ad = zs[:, o1 + LANES:o1 + 2 * LANES]
    gd = zs[:, o1 + 2 * LANES:o1 + 4 * LANES]

    w = -_softplus(-(w0_ref[...] + _mm(jnp.tanh(wd).astype(bf16), wup_ref[...]))) - 0.5
    ld = -jnp.exp(w)
    a = _sigmoid(a0_ref[...] + _mm(ad.astype(bf16), aup_ref[...]))
    g = _mm(_sigmoid(gd).astype(bf16), gup_ref[...])
    bdm = bdm_ref[...]
    kk = k * kk_ref[...]
    nrm = jnp.sqrt(_mm((kk * kk).astype(bf16), bdm) * HEAD_DIM)
    kk = kk / jnp.maximum(nrm, 1e-12)
    k2 = k * (1.0 + (a - 1.0) * ka_ref[...])
    b = kk * a

    tr = lax.broadcasted_iota(i32, (RWKV_ROWS, RWKV_ROWS), 0)
    tc = lax.broadcasted_iota(i32, (RWKV_ROWS, RWKV_ROWS), 1)
    tri = ((tr >= tc) & ((tr // CHUNK) == (tc // CHUNK))).astype(bf16)
    lc = _three_term_mm(tri, ld)
    lp = lc - ld
    chunks = range(RWKV_ROWS // CHUNK)
    mids = [lc[c * CHUNK + CHUNK // 2 - 1:c * CHUNK + CHUNK // 2, :] for c in chunks]
    tots = [lc[(c + 1) * CHUNK - 1:(c + 1) * CHUNK, :] for c in chunks]
    mid, tot = mids[-1], tots[-1]
    for c in reversed(chunks[:-1]):
        mid = jnp.where(rows < (c + 1) * CHUNK, mids[c], mid)
        tot = jnp.where(rows < (c + 1) * CHUNK, tots[c], tot)
    kq = kk * jnp.exp(lp - mid)
    rr = r * jnp.exp(lc - mid)
    e_after = jnp.exp(mid - lc)
    kh = k2 * e_after
    bh = b * e_after
    e_end = jnp.exp(tot - lc)
    kc = k2 * e_end
    bc = b * e_end
    e_mid = [jnp.exp(m) for m in mids]
    p_end = [jnp.exp(t) for t in tots]

    sr = lax.broadcasted_iota(i32, (PAIR, PAIR), 0)
    sc = lax.broadcasted_iota(i32, (PAIR, PAIR), 1)
    same = (sr // CHUNK) == (sc // CHUNK)
    strict = same & ((sr % CHUNK) > (sc % CHUNK))
    incl = same & ((sr % CHUNK) >= (sc % CHUNK))
    lane_head = lax.broadcasted_iota(i32, (1, PAIR), 1) // HEAD_DIM

    npair = N_HEADS // 2
    units = [(c, p) for c in chunks for p in range(npair)]
    ids = range(len(units))
    lanes = [slice(p * PAIR, (p + 1) * PAIR) for _, p in units]
    st = lambda x: [_stack(x[c * CHUNK:(c + 1) * CHUNK, p * PAIR:(p + 1) * PAIR], lane_head) for c, p in units]
    kq_s, rr_s, kh_s, bh_s, v_s, kc_s, bc_s = (st(x) for x in (kq, rr, kh, bh, v, kc, bc))
    amat = [_nt(jnp.concatenate([kq_s[i], rr_s[i]], axis=0).astype(bf16),
                jnp.concatenate([kh_s[i], bh_s[i]], axis=0).astype(bf16)) for i in ids]
    a_kk = [jnp.where(strict, amat[i][0:PAIR, 0:PAIR], 0.0).astype(bf16) for i in ids]
    a_kb = [jnp.where(strict, amat[i][0:PAIR, PAIR:], 0.0) for i in ids]
    a_rk = [jnp.where(incl, amat[i][PAIR:, 0:PAIR], 0.0).astype(bf16) for i in ids]
    a_rb = [jnp.where(incl, amat[i][PAIR:, PAIR:], 0.0).astype(bf16) for i in ids]
    t = _unit_lower_inverse(a_kb, sr, sc)
    vb = [v_s[i].astype(bf16) for i in ids]
    av = [_mm(a_kk[i], vb[i]).astype(bf16) for i in ids]
    wm = [_mm(t[i], kq_s[i].astype(bf16)) for i in ids]
    uv = [_mm(t[i], av[i]) for i in ids]
    rq = [rr_s[i] - _mm(a_rb[i], wm[i].astype(bf16)) for i in ids]
    yv = [_mm(a_rk[i], vb[i]) - _mm(a_rb[i], uv[i].astype(bf16)) for i in ids]
    wm_t = [(wm[i] * e_mid[units[i][0]][:, lanes[i]]).astype(bf16) for i in ids]
    rq_t = [(rq[i] * e_mid[units[i][0]][:, lanes[i]]).astype(bf16) for i in ids]
    bcb = [bc_s[i].astype(bf16) for i in ids]
    omega = [_tn(wm_t[i], bcb[i]).astype(bf16) for i in ids]
    psi = [_tn(vb[i], kc_s[i].astype(bf16)) - _tn(uv[i].astype(bf16), bcb[i]) for i in ids]
    state = [s_ref[p] for p in range(npair)]
    ystack = []
    for c in chunks:
        new = []
        for p in range(npair):
            i = c * npair + p
            s_hi = state[p].astype(bf16)
            s_lo = (state[p] - s_hi.astype(f32)).astype(bf16)
            ystack.append(_nt(rq_t[i], s_hi) + yv[i])
            new.append(state[p] * p_end[c][:, lanes[i]] - (_mm(s_hi, omega[i]) + _mm(s_lo, omega[i])) + psi[i])
        state = new
    for p in range(npair):
        s_ref[p] = state[p]
    y = jnp.concatenate(
        [jnp.concatenate([ystack[c * npair + p][0:CHUNK, :] + ystack[c * npair + p][CHUNK:, :] for p in range(npair)],
                         axis=1) for c in chunks], axis=0)

    mean = _head_mean(y, bdm)
    yc = y - mean
    var = _head_mean(yc * yc, bdm)
    yn = yc * lax.rsqrt(var + RWKV_GN_EPS) * lng_ref[...] + lnb_ref[...]
    bonus = _mm((r * k2 * rk_ref[...]).astype(bf16), bdm) * HEAD_DIM * v
    o_ref[...] = ((yn + bonus) * g).astype(o_ref.dtype)


def _rwkv(zr, mu, w0, a0, k_k, k_a, r_k, ln_g, ln_b, w_up, a_up, g_up, bdm):
    b, length, _ = zr.shape
    vec = _const_spec((1, WIDTH))
    return pl.pallas_call(
        _rwkv_kernel,
        grid=(b, length // RWKV_ROWS),
        in_specs=[pl.BlockSpec((None, RWKV_ROWS, RWKV_COLS), lambda i, c: (i, c, 0)),
                  _const_spec((1, RWKV_COLS)), vec, vec, vec, vec, vec, vec, vec,
                  _const_spec((LANES, WIDTH)), _const_spec((LANES, WIDTH)), _const_spec((2 * LANES, WIDTH)),
                  _const_spec((WIDTH, WIDTH))],
        out_specs=pl.BlockSpec((None, RWKV_ROWS, WIDTH), lambda i, c: (i, c, 0)),
        out_shape=jax.ShapeDtypeStruct((b, length, WIDTH), bf16),
        scratch_shapes=[pltpu.VMEM((N_HEADS // 2, PAIR, PAIR), f32), pltpu.VMEM((1, RWKV_COLS), f32)],
        compiler_params=_cparams("parallel", "arbitrary"),
        name="rwkv",
    )(zr, mu, w0, a0, k_k, k_a, r_k, ln_g, ln_b, w_up, a_up, g_up, bdm)


def _merge_kernel(yf_ref, yr_ref, gate_ref, h_ref, wf_ref, wr_ref, wo_ref, o_ref):
    d = h_ref.shape[1]
    pf = _mm(yf_ref[...], wf_ref[...])
    pr = _mm(yr_ref[...], wr_ref[...])
    mixed = gate_ref[:, 0:d].astype(f32) * pf + gate_ref[:, d:2 * d].astype(f32) * pr
    o_ref[...] = h_ref[...] + _mm(mixed.astype(bf16), wo_ref[...])


def _merge(yf, yr, gates, hflat, wf, wr, wo):
    n, d = hflat.shape
    tm = _row_block(n)
    row = lambda c: pl.BlockSpec((tm, c), lambda i: (i, 0))
    return pl.pallas_call(
        _merge_kernel,
        grid=(n // tm,),
        in_specs=[row(WIDTH), row(WIDTH), row(2 * d), row(d),
                  _const_spec((WIDTH, d)), _const_spec((WIDTH, d)), _const_spec((d, d))],
        out_specs=row(d),
        out_shape=jax.ShapeDtypeStruct((n, d), f32),
        compiler_params=_cparams("parallel"),
        name="merge",
    )(yf, yr, gates, hflat, wf, wr, wo)


def _top_rows(s, k):
    n = s.shape[0]
    pos = lax.broadcasted_iota(i32, s.shape, 0).astype(f32)
    vals, outs = [], []
    for _ in range(k):
        m = jnp.max(s, axis=0, keepdims=True)
        first = jnp.min(jnp.where(s == m, pos, float(n)), axis=0, keepdims=True)
        vals.append(m)
        outs.append(first)
        s = jnp.where(pos == first, -jnp.inf, s)
    return jnp.concatenate(vals, axis=0), jnp.concatenate(outs, axis=0)


def _staircase():
    pairs = [(a, b) for a in range(PEER_TOPK) for b in range(PEER_TOPK) if (a + 1) * (b + 1) <= PEER_TOPK]
    rows = -(-len(pairs) // 8) * 8
    sel = jnp.zeros((2, rows, PEER_TOPK), f32)
    r = jnp.arange(len(pairs))
    sel = sel.at[0, r, jnp.array([a for a, _ in pairs])].set(1.0)
    sel = sel.at[1, r, jnp.array([b for _, b in pairs])].set(1.0)
    return sel, len(pairs)


def _peer_sel_kernel(h_ref, g_ref, wq_ref, k1_ref, k2_ref, sel_ref, xn_ref, idx_ref, gate_ref, *, n_cand):
    x = h_ref[...]
    ms = jnp.mean(x * x, axis=-1, keepdims=True)
    xn = x * lax.rsqrt(ms + NORM_EPS) * g_ref[...]
    xn_ref[...] = xn.reshape(xn_ref.shape)
    qt = _nt(wq_ref[...], xn.astype(bf16))
    half = PEER_QDIM // 2
    sel_a, sel_b = sel_ref[0], sel_ref[1]
    pick = lambda sel, t: _mm(sel, t, precision=HIGHEST)
    s1 = jnp.concatenate([_mm(k1_ref[...], qt[h * PEER_QDIM:h * PEER_QDIM + half, :].astype(bf16))
                          for h in range(PEER_HEADS)], axis=1)
    s2 = jnp.concatenate([_mm(k2_ref[...], qt[h * PEER_QDIM + half:(h + 1) * PEER_QDIM, :].astype(bf16))
                          for h in range(PEER_HEADS)], axis=1)
    t1, i1 = _top_rows(s1, PEER_TOPK)
    t2, i2 = _top_rows(s2, PEER_TOPK)
    rows = lax.broadcasted_iota(i32, (sel_a.shape[0], PEER_HEADS * PEER_TOK), 0).astype(f32)
    cand_s = jnp.where(rows < n_cand, pick(sel_a, t1) + pick(sel_b, t2), -jnp.inf)
    cand_i = pick(sel_a, i1) * N_KEYS + pick(sel_b, i2)
    top_s, top_p = _top_rows(cand_s, PEER_TOPK)
    top_i = jnp.concatenate(
        [jnp.max(jnp.where(rows == top_p[j:j + 1, :], cand_i, -1.0), axis=0, keepdims=True)
         for j in range(PEER_TOPK)], axis=0).astype(i32)
    e = jnp.exp(top_s - top_s[0:1, :])
    gate = e / jnp.sum(e, axis=0, keepdims=True)
    by_head = lambda a: jnp.concatenate([a[:, h * PEER_TOK:(h + 1) * PEER_TOK] for h in range(PEER_HEADS)], axis=0)
    idx_ref[...] = by_head(top_i).T * TABLE_ROWS
    gate_ref[...] = by_head(gate).T


def _peer_select(h2, g, wq_t, k1, k2, batch, seq, length):
    d = h2.shape[-1]
    nt = seq // PEER_TOK
    skip = (length - seq) // PEER_TOK
    nblk = batch * nt
    sel, n_cand = _staircase()
    blk = pl.BlockSpec((None, PEER_TOK, PEER_PAIRS), lambda i, j: (i * nt + j, 0, 0))
    return pl.pallas_call(
        functools.partial(_peer_sel_kernel, n_cand=n_cand),
        grid=(batch, nt),
        in_specs=[pl.BlockSpec((None, PEER_TOK, d), lambda i, j: (i, j + skip, 0)),
                  _const_spec((1, d)), _const_spec(wq_t.shape), _const_spec(k1.shape), _const_spec(k2.shape),
                  _const_spec(sel.shape)],
        out_specs=[pl.BlockSpec((PEER_TOK, d // LANES, LANES), lambda i, j: (i * nt + j, 0, 0)), blk, blk],
        out_shape=[jax.ShapeDtypeStruct((nblk * PEER_TOK, d // LANES, LANES), f32),
                   jax.ShapeDtypeStruct((nblk, PEER_TOK, PEER_PAIRS), i32),
                   jax.ShapeDtypeStruct((nblk, PEER_TOK, PEER_PAIRS), f32)],
        compiler_params=_cparams("parallel", "parallel"),
        name="peer_sel",
    )(h2.reshape(batch, length, d), g, wq_t, k1, k2, sel)


def _pack_table(tab):
    e, d = tab.shape
    assert d == 2 * TABLE_ROWS * LANES
    t = lax.bitcast_convert_type(tab.astype(bf16), jnp.uint16).astype(jnp.uint32).reshape(e, 2, TABLE_ROWS, LANES)
    return lax.bitcast_convert_type((t[:, 0] << 16) | t[:, 1], i32).reshape(e * TABLE_ROWS, LANES)


def _unpack(word):
    return (lax.bitcast_convert_type(word & jnp.int32(-65536), f32), lax.bitcast_convert_type(word << 16, f32))


def _expert_rows(idx_ref, tab_ref, t, k):
    return tab_ref[pl.ds(pl.multiple_of(idx_ref[t, k], TABLE_ROWS), TABLE_ROWS), :]


def _gather(idx_ref, tab_ref, stage_ref, t):
    for k in range(PEER_PAIRS):
        stage_ref[TABLE_ROWS * k:TABLE_ROWS * (k + 1), :] = _expert_rows(idx_ref, tab_ref, t, k)


def _staged(stage_ref, s):
    return _unpack(stage_ref[pl.ds(s, PEER_PAIRS, stride=TABLE_ROWS), :])


def _token_pipeline(idx_ref, tab_ref, stage, consume):
    ahead = len(stage)
    for t in range(ahead):
        _gather(idx_ref, tab_ref, stage[t], t)
    for t in range(PEER_TOK):
        consume(t, stage[t % ahead])
        if t + ahead < PEER_TOK:
            _gather(idx_ref, tab_ref, stage[t % ahead], t + ahead)


def _peer_hid_kernel(idx_ref, x_ref, gate_ref, tab_ref, o_ref, *scratch):
    stage, dots_ref = scratch[:-1], scratch[-1]
    ones = jnp.ones((8, LANES), bf16)

    def dots(t, buf):
        xt = x_ref[t]
        acc = jnp.zeros((PEER_PAIRS, LANES), f32)
        for s in range(TABLE_ROWS):
            hi, lo = _staged(buf, s)
            acc = acc + hi * xt[s:s + 1, :] + lo * xt[TABLE_ROWS + s:TABLE_ROWS + s + 1, :]
        a_hi = acc.astype(bf16)
        a_lo = (acc - a_hi.astype(f32)).astype(bf16)
        sums = _nt(ones, a_hi) + _nt(ones, a_lo)
        dots_ref[t:t + 1, :] = sums[0:1, :]

    _token_pipeline(idx_ref, tab_ref, stage, dots)
    hid = dots_ref[...]
    o_ref[...] = gate_ref[...] * (0.5 * hid * (1.0 + lax.erf(hid * (2.0 ** -0.5))))


def _two_term_rows(row):
    hi = row.astype(bf16).astype(f32)
    r = lax.broadcasted_iota(i32, (8, LANES), 0)
    return jnp.where(r == 0, hi, jnp.where(r == 1, row - hi, 0.0)).astype(bf16)


def _peer_out_kernel(idx_ref, c_ref, h_ref, tab_ref, o_ref, *scratch):
    stage, acc_ref = scratch[:-1], scratch[-1]
    row = lax.broadcasted_iota(i32, (8, LANES), 0)
    acc_ref[...] = h_ref[...].reshape(acc_ref.shape)

    def combine(t, buf):
        lhs = _two_term_rows(c_ref[t:t + 1, :])
        out = acc_ref[t]
        for s in range(TABLE_ROWS):
            hi, lo = _staged(buf, s)
            a = _mm(lhs, hi.astype(bf16))
            b = _mm(lhs, lo.astype(bf16))
            out = (out + jnp.where(row == s, a[0:1, :] + a[1:2, :], 0.0)
                   + jnp.where(row == TABLE_ROWS + s, b[0:1, :] + b[1:2, :], 0.0))
        acc_ref[t] = out

    _token_pipeline(idx_ref, tab_ref, stage, combine)
    o_ref[...] = acc_ref[...].reshape(o_ref.shape)


def _smem_blk(index_map):
    return pl.BlockSpec((None, PEER_TOK, PEER_PAIRS), index_map, memory_space=pltpu.SMEM,
                        pipeline_mode=pl.Buffered(1))


def _table_spec(tab):
    return pl.BlockSpec(tab.shape, lambda *_: (0, 0), pipeline_mode=pl.Buffered(1))


def _peer_hidden(idx, x3, gate, utab):
    nblk = idx.shape[0]
    sub = x3.shape[1]
    vblk = pl.BlockSpec((None, PEER_TOK, PEER_PAIRS), lambda i: (i, 0, 0))
    rows = TABLE_ROWS * PEER_PAIRS
    return pl.pallas_call(
        _peer_hid_kernel,
        grid=(nblk,),
        in_specs=[_smem_blk(lambda i: (i, 0, 0)), pl.BlockSpec((PEER_TOK, sub, LANES), lambda i: (i, 0, 0)), vblk,
                  _table_spec(utab)],
        out_specs=vblk,
        out_shape=jax.ShapeDtypeStruct((nblk, PEER_TOK, PEER_PAIRS), f32),
        scratch_shapes=[pltpu.VMEM((rows, LANES), i32)] * PEER_STAGES + [pltpu.VMEM((PEER_TOK, PEER_PAIRS), f32)],
        compiler_params=_cparams("arbitrary"),
        name="peer_hid",
    )(idx, x3, gate, utab)


def _peer_output(idx, coef, h2, vtab, batch, seq, length):
    d = h2.shape[-1]
    sub = d // LANES
    nt = seq // PEER_TOK
    skip = (length - seq) // PEER_TOK
    rows = TABLE_ROWS * PEER_PAIRS
    blk = lambda i, j: (i * nt + j, 0, 0)
    return pl.pallas_call(
        _peer_out_kernel,
        grid=(batch, nt),
        in_specs=[_smem_blk(blk), pl.BlockSpec((None, PEER_TOK, PEER_PAIRS), blk),
                  pl.BlockSpec((None, PEER_TOK, d), lambda i, j: (i, j + skip, 0)),
                  _table_spec(vtab)],
        out_specs=pl.BlockSpec((None, PEER_TOK, d), lambda i, j: (i, j, 0)),
        out_shape=jax.ShapeDtypeStruct((batch, seq, d), f32),
        scratch_shapes=[pltpu.VMEM((rows, LANES), i32)] * PEER_STAGES + [pltpu.VMEM((PEER_TOK, sub, LANES), f32)],
        compiler_params=_cparams("arbitrary", "arbitrary"),
        name="peer_out",
    )(idx, coef, h2.reshape(batch, length, d), vtab)


def _pad_cols(w, width):
    return jnp.pad(w, ((0, 0), (0, width - w.shape[1])))


def _pad_rows(w, height):
    return jnp.pad(w, ((0, height - w.shape[0]), (0, 0)))


def kernel(x, meta_tokens, norm1_g, w_in, fox_q_norm, fox_k_norm, fox_f_bias, rwkv_mu, rwkv_w0, rwkv_w_up, rwkv_a0, rwkv_a_up, rwkv_g_up, rwkv_k_k, rwkv_k_a, rwkv_r_k, rwkv_ln_g, rwkv_ln_b, w_branch_fox, w_branch_rwkv, w_out, norm2_g, peer_w_q, peer_sub_k1, peer_sub_k2, peer_u, peer_v):
    batch, seq, d = x.shape
    assert w_in.shape[0] == 1, "one layer"
    assert seq % PEER_TOK == 0 and d % LANES == 0
    t_real = seq + N_META
    pad = (-t_real) % ATTN_BLOCK
    length = t_real + pad
    ta = _attn_block(length)

    meta = jnp.broadcast_to(meta_tokens[None].astype(x.dtype), (batch, N_META, d))
    hpad = jnp.concatenate([jnp.zeros((batch, pad, d), x.dtype), meta, x], axis=1).reshape(batch * length, d)
    w = w_in[0]
    fox_w = 3 * WIDTH + N_HEADS
    o = fox_w
    lora = lambda lo, n, width: _pad_cols(w[:, lo:lo + n], width)
    w_all = jnp.concatenate([
        w[:, 0:3 * WIDTH], _pad_cols(w[:, 3 * WIDTH:fox_w], LANES),
        w[:, o:o + 3 * WIDTH],
        lora(o + 3 * WIDTH, DECAY_LORA, LANES),
        lora(o + 3 * WIDTH + DECAY_LORA, AAA_LORA, LANES),
        lora(o + 3 * WIDTH + DECAY_LORA + AAA_LORA, GATE_LORA, 2 * LANES),
        w[:, o + 3 * WIDTH + DECAY_LORA + AAA_LORA + GATE_LORA:],
    ], axis=1).astype(bf16)
    mu = rwkv_mu[0]
    mu_all = jnp.concatenate([
        mu[0:3 * WIDTH], jnp.pad(mu[3 * WIDTH:3 * WIDTH + DECAY_LORA], (0, LANES - DECAY_LORA)),
        jnp.pad(mu[3 * WIDTH + DECAY_LORA:3 * WIDTH + DECAY_LORA + AAA_LORA], (0, LANES - AAA_LORA)),
        jnp.pad(mu[3 * WIDTH + DECAY_LORA + AAA_LORA:], (0, 2 * LANES - GATE_LORA)),
    ])[None]
    head_of = jnp.arange(WIDTH) // HEAD_DIM
    bdm = (head_of[:, None] == head_of[None, :]).astype(f32) / HEAD_DIM
    vec = lambda p: p[0].reshape(1, -1)

    q, k, v, lf, zr, gates = _proj(hpad, vec(norm1_g), w_all, vec(fox_q_norm), vec(fox_k_norm),
                                   jnp.pad(fox_f_bias[0], (0, LANES - N_HEADS))[None], bdm.astype(bf16))
    three = lambda a: a.reshape(batch, length, a.shape[-1])
    y_fox = _attention(three(q), three(k), _key_bias(three(lf), pad), three(v), ta)
    y_rwkv = _rwkv(three(zr), mu_all, vec(rwkv_w0), vec(rwkv_a0), vec(rwkv_k_k), vec(rwkv_k_a), vec(rwkv_r_k),
                   vec(rwkv_ln_g), vec(rwkv_ln_b),
                   _pad_rows(rwkv_w_up[0], LANES).astype(bf16), _pad_rows(rwkv_a_up[0], LANES).astype(bf16),
                   _pad_rows(rwkv_g_up[0], 2 * LANES).astype(bf16), bdm.astype(bf16))
    h2 = _merge(y_fox.reshape(-1, WIDTH), y_rwkv.reshape(-1, WIDTH), gates, hpad,
                w_branch_fox[0].astype(bf16), w_branch_rwkv[0].astype(bf16), w_out[0].astype(bf16))

    xn2, idx, gate = _peer_select(h2, vec(norm2_g), peer_w_q[0].T.astype(bf16),
                                  peer_sub_k1[0].astype(bf16), peer_sub_k2[0].astype(bf16), batch, seq, length)
    coef = _peer_hidden(idx, xn2, gate, _pack_table(peer_u[0]))
    out = _peer_output(idx, coef, h2, _pack_table(peer_v[0]), batch, seq, length)
    return out.reshape(batch, seq, d)
```

```python
import functools

import jax
import jax.numpy as jnp
from jax import lax
from jax.experimental import pallas as pl
from jax.experimental.pallas import tpu as pltpu

f32 = jnp.float32
bf16 = jnp.bfloat16
i32 = jnp.int32
HIGHEST = lax.Precision.HIGHEST

N_META = 16
HEAD_DIM = 64
N_HEADS = 8
WIDTH = N_HEADS * HEAD_DIM
DECAY_LORA, AAA_LORA, GATE_LORA = 64, 64, 160
ATTN_BLOCK = 128
PEER_HEADS, PEER_QDIM, N_KEYS, PEER_TOPK = 8, 256, 128, 16
NORM_EPS = 1e-6
RWKV_GN_EPS = 64e-5
MASK_VALUE = -1e30
LOG2_E = 1.4426950408889634

LANES = 128
VMEM_LIMIT_BYTES = 56 * 1024 * 1024

CHUNK = 64
RWKV_ROWS = 6 * CHUNK
PAIR = 2 * HEAD_DIM
PEER_TOK = 128
PEER_PAIRS = PEER_HEADS * PEER_TOPK
PEER_STAGES = 2
TABLE_ROWS = 4
FOX_COLS = 3 * WIDTH + LANES
RWKV_COLS = 3 * WIDTH + 2 * LANES + 2 * LANES


def _row_block(n):
    for t in (512, 384, 256, 128):
        if n % t == 0:
            return t
    raise ValueError(f"row count {n} is not a multiple of 128")


def _attn_block(length):
    return 384 if length % 384 == 0 else ATTN_BLOCK


def _cparams(*sem):
    return pltpu.CompilerParams(dimension_semantics=sem, vmem_limit_bytes=VMEM_LIMIT_BYTES)


def _const_spec(shape):
    nd = len(shape)
    return pl.BlockSpec(shape, lambda *_: (0,) * nd)


def _nt(a, b, **kw):
    return lax.dot_general(a, b, (((1,), (1,)), ((), ())), preferred_element_type=f32, **kw)


def _tn(a, b, **kw):
    return lax.dot_general(a, b, (((0,), (0,)), ((), ())), preferred_element_type=f32, **kw)


def _mm(a, b, **kw):
    return jnp.dot(a, b, preferred_element_type=f32, **kw)


def _softplus(y):
    return jnp.maximum(y, 0.0) + jnp.log1p(jnp.exp(-jnp.abs(y)))


def _sigmoid(y):
    return 1.0 / (1.0 + jnp.exp(-y))


def _three_term_mm(a, x):
    hi = x.astype(bf16)
    r1 = x - hi.astype(f32)
    mid = r1.astype(bf16)
    lo = (r1 - mid.astype(f32)).astype(bf16)
    return _mm(a, hi) + _mm(a, mid) + _mm(a, lo)


def _head_mean(x, bd):
    hi = x.astype(bf16)
    lo = (x - hi.astype(f32)).astype(bf16)
    return _mm(hi, bd) + _mm(lo, bd)


def _proj_kernel(x_ref, g_ref, w_ref, qn_ref, kn_ref, fb_ref, bd_ref,
                 q_ref, k_ref, v_ref, lf_ref, zr_ref, gate_ref):
    x = x_ref[...]
    ms = jnp.mean(x * x, axis=-1, keepdims=True)
    xn = (x * lax.rsqrt(ms + NORM_EPS) * g_ref[...]).astype(bf16)

    def head_norm(z, gain):
        msq = _mm((z * z).astype(bf16), bd_ref[...])
        return z * lax.rsqrt(msq + NORM_EPS) * gain

    zq = _mm(xn, w_ref[:, 0:WIDTH])
    q_ref[...] = (head_norm(zq, qn_ref[...]) * (HEAD_DIM ** -0.5 * LOG2_E)).astype(bf16)
    zk = _mm(xn, w_ref[:, WIDTH:2 * WIDTH])
    k_ref[...] = head_norm(zk, kn_ref[...]).astype(bf16)
    v_ref[...] = _mm(xn, w_ref[:, 2 * WIDTH:3 * WIDTH]).astype(bf16)
    zf = _mm(xn, w_ref[:, 3 * WIDTH:FOX_COLS]) + fb_ref[...]
    lf_ref[...] = -_softplus(-zf) * LOG2_E
    zr_ref[...] = _mm(xn, w_ref[:, FOX_COLS:FOX_COLS + RWKV_COLS])
    zg = _mm(xn, w_ref[:, FOX_COLS + RWKV_COLS:])
    gate_ref[...] = _sigmoid(zg).astype(bf16)


def _proj(hflat, g, w, qn, kn, fb, bd):
    n, d = hflat.shape
    tm = _row_block(n)
    ncol = w.shape[1]
    ngate = ncol - FOX_COLS - RWKV_COLS
    row = lambda c: pl.BlockSpec((tm, c), lambda i: (i, 0))
    return pl.pallas_call(
        _proj_kernel,
        grid=(n // tm,),
        in_specs=[row(d), _const_spec((1, d)), _const_spec((d, ncol)), _const_spec((1, WIDTH)),
                  _const_spec((1, WIDTH)), _const_spec((1, LANES)), _const_spec((WIDTH, WIDTH))],
        out_specs=[row(WIDTH), row(WIDTH), row(WIDTH), row(LANES), row(RWKV_COLS), row(ngate)],
        out_shape=[jax.ShapeDtypeStruct((n, WIDTH), bf16)] * 3
        + [jax.ShapeDtypeStruct((n, LANES), f32), jax.ShapeDtypeStruct((n, RWKV_COLS), f32),
           jax.ShapeDtypeStruct((n, ngate), bf16)],
        compiler_params=_cparams("parallel"),
        name="proj",
    )(hflat, g, w, qn, kn, fb, bd)


def _cumsum_kernel(lf_ref, o_ref, *, pad):
    length = lf_ref.shape[0]
    r = lax.broadcasted_iota(i32, (LANES, LANES), 0)
    c = lax.broadcasted_iota(i32, (LANES, LANES), 1)
    tri = (r >= c).astype(f32)
    carry = jnp.zeros((1, LANES), f32)
    for blk in range(length // LANES):
        cs = _mm(tri, lf_ref[blk * LANES:(blk + 1) * LANES, :], precision=HIGHEST) + carry
        carry = cs[LANES - 1:LANES, :]
        neg = jnp.where(r + blk * LANES >= pad, -cs, MASK_VALUE)
        for h in range(N_HEADS):
            o_ref[h, blk * LANES:(blk + 1) * LANES, :] = jnp.broadcast_to(neg[:, h:h + 1], (LANES, LANES))


def _key_bias(lf, pad):
    b, length, _ = lf.shape
    return pl.pallas_call(
        functools.partial(_cumsum_kernel, pad=pad),
        grid=(b,),
        in_specs=[pl.BlockSpec((None, length, LANES), lambda i: (i, 0, 0))],
        out_specs=pl.BlockSpec((None, N_HEADS, length, LANES), lambda i: (i, 0, 0, 0)),
        out_shape=jax.ShapeDtypeStruct((b, N_HEADS, length, LANES), f32),
        compiler_params=_cparams("parallel"),
        name="cumsum",
    )(lf)


def _attn_kernel(q_ref, k_ref, kb_ref, v_ref, o_ref, *, ta):
    qi = pl.program_id(2)
    lane = lax.broadcasted_iota(i32, (1, PAIR), 1)
    key = lax.broadcasted_iota(i32, (ta, ATTN_BLOCK), 0)
    qry = lax.broadcasted_iota(i32, (ta, ATTN_BLOCK), 1)
    nrow = ta // ATTN_BLOCK
    pcs = [(h, r) for h in range(2) for r in range(nrow)]
    qa = []
    for h, r in pcs:
        qr = q_ref[r * ATTN_BLOCK:(r + 1) * ATTN_BLOCK, :]
        qa.append(jnp.where((lane // HEAD_DIM) == h, qr, jnp.zeros_like(qr)))

    def block(j, carries, diagonal):
        start = pl.multiple_of(j * ta, ta)
        ks = k_ref[pl.ds(start, ta), :]
        vt = v_ref[pl.ds(start, ta), :].T
        bias = [kb_ref[h, pl.ds(start, ta), :] for h in range(2)]
        ids = range(len(pcs))
        s = [_nt(ks, qa[i]) + bias[pcs[i][0]] for i in ids]
        if diagonal:
            s = [jnp.where(key <= qry + r * ATTN_BLOCK, s[i], MASK_VALUE) for i, (h, r) in enumerate(pcs)]
        m_new = [jnp.maximum(carries[i][0], jnp.max(s[i], axis=0, keepdims=True)) for i in ids]
        p = [jnp.exp2(s[i] - m_new[i]) for i in ids]
        alpha = [jnp.exp2(carries[i][0] - m_new[i]) for i in ids]
        l = [alpha[i] * carries[i][1] + jnp.sum(p[i], axis=0, keepdims=True) for i in ids]
        pv = [_mm(vt, p[i].astype(bf16)) for i in ids]
        acc = [alpha[i] * carries[i][2] + pv[i] for i in ids]
        return tuple((m_new[i], l[i], acc[i]) for i in ids)

    init = (jnp.full((1, ATTN_BLOCK), MASK_VALUE, f32), jnp.zeros((1, ATTN_BLOCK), f32),
            jnp.zeros((PAIR, ATTN_BLOCK), f32))
    carries = lax.fori_loop(0, qi // 2, lambda j, c: block(2 * j + 1, block(2 * j, c, False), False),
                            (init,) * len(pcs))
    carries = lax.cond(qi % 2 == 1, lambda c: block(qi - 1, c, False), lambda c: c, carries)
    final = block(qi, carries, True)
    vrow = lax.broadcasted_iota(i32, (PAIR, 1), 0)
    for r in range(nrow):
        o0 = final[r][2] / final[r][1]
        o1 = final[nrow + r][2] / final[nrow + r][1]
        o_ref[r * ATTN_BLOCK:(r + 1) * ATTN_BLOCK, :] = jnp.where(vrow < HEAD_DIM, o0, o1).T.astype(o_ref.dtype)


def _attention(q, k, kbias, v, ta):
    b, length, _ = q.shape
    nb = length // ta
    npair = N_HEADS // 2
    full = pl.BlockSpec((None, length, PAIR), lambda i, p, j: (i, 0, p))
    return pl.pallas_call(
        functools.partial(_attn_kernel, ta=ta),
        grid=(b, npair, nb),
        in_specs=[pl.BlockSpec((None, ta, PAIR), lambda i, p, j: (i, j, p)), full,
                  pl.BlockSpec((None, 2, length, LANES), lambda i, p, j: (i, p, 0, 0)), full],
        out_specs=pl.BlockSpec((None, ta, PAIR), lambda i, p, j: (i, j, p)),
        out_shape=jax.ShapeDtypeStruct((b, length, WIDTH), bf16),
        compiler_params=_cparams("parallel", "parallel", "arbitrary"),
        name="attn",
    )(q, k, kbias, v)


def _stack(x, lane_head):
    zero = jnp.zeros_like(x)
    return jnp.concatenate([jnp.where(lane_head == 0, x, zero), jnp.where(lane_head == 1, x, zero)], axis=0)


def _unit_lower_inverse(ns, r, c):
    eye = (r == c).astype(f32)
    first = ((r ^ c) == 1) & ((r & 1) == 1)
    ts = [(eye - jnp.where(first, n, 0.0)).astype(bf16) for n in ns]
    s = 2
    while s < CHUNK:
        sel = ((r // (2 * s)) == (c // (2 * s))) & ((r & s) != 0) & ((c & s) == 0)
        low = [jnp.where(sel, n, 0.0).astype(bf16) for n in ns]
        tn = [_mm(t, lo).astype(bf16) for t, lo in zip(ts, low)]
        ts = [(t.astype(f32) - _mm(x, t)).astype(bf16) for t, x in zip(ts, tn)]
        s *= 2
    return ts


def _rwkv_kernel(z_ref, mu_ref, w0_ref, a0_ref, kk_ref, ka_ref, rk_ref, lng_ref, lnb_ref,
                 wup_ref, aup_ref, gup_ref, bdm_ref, o_ref, s_ref, prev_ref):
    ci = pl.program_id(1)

    @pl.when(ci == 0)
    def _():
        s_ref[...] = jnp.zeros_like(s_ref)
        prev_ref[...] = jnp.zeros_like(prev_ref)

    z = z_ref[...]
    rows = lax.broadcasted_iota(i32, (RWKV_ROWS, 1), 0)
    zprev = jnp.where(rows == 0, prev_ref[...], pltpu.roll(z, 1, 0))
    prev_ref[...] = z[RWKV_ROWS - 1:RWKV_ROWS, :]
    zs = z + mu_ref[...] * (zprev - z)
    r = zs[:, 0:WIDTH]
    k = zs[:, WIDTH:2 * WIDTH]
    v = zs[:, 2 * WIDTH:3 * WIDTH]
    o1 = 3 * WIDTH
    wd = zs[:, o1:o1 + LANES]
    ad = zs[:, o1 + LANES:o1 + 2 * LANES]
    gd = zs[:, o1 + 2 * LANES:o1 + 4 * LANES]

    w = -_softplus(-(w0_ref[...] + _mm(jnp.tanh(wd).astype(bf16), wup_ref[...]))) - 0.5
    ld = -jnp.exp(w)
    a = _sigmoid(a0_ref[...] + _mm(ad.astype(bf16), aup_ref[...]))
    g = _mm(_sigmoid(gd).astype(bf16), gup_ref[...])
    bdm = bdm_ref[...]
    kk = k * kk_ref[...]
    nrm = jnp.sqrt(_mm((kk * kk).astype(bf16), bdm) * HEAD_DIM)
    kk = kk / jnp.maximum(nrm, 1e-12)
    k2 = k * (1.0 + (a - 1.0) * ka_ref[...])
    b = kk * a

    tr = lax.broadcasted_iota(i32, (RWKV_ROWS, RWKV_ROWS), 0)
    tc = lax.broadcasted_iota(i32, (RWKV_ROWS, RWKV_ROWS), 1)
    tri = ((tr >= tc) & ((tr // CHUNK) == (tc // CHUNK))).astype(bf16)
    lc = _three_term_mm(tri, ld)
    lp = lc - ld
    chunks = range(RWKV_ROWS // CHUNK)
    mids = [lc[c * CHUNK + CHUNK // 2 - 1:c * CHUNK + CHUNK // 2, :] for c in chunks]
    tots = [lc[(c + 1) * CHUNK - 1:(c + 1) * CHUNK, :] for c in chunks]
    mid, tot = mids[-1], tots[-1]
    for c in reversed(chunks[:-1]):
        mid = jnp.where(rows < (c + 1) * CHUNK, mids[c], mid)
        tot = jnp.where(rows < (c + 1) * CHUNK, tots[c], tot)
    kq = kk * jnp.exp(lp - mid)
    rr = r * jnp.exp(lc - mid)
    e_after = jnp.exp(mid - lc)
    kh = k2 * e_after
    bh = b * e_after
    e_end = jnp.exp(tot - lc)
    kc = k2 * e_end
    bc = b * e_end
    e_mid = [jnp.exp(m) for m in mids]
    p_end = [jnp.exp(t) for t in tots]

    sr = lax.broadcasted_iota(i32, (PAIR, PAIR), 0)
    sc = lax.broadcasted_iota(i32, (PAIR, PAIR), 1)
    same = (sr // CHUNK) == (sc // CHUNK)
    strict = same & ((sr % CHUNK) > (sc % CHUNK))
    incl = same & ((sr % CHUNK) >= (sc % CHUNK))
    lane_head = lax.broadcasted_iota(i32, (1, PAIR), 1) // HEAD_DIM

    npair = N_HEADS // 2
    units = [(c, p) for c in chunks for p in range(npair)]
    ids = range(len(units))
    lanes = [slice(p * PAIR, (p + 1) * PAIR) for _, p in units]
    st = lambda x: [_stack(x[c * CHUNK:(c + 1) * CHUNK, p * PAIR:(p + 1) * PAIR], lane_head) for c, p in units]
    kq_s, rr_s, kh_s, bh_s, v_s, kc_s, bc_s = (st(x) for x in (kq, rr, kh, bh, v, kc, bc))
    amat = [_nt(jnp.concatenate([kq_s[i], rr_s[i]], axis=0).astype(bf16),
                jnp.concatenate([kh_s[i], bh_s[i]], axis=0).astype(bf16)) for i in ids]
    a_kk = [jnp.where(strict, amat[i][0:PAIR, 0:PAIR], 0.0).astype(bf16) for i in ids]
    a_kb = [jnp.where(strict, amat[i][0:PAIR, PAIR:], 0.0) for i in ids]
    a_rk = [jnp.where(incl, amat[i][PAIR:, 0:PAIR], 0.0).astype(bf16) for i in ids]
    a_rb = [jnp.where(incl, amat[i][PAIR:, PAIR:], 0.0).astype(bf16) for i in ids]
    t = _unit_lower_inverse(a_kb, sr, sc)
    vb = [v_s[i].astype(bf16) for i in ids]
    av = [_mm(a_kk[i], vb[i]).astype(bf16) for i in ids]
    wm = [_mm(t[i], kq_s[i].astype(bf16)) for i in ids]
    uv = [_mm(t[i], av[i]) for i in ids]
    rq = [rr_s[i] - _mm(a_rb[i], wm[i].astype(bf16)) for i in ids]
    yv = [_mm(a_rk[i], vb[i]) - _mm(a_rb[i], uv[i].astype(bf16)) for i in ids]
    wm_t = [(wm[i] * e_mid[units[i][0]][:, lanes[i]]).astype(bf16) for i in ids]
    rq_t = [(rq[i] * e_mid[units[i][0]][:, lanes[i]]).astype(bf16) for i in ids]
    bcb = [bc_s[i].astype(bf16) for i in ids]
    omega = [_tn(wm_t[i], bcb[i]).astype(bf16) for i in ids]
    psi = [_tn(vb[i], kc_s[i].astype(bf16)) - _tn(uv[i].astype(bf16), bcb[i]) for i in ids]
    state = [s_ref[p] for p in range(npair)]
    ystack = []
    for c in chunks:
        new = []
        for p in range(npair):
            i = c * npair + p
            s_hi = state[p].astype(bf16)
            s_lo = (state[p] - s_hi.astype(f32)).astype(bf16)
            ystack.append(_nt(rq_t[i], s_hi) + yv[i])
            new.append(state[p] * p_end[c][:, lanes[i]] - (_mm(s_hi, omega[i]) + _mm(s_lo, omega[i])) + psi[i])
        state = new
    for p in range(npair):
        s_ref[p] = state[p]
    y = jnp.concatenate(
        [jnp.concatenate([ystack[c * npair + p][0:CHUNK, :] + ystack[c * npair + p][CHUNK:, :] for p in range(npair)],
                         axis=1) for c in chunks], axis=0)

    mean = _head_mean(y, bdm)
    yc = y - mean
    var = _head_mean(yc * yc, bdm)
    yn = yc * lax.rsqrt(var + RWKV_GN_EPS) * lng_ref[...] + lnb_ref[...]
    bonus = _mm((r * k2 * rk_ref[...]).astype(bf16), bdm) * HEAD_DIM * v
    o_ref[...] = ((yn + bonus) * g).astype(o_ref.dtype)


def _rwkv(zr, mu, w0, a0, k_k, k_a, r_k, ln_g, ln_b, w_up, a_up, g_up, bdm):
    b, length, _ = zr.shape
    assert length % RWKV_ROWS == 0, (length, RWKV_ROWS)
    vec = _const_spec((1, WIDTH))
    return pl.pallas_call(
        _rwkv_kernel,
        grid=(b, length // RWKV_ROWS),
        in_specs=[pl.BlockSpec((None, RWKV_ROWS, RWKV_COLS), lambda i, c: (i, c, 0)),
                  _const_spec((1, RWKV_COLS)), vec, vec, vec, vec, vec, vec, vec,
                  _const_spec((LANES, WIDTH)), _const_spec((LANES, WIDTH)), _const_spec((2 * LANES, WIDTH)),
                  _const_spec((WIDTH, WIDTH))],
        out_specs=pl.BlockSpec((None, RWKV_ROWS, WIDTH), lambda i, c: (i, c, 0)),
        out_shape=jax.ShapeDtypeStruct((b, length, WIDTH), bf16),
        scratch_shapes=[pltpu.VMEM((N_HEADS // 2, PAIR, PAIR), f32), pltpu.VMEM((1, RWKV_COLS), f32)],
        compiler_params=_cparams("parallel", "arbitrary"),
        name="rwkv",
    )(zr, mu, w0, a0, k_k, k_a, r_k, ln_g, ln_b, w_up, a_up, g_up, bdm)


def _merge_kernel(yf_ref, yr_ref, gate_ref, h_ref, wf_ref, wr_ref, wo_ref, o_ref):
    d = h_ref.shape[1]
    pf = _mm(yf_ref[...], wf_ref[...])
    pr = _mm(yr_ref[...], wr_ref[...])
    mixed = gate_ref[:, 0:d].astype(f32) * pf + gate_ref[:, d:2 * d].astype(f32) * pr
    o_ref[...] = h_ref[...] + _mm(mixed.astype(bf16), wo_ref[...])


def _merge(yf, yr, gates, hflat, wf, wr, wo):
    n, d = hflat.shape
    tm = _row_block(n)
    row = lambda c: pl.BlockSpec((tm, c), lambda i: (i, 0))
    return pl.pallas_call(
        _merge_kernel,
        grid=(n // tm,),
        in_specs=[row(WIDTH), row(WIDTH), row(2 * d), row(d),
                  _const_spec((WIDTH, d)), _const_spec((WIDTH, d)), _const_spec((d, d))],
        out_specs=row(d),
        out_shape=jax.ShapeDtypeStruct((n, d), f32),
        compiler_params=_cparams("parallel"),
        name="merge",
    )(yf, yr, gates, hflat, wf, wr, wo)


def _top_rows(s, k):
    n = s.shape[0]
    pos = lax.broadcasted_iota(i32, s.shape, 0).astype(f32)
    vals, outs = [], []
    for _ in range(k):
        m = jnp.max(s, axis=0, keepdims=True)
        first = jnp.min(jnp.where(s == m, pos, float(n)), axis=0, keepdims=True)
        vals.append(m)
        outs.append(first)
        s = jnp.where(pos == first, -jnp.inf, s)
    return jnp.concatenate(vals, axis=0), jnp.concatenate(outs, axis=0)


def _staircase():
    pairs = [(a, b) for a in range(PEER_TOPK) for b in range(PEER_TOPK) if (a + 1) * (b + 1) <= PEER_TOPK]
    rows = -(-len(pairs) // 8) * 8
    sel = jnp.zeros((2, rows, PEER_TOPK), f32)
    r = jnp.arange(len(pairs))
    sel = sel.at[0, r, jnp.array([a for a, _ in pairs])].set(1.0)
    sel = sel.at[1, r, jnp.array([b for _, b in pairs])].set(1.0)
    return sel, len(pairs)


def _peer_sel_kernel(h_ref, g_ref, wq_ref, k1_ref, k2_ref, sel_ref, xn_ref, idx_ref, gate_ref, *, n_cand):
    x = h_ref[...]
    ms = jnp.mean(x * x, axis=-1, keepdims=True)
    xn = x * lax.rsqrt(ms + NORM_EPS) * g_ref[...]
    xn_ref[...] = xn.reshape(xn_ref.shape)
    qt = _nt(wq_ref[...], xn.astype(bf16))
    half = PEER_QDIM // 2
    sel_a, sel_b = sel_ref[0], sel_ref[1]
    pick = lambda sel, t: _mm(sel, t, precision=HIGHEST)
    s1 = jnp.concatenate([_mm(k1_ref[...], qt[h * PEER_QDIM:h * PEER_QDIM + half, :].astype(bf16))
                          for h in range(PEER_HEADS)], axis=1)
    s2 = jnp.concatenate([_mm(k2_ref[...], qt[h * PEER_QDIM + half:(h + 1) * PEER_QDIM, :].astype(bf16))
                          for h in range(PEER_HEADS)], axis=1)
    t1, i1 = _top_rows(s1, PEER_TOPK)
    t2, i2 = _top_rows(s2, PEER_TOPK)
    rows = lax.broadcasted_iota(i32, (sel_a.shape[0], PEER_HEADS * PEER_TOK), 0).astype(f32)
    cand_s = jnp.where(rows < n_cand, pick(sel_a, t1) + pick(sel_b, t2), -jnp.inf)
    cand_i = pick(sel_a, i1) * N_KEYS + pick(sel_b, i2)
    top_s, top_p = _top_rows(cand_s, PEER_TOPK)
    top_i = jnp.concatenate(
        [jnp.max(jnp.where(rows == top_p[j:j + 1, :], cand_i, -1.0), axis=0, keepdims=True)
         for j in range(PEER_TOPK)], axis=0).astype(i32)
    e = jnp.exp(top_s - top_s[0:1, :])
    gate = e / jnp.sum(e, axis=0, keepdims=True)
    by_head = lambda a: jnp.concatenate([a[:, h * PEER_TOK:(h + 1) * PEER_TOK] for h in range(PEER_HEADS)], axis=0)
    idx_ref[...] = by_head(top_i).T * TABLE_ROWS
    gate_ref[...] = by_head(gate).T


def _peer_select(h2, g, wq_t, k1, k2, batch, seq, length):
    d = h2.shape[-1]
    nt = seq // PEER_TOK
    skip = (length - seq) // PEER_TOK
    nblk = batch * nt
    sel, n_cand = _staircase()
    blk = pl.BlockSpec((None, PEER_TOK, PEER_PAIRS), lambda i, j: (i * nt + j, 0, 0))
    return pl.pallas_call(
        functools.partial(_peer_sel_kernel, n_cand=n_cand),
        grid=(batch, nt),
        in_specs=[pl.BlockSpec((None, PEER_TOK, d), lambda i, j: (i, j + skip, 0)),
                  _const_spec((1, d)), _const_spec(wq_t.shape), _const_spec(k1.shape), _const_spec(k2.shape),
                  _const_spec(sel.shape)],
        out_specs=[pl.BlockSpec((PEER_TOK, d // LANES, LANES), lambda i, j: (i * nt + j, 0, 0)), blk, blk],
        out_shape=[jax.ShapeDtypeStruct((nblk * PEER_TOK, d // LANES, LANES), f32),
                   jax.ShapeDtypeStruct((nblk, PEER_TOK, PEER_PAIRS), i32),
                   jax.ShapeDtypeStruct((nblk, PEER_TOK, PEER_PAIRS), f32)],
        compiler_params=_cparams("parallel", "parallel"),
        name="peer_sel",
    )(h2.reshape(batch, length, d), g, wq_t, k1, k2, sel)


def _pack_table(tab):
    e, d = tab.shape
    assert d == 2 * TABLE_ROWS * LANES
    t = lax.bitcast_convert_type(tab.astype(bf16), jnp.uint16).astype(jnp.uint32).reshape(e, 2, TABLE_ROWS, LANES)
    return lax.bitcast_convert_type((t[:, 0] << 16) | t[:, 1], i32).reshape(e * TABLE_ROWS, LANES)


def _unpack(word):
    return (lax.bitcast_convert_type(word & jnp.int32(-65536), f32), lax.bitcast_convert_type(word << 16, f32))


def _expert_rows(idx_ref, tab_ref, t, k):
    return tab_ref[pl.ds(pl.multiple_of(idx_ref[t, k], TABLE_ROWS), TABLE_ROWS), :]


def _gather(idx_ref, tab_ref, stage_ref, t):
    for k in range(PEER_PAIRS):
        stage_ref[TABLE_ROWS * k:TABLE_ROWS * (k + 1), :] = _expert_rows(idx_ref, tab_ref, t, k)


def _staged(stage_ref, s):
    return _unpack(stage_ref[pl.ds(s, PEER_PAIRS, stride=TABLE_ROWS), :])


def _token_pipeline(idx_ref, tab_ref, stage, consume):
    ahead = len(stage)
    for t in range(ahead):
        _gather(idx_ref, tab_ref, stage[t], t)
    for t in range(PEER_TOK):
        consume(t, stage[t % ahead])
        if t + ahead < PEER_TOK:
            _gather(idx_ref, tab_ref, stage[t % ahead], t + ahead)


def _peer_hid_kernel(idx_ref, x_ref, gate_ref, tab_ref, o_ref, *scratch):
    stage, dots_ref = scratch[:-1], scratch[-1]
    ones = jnp.ones((8, LANES), bf16)

    def dots(t, buf):
        xt = x_ref[t]
        acc = jnp.zeros((PEER_PAIRS, LANES), f32)
        for s in range(TABLE_ROWS):
            hi, lo = _staged(buf, s)
            acc = acc + hi * xt[s:s + 1, :] + lo * xt[TABLE_ROWS + s:TABLE_ROWS + s + 1, :]
        a_hi = acc.astype(bf16)
        a_lo = (acc - a_hi.astype(f32)).astype(bf16)
        sums = _nt(ones, a_hi) + _nt(ones, a_lo)
        dots_ref[t:t + 1, :] = sums[0:1, :]

    _token_pipeline(idx_ref, tab_ref, stage, dots)
    hid = dots_ref[...]
    o_ref[...] = gate_ref[...] * (0.5 * hid * (1.0 + lax.erf(hid * (2.0 ** -0.5))))


def _two_term_rows(row):
    hi = row.astype(bf16).astype(f32)
    r = lax.broadcasted_iota(i32, (8, LANES), 0)
    return jnp.where(r == 0, hi, jnp.where(r == 1, row - hi, 0.0)).astype(bf16)


def _peer_out_kernel(idx_ref, c_ref, h_ref, tab_ref, o_ref, *scratch):
    stage, acc_ref = scratch[:-1], scratch[-1]
    row = lax.broadcasted_iota(i32, (8, LANES), 0)
    acc_ref[...] = h_ref[...].reshape(acc_ref.shape)

    def combine(t, buf):
        lhs = _two_term_rows(c_ref[t:t + 1, :])
        out = acc_ref[t]
        for s in range(TABLE_ROWS):
            hi, lo = _staged(buf, s)
            a = _mm(lhs, hi.astype(bf16))
            b = _mm(lhs, lo.astype(bf16))
            out = (out + jnp.where(row == s, a[0:1, :] + a[1:2, :], 0.0)
                   + jnp.where(row == TABLE_ROWS + s, b[0:1, :] + b[1:2, :], 0.0))
        acc_ref[t] = out

    _token_pipeline(idx_ref, tab_ref, stage, combine)
    o_ref[...] = acc_ref[...].reshape(o_ref.shape)


def _smem_blk(index_map):
    return pl.BlockSpec((None, PEER_TOK, PEER_PAIRS), index_map, memory_space=pltpu.SMEM,
                        pipeline_mode=pl.Buffered(1))


def _table_spec(tab):
    return pl.BlockSpec(tab.shape, lambda *_: (0, 0), pipeline_mode=pl.Buffered(1))


def _peer_hidden(idx, x3, gate, utab):
    nblk = idx.shape[0]
    sub = x3.shape[1]
    vblk = pl.BlockSpec((None, PEER_TOK, PEER_PAIRS), lambda i: (i, 0, 0))
    rows = TABLE_ROWS * PEER_PAIRS
    return pl.pallas_call(
        _peer_hid_kernel,
        grid=(nblk,),
        in_specs=[_smem_blk(lambda i: (i, 0, 0)), pl.BlockSpec((PEER_TOK, sub, LANES), lambda i: (i, 0, 0)), vblk,
                  _table_spec(utab)],
        out_specs=vblk,
        out_shape=jax.ShapeDtypeStruct((nblk, PEER_TOK, PEER_PAIRS), f32),
        scratch_shapes=[pltpu.VMEM((rows, LANES), i32)] * PEER_STAGES + [pltpu.VMEM((PEER_TOK, PEER_PAIRS), f32)],
        compiler_params=_cparams("arbitrary"),
        name="peer_hid",
    )(idx, x3, gate, utab)


def _peer_output(idx, coef, h2, vtab, batch, seq, length):
    d = h2.shape[-1]
    sub = d // LANES
    nt = seq // PEER_TOK
    skip = (length - seq) // PEER_TOK
    rows = TABLE_ROWS * PEER_PAIRS
    blk = lambda i, j: (i * nt + j, 0, 0)
    return pl.pallas_call(
        _peer_out_kernel,
        grid=(batch, nt),
        in_specs=[_smem_blk(blk), pl.BlockSpec((None, PEER_TOK, PEER_PAIRS), blk),
                  pl.BlockSpec((None, PEER_TOK, d), lambda i, j: (i, j + skip, 0)),
                  _table_spec(vtab)],
        out_specs=pl.BlockSpec((None, PEER_TOK, d), lambda i, j: (i, j, 0)),
        out_shape=jax.ShapeDtypeStruct((batch, seq, d), f32),
        scratch_shapes=[pltpu.VMEM((rows, LANES), i32)] * PEER_STAGES + [pltpu.VMEM((PEER_TOK, sub, LANES), f32)],
        compiler_params=_cparams("arbitrary", "arbitrary"),
        name="peer_out",
    )(idx, coef, h2.reshape(batch, length, d), vtab)


def _pad_cols(w, width):
    return jnp.pad(w, ((0, 0), (0, width - w.shape[1])))


def _pad_rows(w, height):
    return jnp.pad(w, ((0, height - w.shape[0]), (0, 0)))


def kernel(x, meta_tokens, norm1_g, w_in, fox_q_norm, fox_k_norm, fox_f_bias, rwkv_mu, rwkv_w0, rwkv_w_up, rwkv_a0, rwkv_a_up, rwkv_g_up, rwkv_k_k, rwkv_k_a, rwkv_r_k, rwkv_ln_g, rwkv_ln_b, w_branch_fox, w_branch_rwkv, w_out, norm2_g, peer_w_q, peer_sub_k1, peer_sub_k2, peer_u, peer_v):
    batch, seq, d = x.shape
    assert w_in.shape[0] == 1, "one layer"
    assert seq % PEER_TOK == 0 and d % LANES == 0
    t_real = seq + N_META
    pad = (-t_real) % ATTN_BLOCK
    length = t_real + pad
    ta = _attn_block(length)

    meta = jnp.broadcast_to(meta_tokens[None].astype(x.dtype), (batch, N_META, d))
    hpad = jnp.concatenate([jnp.zeros((batch, pad, d), x.dtype), meta, x], axis=1).reshape(batch * length, d)
    w = w_in[0]
    fox_w = 3 * WIDTH + N_HEADS
    o = fox_w
    lora = lambda lo, n, width: _pad_cols(w[:, lo:lo + n], width)
    w_all = jnp.concatenate([
        w[:, 0:3 * WIDTH], _pad_cols(w[:, 3 * WIDTH:fox_w], LANES),
        w[:, o:o + 3 * WIDTH],
        lora(o + 3 * WIDTH, DECAY_LORA, LANES),
        lora(o + 3 * WIDTH + DECAY_LORA, AAA_LORA, LANES),
        lora(o + 3 * WIDTH + DECAY_LORA + AAA_LORA, GATE_LORA, 2 * LANES),
        w[:, o + 3 * WIDTH + DECAY_LORA + AAA_LORA + GATE_LORA:],
    ], axis=1).astype(bf16)
    mu = rwkv_mu[0]
    mu_all = jnp.concatenate([
        mu[0:3 * WIDTH], jnp.pad(mu[3 * WIDTH:3 * WIDTH + DECAY_LORA], (0, LANES - DECAY_LORA)),
        jnp.pad(mu[3 * WIDTH + DECAY_LORA:3 * WIDTH + DECAY_LORA + AAA_LORA], (0, LANES - AAA_LORA)),
        jnp.pad(mu[3 * WIDTH + DECAY_LORA + AAA_LORA:], (0, 2 * LANES - GATE_LORA)),
    ])[None]
    head_of = jnp.arange(WIDTH) // HEAD_DIM
    bdm = (head_of[:, None] == head_of[None, :]).astype(f32) / HEAD_DIM
    vec = lambda p: p[0].reshape(1, -1)

    q, k, v, lf, zr, gates = _proj(hpad, vec(norm1_g), w_all, vec(fox_q_norm), vec(fox_k_norm),
                                   jnp.pad(fox_f_bias[0], (0, LANES - N_HEADS))[None], bdm.astype(bf16))
    three = lambda a: a.reshape(batch, length, a.shape[-1])
    y_fox = _attention(three(q), three(k), _key_bias(three(lf), pad), three(v), ta)
    y_rwkv = _rwkv(three(zr), mu_all, vec(rwkv_w0), vec(rwkv_a0), vec(rwkv_k_k), vec(rwkv_k_a), vec(rwkv_r_k),
                   vec(rwkv_ln_g), vec(rwkv_ln_b),
                   _pad_rows(rwkv_w_up[0], LANES).astype(bf16), _pad_rows(rwkv_a_up[0], LANES).astype(bf16),
                   _pad_rows(rwkv_g_up[0], 2 * LANES).astype(bf16), bdm.astype(bf16))
    h2 = _merge(y_fox.reshape(-1, WIDTH), y_rwkv.reshape(-1, WIDTH), gates, hpad,
                w_branch_fox[0].astype(bf16), w_branch_rwkv[0].astype(bf16), w_out[0].astype(bf16))

    xn2, idx, gate = _peer_select(h2, vec(norm2_g), peer_w_q[0].T.astype(bf16),
                                  peer_sub_k1[0].astype(bf16), peer_sub_k2[0].astype(bf16), batch, seq, length)
    coef = _peer_hidden(idx, xn2, gate, _pack_table(peer_u[0]))
    out = _peer_output(idx, coef, h2, _pack_table(peer_v[0]), batch, seq, length)
    return out.reshape(batch, seq, d)
```

```python
import functools

import jax
import jax.numpy as jnp
from jax import lax
from jax.experimental import pallas as pl
from jax.experimental.pallas import tpu as pltpu

f32 = jnp.float32
bf16 = jnp.bfloat16
i32 = jnp.int32
HIGHEST = lax.Precision.HIGHEST

N_META = 16
HEAD_DIM = 64
N_HEADS = 8
WIDTH = N_HEADS * HEAD_DIM
DECAY_LORA, AAA_LORA, GATE_LORA = 64, 64, 160
ATTN_BLOCK = 128
PEER_HEADS, PEER_QDIM, N_KEYS, PEER_TOPK = 8, 256, 128, 16
NORM_EPS = 1e-6
RWKV_GN_EPS = 64e-5
MASK_VALUE = -1e30
LOG2_E = 1.4426950408889634

LANES = 128
VMEM_LIMIT_BYTES = 56 * 1024 * 1024

CHUNK = 64
RWKV_ROWS = 6 * CHUNK
PAIR = 2 * HEAD_DIM
PEER_TOK = 128
PEER_PAIRS = PEER_HEADS * PEER_TOPK
PEER_STAGES = 2
TABLE_ROWS = 4
FOX_COLS = 3 * WIDTH + LANES
RWKV_COLS = 3 * WIDTH + 2 * LANES + 2 * LANES


def _row_block(n):
    for t in (512, 384, 256, 128):
        if n % t == 0:
            return t
    raise ValueError(f"row count {n} is not a multiple of 128")


def _attn_block(length):
    return 384 if length % 384 == 0 else ATTN_BLOCK


def _cparams(*sem):
    return pltpu.CompilerParams(dimension_semantics=sem, vmem_limit_bytes=VMEM_LIMIT_BYTES)


def _const_spec(shape):
    nd = len(shape)
    return pl.BlockSpec(shape, lambda *_: (0,) * nd)


def _nt(a, b, **kw):
    return lax.dot_general(a, b, (((1,), (1,)), ((), ())), preferred_element_type=f32, **kw)


def _tn(a, b, **kw):
    return lax.dot_general(a, b, (((0,), (0,)), ((), ())), preferred_element_type=f32, **kw)


def _mm(a, b, **kw):
    return jnp.dot(a, b, preferred_element_type=f32, **kw)


def _softplus(y):
    return jnp.maximum(y, 0.0) + jnp.log1p(jnp.exp(-jnp.abs(y)))


def _sigmoid(y):
    return 1.0 / (1.0 + jnp.exp(-y))


def _three_term_mm(a, x):
    hi = x.astype(bf16)
    r1 = x - hi.astype(f32)
    mid = r1.astype(bf16)
    lo = (r1 - mid.astype(f32)).astype(bf16)
    return _mm(a, hi) + _mm(a, mid) + _mm(a, lo)


def _head_mean(x, bd):
    hi = x.astype(bf16)
    lo = (x - hi.astype(f32)).astype(bf16)
    return _mm(hi, bd) + _mm(lo, bd)


def _proj_kernel(x_ref, g_ref, w_ref, qn_ref, kn_ref, fb_ref, bd_ref,
                 q_ref, k_ref, v_ref, lf_ref, zr_ref, gate_ref):
    x = x_ref[...]
    ms = jnp.mean(x * x, axis=-1, keepdims=True)
    xn = (x * lax.rsqrt(ms + NORM_EPS) * g_ref[...]).astype(bf16)

    def head_norm(z, gain):
        msq = _mm((z * z).astype(bf16), bd_ref[...])
        return z * lax.rsqrt(msq + NORM_EPS) * gain

    zq = _mm(xn, w_ref[:, 0:WIDTH])
    q_ref[...] = (head_norm(zq, qn_ref[...]) * (HEAD_DIM ** -0.5 * LOG2_E)).astype(bf16)
    zk = _mm(xn, w_ref[:, WIDTH:2 * WIDTH])
    k_ref[...] = head_norm(zk, kn_ref[...]).astype(bf16)
    v_ref[...] = _mm(xn, w_ref[:, 2 * WIDTH:3 * WIDTH]).astype(bf16)
    zf = _mm(xn, w_ref[:, 3 * WIDTH:FOX_COLS]) + fb_ref[...]
    lf_ref[...] = -_softplus(-zf) * LOG2_E
    zr_ref[...] = _mm(xn, w_ref[:, FOX_COLS:FOX_COLS + RWKV_COLS])
    zg = _mm(xn, w_ref[:, FOX_COLS + RWKV_COLS:])
    gate_ref[...] = _sigmoid(zg).astype(bf16)


def _proj(hflat, g, w, qn, kn, fb, bd):
    n, d = hflat.shape
    tm = _row_block(n)
    ncol = w.shape[1]
    ngate = ncol - FOX_COLS - RWKV_COLS
    row = lambda c: pl.BlockSpec((tm, c), lambda i: (i, 0))
    return pl.pallas_call(
        _proj_kernel,
        grid=(n // tm,),
        in_specs=[row(d), _const_spec((1, d)), _const_spec((d, ncol)), _const_spec((1, WIDTH)),
                  _const_spec((1, WIDTH)), _const_spec((1, LANES)), _const_spec((WIDTH, WIDTH))],
        out_specs=[row(WIDTH), row(WIDTH), row(WIDTH), row(LANES), row(RWKV_COLS), row(ngate)],
        out_shape=[jax.ShapeDtypeStruct((n, WIDTH), bf16)] * 3
        + [jax.ShapeDtypeStruct((n, LANES), f32), jax.ShapeDtypeStruct((n, RWKV_COLS), f32),
           jax.ShapeDtypeStruct((n, ngate), bf16)],
        compiler_params=_cparams("parallel"),
        name="proj",
    )(hflat, g, w, qn, kn, fb, bd)


def _cumsum_kernel(lf_ref, o_ref, *, pad):
    length = lf_ref.shape[0]
    r = lax.broadcasted_iota(i32, (LANES, LANES), 0)
    c = lax.broadcasted_iota(i32, (LANES, LANES), 1)
    tri = (r >= c).astype(f32)
    carry = jnp.zeros((1, LANES), f32)
    for blk in range(length // LANES):
        cs = _mm(tri, lf_ref[blk * LANES:(blk + 1) * LANES, :], precision=HIGHEST) + carry
        carry = cs[LANES - 1:LANES, :]
        neg = jnp.where(r + blk * LANES >= pad, -cs, MASK_VALUE)
        for h in range(N_HEADS):
            o_ref[h, blk * LANES:(blk + 1) * LANES, :] = jnp.broadcast_to(neg[:, h:h + 1], (LANES, LANES))


def _key_bias(lf, pad):
    b, length, _ = lf.shape
    return pl.pallas_call(
        functools.partial(_cumsum_kernel, pad=pad),
        grid=(b,),
        in_specs=[pl.BlockSpec((None, length, LANES), lambda i: (i, 0, 0))],
        out_specs=pl.BlockSpec((None, N_HEADS, length, LANES), lambda i: (i, 0, 0, 0)),
        out_shape=jax.ShapeDtypeStruct((b, N_HEADS, length, LANES), f32),
        compiler_params=_cparams("parallel"),
        name="cumsum",
    )(lf)


def _attn_kernel(q_ref, k_ref, kb_ref, v_ref, o_ref, *, ta):
    qi = pl.program_id(2)
    lane = lax.broadcasted_iota(i32, (1, PAIR), 1)
    key = lax.broadcasted_iota(i32, (ta, ATTN_BLOCK), 0)
    qry = lax.broadcasted_iota(i32, (ta, ATTN_BLOCK), 1)
    nrow = ta // ATTN_BLOCK
    pcs = [(h, r) for h in range(2) for r in range(nrow)]
    qa = []
    for h, r in pcs:
        qr = q_ref[r * ATTN_BLOCK:(r + 1) * ATTN_BLOCK, :]
        qa.append(jnp.where((lane // HEAD_DIM) == h, qr, jnp.zeros_like(qr)))

    def block(j, carries, diagonal):
        start = pl.multiple_of(j * ta, ta)
        ks = k_ref[pl.ds(start, ta), :]
        vt = v_ref[pl.ds(start, ta), :].T
        bias = [kb_ref[h, pl.ds(start, ta), :] for h in range(2)]
        ids = range(len(pcs))
        s = [_nt(ks, qa[i]) + bias[pcs[i][0]] for i in ids]
        if diagonal:
            s = [jnp.where(key <= qry + r * ATTN_BLOCK, s[i], MASK_VALUE) for i, (h, r) in enumerate(pcs)]
        m_new = [jnp.maximum(carries[i][0], jnp.max(s[i], axis=0, keepdims=True)) for i in ids]
        p = [jnp.exp2(s[i] - m_new[i]) for i in ids]
        alpha = [jnp.exp2(carries[i][0] - m_new[i]) for i in ids]
        l = [alpha[i] * carries[i][1] + jnp.sum(p[i], axis=0, keepdims=True) for i in ids]
        pv = [_mm(vt, p[i].astype(bf16)) for i in ids]
        acc = [alpha[i] * carries[i][2] + pv[i] for i in ids]
        return tuple((m_new[i], l[i], acc[i]) for i in ids)

    init = (jnp.full((1, ATTN_BLOCK), MASK_VALUE, f32), jnp.zeros((1, ATTN_BLOCK), f32),
            jnp.zeros((PAIR, ATTN_BLOCK), f32))
    carries = lax.fori_loop(0, qi // 2, lambda j, c: block(2 * j + 1, block(2 * j, c, False), False),
                            (init,) * len(pcs))
    carries = lax.cond(qi % 2 == 1, lambda c: block(qi - 1, c, False), lambda c: c, carries)
    final = block(qi, carries, True)
    vrow = lax.broadcasted_iota(i32, (PAIR, 1), 0)
    for r in range(nrow):
        o0 = final[r][2] / final[r][1]
        o1 = final[nrow + r][2] / final[nrow + r][1]
        o_ref[r * ATTN_BLOCK:(r + 1) * ATTN_BLOCK, :] = jnp.where(vrow < HEAD_DIM, o0, o1).T.astype(o_ref.dtype)


def _attention(q, k, kbias, v, ta):
    b, length, _ = q.shape
    nb = length // ta
    npair = N_HEADS // 2
    full = pl.BlockSpec((None, length, PAIR), lambda i, p, j: (i, 0, p))
    return pl.pallas_call(
        functools.partial(_attn_kernel, ta=ta),
        grid=(b, npair, nb),
        in_specs=[pl.BlockSpec((None, ta, PAIR), lambda i, p, j: (i, j, p)), full,
                  pl.BlockSpec((None, 2, length, LANES), lambda i, p, j: (i, p, 0, 0)), full],
        out_specs=pl.BlockSpec((None, ta, PAIR), lambda i, p, j: (i, j, p)),
        out_shape=jax.ShapeDtypeStruct((b, length, WIDTH), bf16),
        compiler_params=_cparams("parallel", "parallel", "arbitrary"),
        name="attn",
    )(q, k, kbias, v)


def _stack(x, lane_head):
    zero = jnp.zeros_like(x)
    return jnp.concatenate([jnp.where(lane_head == 0, x, zero), jnp.where(lane_head == 1, x, zero)], axis=0)


def _unit_lower_inverse(ns, r, c):
    eye = (r == c).astype(f32)
    first = ((r ^ c) == 1) & ((r & 1) == 1)
    ts = [(eye - jnp.where(first, n, 0.0)).astype(bf16) for n in ns]
    s = 2
    while s < CHUNK:
        sel = ((r // (2 * s)) == (c // (2 * s))) & ((r & s) != 0) & ((c & s) == 0)
        low = [jnp.where(sel, n, 0.0).astype(bf16) for n in ns]
        tn = [_mm(t, lo).astype(bf16) for t, lo in zip(ts, low)]
        ts = [(t.astype(f32) - _mm(x, t)).astype(bf16) for t, x in zip(ts, tn)]
        s *= 2
    return ts


def _rwkv_kernel(z_ref, mu_ref, w0_ref, a0_ref, kk_ref, ka_ref, rk_ref, lng_ref, lnb_ref,
                 wup_ref, aup_ref, gup_ref, bdm_ref, o_ref, s_ref, prev_ref):
    ci = pl.program_id(1)

    @pl.when(ci == 0)
    def _():
        s_ref[...] = jnp.zeros_like(s_ref)
        prev_ref[...] = jnp.zeros_like(prev_ref)

    z = z_ref[...]
    rows = lax.broadcasted_iota(i32, (RWKV_ROWS, 1), 0)
    zprev = jnp.where(rows == 0, prev_ref[...], pltpu.roll(z, 1, 0))
    prev_ref[...] = z[RWKV_ROWS - 1:RWKV_ROWS, :]
    zs = z + mu_ref[...] * (zprev - z)
    r = zs[:, 0:WIDTH]
    k = zs[:, WIDTH:2 * WIDTH]
    v = zs[:, 2 * WIDTH:3 * WIDTH]
    o1 = 3 * WIDTH
    wd = zs[:, o1:o1 + LANES]
    ad = zs[:, o1 + LANES:o1 + 2 * LANES]
    gd = zs[:, o1 + 2 * LANES:o1 + 4 * LANES]

    w = -_softplus(-(w0_ref[...] + _mm(jnp.tanh(wd).astype(bf16), wup_ref[...]))) - 0.5
    ld = -jnp.exp(w)
    a = _sigmoid(a0_ref[...] + _mm(ad.astype(bf16), aup_ref[...]))
    g = _mm(_sigmoid(gd).astype(bf16), gup_ref[...])
    bdm = bdm_ref[...]
    kk = k * kk_ref[...]
    nrm = jnp.sqrt(_mm((kk * kk).astype(bf16), bdm) * HEAD_DIM)
    kk = kk / jnp.maximum(nrm, 1e-12)
    k2 = k * (1.0 + (a - 1.0) * ka_ref[...])
    b = kk * a

    tr = lax.broadcasted_iota(i32, (RWKV_ROWS, RWKV_ROWS), 0)
    tc = lax.broadcasted_iota(i32, (RWKV_ROWS, RWKV_ROWS), 1)
    tri = ((tr >= tc) & ((tr // CHUNK) == (tc // CHUNK))).astype(bf16)
    lc = _three_term_mm(tri, ld)
    lp = lc - ld
    chunks = range(RWKV_ROWS // CHUNK)
    mids = [lc[c * CHUNK + CHUNK // 2 - 1:c * CHUNK + CHUNK // 2, :] for c in chunks]
    tots = [lc[(c + 1) * CHUNK - 1:(c + 1) * CHUNK, :] for c in chunks]
    mid, tot = mids[-1], tots[-1]
    for c in reversed(chunks[:-1]):
        mid = jnp.where(rows < (c + 1) * CHUNK, mids[c], mid)
        tot = jnp.where(rows < (c + 1) * CHUNK, tots[c], tot)
    kq = kk * jnp.exp(lp - mid)
    rr = r * jnp.exp(lc - mid)
    e_after = jnp.exp(mid - lc)
    kh = k2 * e_after
    bh = b * e_after
    e_end = jnp.exp(tot - lc)
    kc = k2 * e_end
    bc = b * e_end
    e_mid = [jnp.exp(m) for m in mids]
    p_end = [jnp.exp(t) for t in tots]

    sr = lax.broadcasted_iota(i32, (PAIR, PAIR), 0)
    sc = lax.broadcasted_iota(i32, (PAIR, PAIR), 1)
    same = (sr // CHUNK) == (sc // CHUNK)
    strict = same & ((sr % CHUNK) > (sc % CHUNK))
    incl = same & ((sr % CHUNK) >= (sc % CHUNK))
    lane_head = lax.broadcasted_iota(i32, (1, PAIR), 1) // HEAD_DIM

    npair = N_HEADS // 2
    units = [(c, p) for c in chunks for p in range(npair)]
    ids = range(len(units))
    lanes = [slice(p * PAIR, (p + 1) * PAIR) for _, p in units]
    st = lambda x: [_stack(x[c * CHUNK:(c + 1) * CHUNK, p * PAIR:(p + 1) * PAIR], lane_head) for c, p in units]
    kq_s, rr_s, kh_s, bh_s, v_s, kc_s, bc_s = (st(x) for x in (kq, rr, kh, bh, v, kc, bc))
    amat = [_nt(jnp.concatenate([kq_s[i], rr_s[i]], axis=0).astype(bf16),
                jnp.concatenate([kh_s[i], bh_s[i]], axis=0).astype(bf16)) for i in ids]
    a_kk = [jnp.where(strict, amat[i][0:PAIR, 0:PAIR], 0.0).astype(bf16) for i in ids]
    a_kb = [jnp.where(strict, amat[i][0:PAIR, PAIR:], 0.0) for i in ids]
    a_rk = [jnp.where(incl, amat[i][PAIR:, 0:PAIR], 0.0).astype(bf16) for i in ids]
    a_rb = [jnp.where(incl, amat[i][PAIR:, PAIR:], 0.0).astype(bf16) for i in ids]
    t = _unit_lower_inverse(a_kb, sr, sc)
    vb = [v_s[i].astype(bf16) for i in ids]
    av = [_mm(a_kk[i], vb[i]).astype(bf16) for i in ids]
    wm = [_mm(t[i], kq_s[i].astype(bf16)) for i in ids]
    uv = [_mm(t[i], av[i]) for i in ids]
    rq = [rr_s[i] - _mm(a_rb[i], wm[i].astype(bf16)) for i in ids]
    yv = [_mm(a_rk[i], vb[i]) - _mm(a_rb[i], uv[i].astype(bf16)) for i in ids]
    wm_t = [(wm[i] * e_mid[units[i][0]][:, lanes[i]]).astype(bf16) for i in ids]
    rq_t = [(rq[i] * e_mid[units[i][0]][:, lanes[i]]).astype(bf16) for i in ids]
    bcb = [bc_s[i].astype(bf16) for i in ids]
    omega = [_tn(wm_t[i], bcb[i]).astype(bf16) for i in ids]
    psi = [_tn(vb[i], kc_s[i].astype(bf16)) - _tn(uv[i].astype(bf16), bcb[i]) for i in ids]
    state = [s_ref[p] for p in range(npair)]
    ystack = []
    for c in chunks:
        new = []
        for p in range(npair):
            i = c * npair + p
            s_hi = state[p].astype(bf16)
            s_lo = (state[p] - s_hi.astype(f32)).astype(bf16)
            ystack.append(_nt(rq_t[i], s_hi) + yv[i])
            new.append(state[p] * p_end[c][:, lanes[i]] - (_mm(s_hi, omega[i]) + _mm(s_lo, omega[i])) + psi[i])
        state = new
    for p in range(npair):
        s_ref[p] = state[p]
    y = jnp.concatenate(
        [jnp.concatenate([ystack[c * npair + p][0:CHUNK, :] + ystack[c * npair + p][CHUNK:, :] for p in range(npair)],
                         axis=1) for c in chunks], axis=0)

    mean = _head_mean(y, bdm)
    yc = y - mean
    var = _head_mean(yc * yc, bdm)
    yn = yc * lax.rsqrt(var + RWKV_GN_EPS) * lng_ref[...] + lnb_ref[...]
    bonus = _mm((r * k2 * rk_ref[...]).astype(bf16), bdm) * HEAD_DIM * v
    o_ref[...] = ((yn + bonus) * g).astype(o_ref.dtype)


def _rwkv(zr, mu, w0, a0, k_k, k_a, r_k, ln_g, ln_b, w_up, a_up, g_up, bdm):
    b, length, _ = zr.shape
    assert length % RWKV_ROWS == 0, (length, RWKV_ROWS)
    vec = _const_spec((1, WIDTH))
    return pl.pallas_call(
        _rwkv_kernel,
        grid=(b, length // RWKV_ROWS),
        in_specs=[pl.BlockSpec((None, RWKV_ROWS, RWKV_COLS), lambda i, c: (i, c, 0)),
                  _const_spec((1, RWKV_COLS)), vec, vec, vec, vec, vec, vec, vec,
                  _const_spec((LANES, WIDTH)), _const_spec((LANES, WIDTH)), _const_spec((2 * LANES, WIDTH)),
                  _const_spec((WIDTH, WIDTH))],
        out_specs=pl.BlockSpec((None, RWKV_ROWS, WIDTH), lambda i, c: (i, c, 0)),
        out_shape=jax.ShapeDtypeStruct((b, length, WIDTH), bf16),
        scratch_shapes=[pltpu.VMEM((N_HEADS // 2, PAIR, PAIR), f32), pltpu.VMEM((1, RWKV_COLS), f32)],
        compiler_params=_cparams("parallel", "arbitrary"),
        name="rwkv",
    )(zr, mu, w0, a0, k_k, k_a, r_k, ln_g, ln_b, w_up, a_up, g_up, bdm)


def _merge_kernel(yf_ref, yr_ref, gate_ref, h_ref, wf_ref, wr_ref, wo_ref, o_ref):
    d = h_ref.shape[1]
    pf = _mm(yf_ref[...], wf_ref[...])
    pr = _mm(yr_ref[...], wr_ref[...])
    mixed = gate_ref[:, 0:d].astype(f32) * pf + gate_ref[:, d:2 * d].astype(f32) * pr
    o_ref[...] = h_ref[...] + _mm(mixed.astype(bf16), wo_ref[...])


def _merge(yf, yr, gates, hflat, wf, wr, wo):
    n, d = hflat.shape
    tm = _row_block(n)
    row = lambda c: pl.BlockSpec((tm, c), lambda i: (i, 0))
    return pl.pallas_call(
        _merge_kernel,
        grid=(n // tm,),
        in_specs=[row(WIDTH), row(WIDTH), row(2 * d), row(d),
                  _const_spec((WIDTH, d)), _const_spec((WIDTH, d)), _const_spec((d, d))],
        out_specs=row(d),
        out_shape=jax.ShapeDtypeStruct((n, d), f32),
        compiler_params=_cparams("parallel"),
        name="merge",
    )(yf, yr, gates, hflat, wf, wr, wo)


def _top_rows(s, k):
    n = s.shape[0]
    pos = lax.broadcasted_iota(i32, s.shape, 0).astype(f32)
    vals, outs = [], []
    for _ in range(k):
        m = jnp.max(s, axis=0, keepdims=True)
        first = jnp.min(jnp.where(s == m, pos, float(n)), axis=0, keepdims=True)
        vals.append(m)
        outs.append(first)
        s = jnp.where(pos == first, -jnp.inf, s)
    return jnp.concatenate(vals, axis=0), jnp.concatenate(outs, axis=0)


def _staircase():
    pairs = [(a, b) for a in range(PEER_TOPK) for b in range(PEER_TOPK) if (a + 1) * (b + 1) <= PEER_TOPK]
    rows = -(-len(pairs) // 8) * 8
    sel = jnp.zeros((2, rows, PEER_TOPK), f32)
    r = jnp.arange(len(pairs))
    sel = sel.at[0, r, jnp.array([a for a, _ in pairs])].set(1.0)
    sel = sel.at[1, r, jnp.array([b for _, b in pairs])].set(1.0)
    return sel, len(pairs)


def _peer_sel_kernel(h_ref, g_ref, wq_ref, k1_ref, k2_ref, sel_ref, xn_ref, idx_ref, gate_ref, *, n_cand):
    x = h_ref[...]
    ms = jnp.mean(x * x, axis=-1, keepdims=True)
    xn = x * lax.rsqrt(ms + NORM_EPS) * g_ref[...]
    xn_ref[...] = xn.reshape(xn_ref.shape)
    qt = _nt(wq_ref[...], xn.astype(bf16))
    half = PEER_QDIM // 2
    sel_a, sel_b = sel_ref[0], sel_ref[1]
    pick = lambda sel, t: _mm(sel, t, precision=HIGHEST)
    s1 = jnp.concatenate([_mm(k1_ref[...], qt[h * PEER_QDIM:h * PEER_QDIM + half, :].astype(bf16))
                          for h in range(PEER_HEADS)], axis=1)
    s2 = jnp.concatenate([_mm(k2_ref[...], qt[h * PEER_QDIM + half:(h + 1) * PEER_QDIM, :].astype(bf16))
                          for h in range(PEER_HEADS)], axis=1)
    t1, i1 = _top_rows(s1, PEER_TOPK)
    t2, i2 = _top_rows(s2, PEER_TOPK)
    rows = lax.broadcasted_iota(i32, (sel_a.shape[0], PEER_HEADS * PEER_TOK), 0).astype(f32)
    cand_s = jnp.where(rows < n_cand, pick(sel_a, t1) + pick(sel_b, t2), -jnp.inf)
    cand_i = pick(sel_a, i1) * N_KEYS + pick(sel_b, i2)
    top_s, top_p = _top_rows(cand_s, PEER_TOPK)
    top_i = jnp.concatenate(
        [jnp.max(jnp.where(rows == top_p[j:j + 1, :], cand_i, -1.0), axis=0, keepdims=True)
         for j in range(PEER_TOPK)], axis=0).astype(i32)
    e = jnp.exp(top_s - top_s[0:1, :])
    gate = e / jnp.sum(e, axis=0, keepdims=True)
    by_head = lambda a: jnp.concatenate([a[:, h * PEER_TOK:(h + 1) * PEER_TOK] for h in range(PEER_HEADS)], axis=0)
    idx_ref[...] = by_head(top_i).T * TABLE_ROWS
    gate_ref[...] = by_head(gate).T


def _peer_select(h2, g, wq_t, k1, k2, batch, seq, length):
    d = h2.shape[-1]
    nt = seq // PEER_TOK
    skip = (length - seq) // PEER_TOK
    nblk = batch * nt
    sel, n_cand = _staircase()
    blk = pl.BlockSpec((None, PEER_TOK, PEER_PAIRS), lambda i, j: (i * nt + j, 0, 0))
    return pl.pallas_call(
        functools.partial(_peer_sel_kernel, n_cand=n_cand),
        grid=(batch, nt),
        in_specs=[pl.BlockSpec((None, PEER_TOK, d), lambda i, j: (i, j + skip, 0)),
                  _const_spec((1, d)), _const_spec(wq_t.shape), _const_spec(k1.shape), _const_spec(k2.shape),
                  _const_spec(sel.shape)],
        out_specs=[pl.BlockSpec((PEER_TOK, d // LANES, LANES), lambda i, j: (i * nt + j, 0, 0)), blk, blk],
        out_shape=[jax.ShapeDtypeStruct((nblk * PEER_TOK, d // LANES, LANES), f32),
                   jax.ShapeDtypeStruct((nblk, PEER_TOK, PEER_PAIRS), i32),
                   jax.ShapeDtypeStruct((nblk, PEER_TOK, PEER_PAIRS), f32)],
        compiler_params=_cparams("parallel", "parallel"),
        name="peer_sel",
    )(h2.reshape(batch, length, d), g, wq_t, k1, k2, sel)


def _pack_table(tab):
    e, d = tab.shape
    assert d == 2 * TABLE_ROWS * LANES
    tm = _row_block(e)
    return pl.pallas_call(
        _pack_kernel,
        grid=(e // tm,),
        in_specs=[pl.BlockSpec((tm, d), lambda i: (i, 0))],
        out_specs=pl.BlockSpec((tm * TABLE_ROWS, LANES), lambda i: (i, 0)),
        out_shape=jax.ShapeDtypeStruct((e * TABLE_ROWS, LANES), i32),
        compiler_params=_cparams("parallel"),
        name="pack",
    )(tab)


def _pack_kernel(x_ref, o_ref):
    half = x_ref.shape[1] // 2
    bits = lax.bitcast_convert_type(x_ref[...].astype(bf16).astype(f32), i32)
    words = bits[:, 0:half] | lax.shift_right_logical(bits[:, half:], 16)
    o_ref[...] = words.reshape(o_ref.shape)


def _unpack(word):
    return (lax.bitcast_convert_type(word & jnp.int32(-65536), f32), lax.bitcast_convert_type(word << 16, f32))


def _expert_rows(idx_ref, tab_ref, t, k):
    return tab_ref[pl.ds(pl.multiple_of(idx_ref[t, k], TABLE_ROWS), TABLE_ROWS), :]


def _gather(idx_ref, tab_ref, stage_ref, t):
    for k in range(PEER_PAIRS):
        stage_ref[TABLE_ROWS * k:TABLE_ROWS * (k + 1), :] = _expert_rows(idx_ref, tab_ref, t, k)


def _staged(stage_ref, s):
    return _unpack(stage_ref[pl.ds(s, PEER_PAIRS, stride=TABLE_ROWS), :])


def _token_pipeline(idx_ref, tab_ref, stage, consume):
    ahead = len(stage)
    for t in range(ahead):
        _gather(idx_ref, tab_ref, stage[t], t)
    for t in range(PEER_TOK):
        consume(t, stage[t % ahead])
        if t + ahead < PEER_TOK:
            _gather(idx_ref, tab_ref, stage[t % ahead], t + ahead)


def _peer_hid_kernel(idx_ref, x_ref, gate_ref, tab_ref, o_ref, *scratch):
    stage, dots_ref = scratch[:-1], scratch[-1]
    ones = jnp.ones((8, LANES), bf16)

    def dots(t, buf):
        xt = x_ref[t]
        acc = jnp.zeros((PEER_PAIRS, LANES), f32)
        for s in range(TABLE_ROWS):
            hi, lo = _staged(buf, s)
            acc = acc + hi * xt[s:s + 1, :] + lo * xt[TABLE_ROWS + s:TABLE_ROWS + s + 1, :]
        a_hi = acc.astype(bf16)
        a_lo = (acc - a_hi.astype(f32)).astype(bf16)
        sums = _nt(ones, a_hi) + _nt(ones, a_lo)
        dots_ref[t:t + 1, :] = sums[0:1, :]

    _token_pipeline(idx_ref, tab_ref, stage, dots)
    hid = dots_ref[...]
    o_ref[...] = gate_ref[...] * (0.5 * hid * (1.0 + lax.erf(hid * (2.0 ** -0.5))))


def _two_term_rows(row):
    hi = row.astype(bf16).astype(f32)
    r = lax.broadcasted_iota(i32, (8, LANES), 0)
    return jnp.where(r == 0, hi, jnp.where(r == 1, row - hi, 0.0)).astype(bf16)


def _peer_out_kernel(idx_ref, c_ref, h_ref, tab_ref, o_ref, *scratch):
    stage, acc_ref = scratch[:-1], scratch[-1]
    row = lax.broadcasted_iota(i32, (8, LANES), 0)
    acc_ref[...] = h_ref[...].reshape(acc_ref.shape)

    def combine(t, buf):
        lhs = _two_term_rows(c_ref[t:t + 1, :])
        out = acc_ref[t]
        for s in range(TABLE_ROWS):
            hi, lo = _staged(buf, s)
            a = _mm(lhs, hi.astype(bf16))
            b = _mm(lhs, lo.astype(bf16))
            out = (out + jnp.where(row == s, a[0:1, :] + a[1:2, :], 0.0)
                   + jnp.where(row == TABLE_ROWS + s, b[0:1, :] + b[1:2, :], 0.0))
        acc_ref[t] = out

    _token_pipeline(idx_ref, tab_ref, stage, combine)
    o_ref[...] = acc_ref[...].reshape(o_ref.shape)


def _smem_blk(index_map):
    return pl.BlockSpec((None, PEER_TOK, PEER_PAIRS), index_map, memory_space=pltpu.SMEM,
                        pipeline_mode=pl.Buffered(1))


def _table_spec(tab):
    return pl.BlockSpec(tab.shape, lambda *_: (0, 0), pipeline_mode=pl.Buffered(1))


def _peer_hidden(idx, x3, gate, utab):
    nblk = idx.shape[0]
    sub = x3.shape[1]
    vblk = pl.BlockSpec((None, PEER_TOK, PEER_PAIRS), lambda i: (i, 0, 0))
    rows = TABLE_ROWS * PEER_PAIRS
    return pl.pallas_call(
        _peer_hid_kernel,
        grid=(nblk,),
        in_specs=[_smem_blk(lambda i: (i, 0, 0)), pl.BlockSpec((PEER_TOK, sub, LANES), lambda i: (i, 0, 0)), vblk,
                  _table_spec(utab)],
        out_specs=vblk,
        out_shape=jax.ShapeDtypeStruct((nblk, PEER_TOK, PEER_PAIRS), f32),
        scratch_shapes=[pltpu.VMEM((rows, LANES), i32)] * PEER_STAGES + [pltpu.VMEM((PEER_TOK, PEER_PAIRS), f32)],
        compiler_params=_cparams("arbitrary"),
        name="peer_hid",
    )(idx, x3, gate, utab)


def _peer_output(idx, coef, h2, vtab, batch, seq, length):
    d = h2.shape[-1]
    sub = d // LANES
    nt = seq // PEER_TOK
    skip = (length - seq) // PEER_TOK
    rows = TABLE_ROWS * PEER_PAIRS
    blk = lambda i, j: (i * nt + j, 0, 0)
    return pl.pallas_call(
        _peer_out_kernel,
        grid=(batch, nt),
        in_specs=[_smem_blk(blk), pl.BlockSpec((None, PEER_TOK, PEER_PAIRS), blk),
                  pl.BlockSpec((None, PEER_TOK, d), lambda i, j: (i, j + skip, 0)),
                  _table_spec(vtab)],
        out_specs=pl.BlockSpec((None, PEER_TOK, d), lambda i, j: (i, j, 0)),
        out_shape=jax.ShapeDtypeStruct((batch, seq, d), f32),
        scratch_shapes=[pltpu.VMEM((rows, LANES), i32)] * PEER_STAGES + [pltpu.VMEM((PEER_TOK, sub, LANES), f32)],
        compiler_params=_cparams("arbitrary", "arbitrary"),
        name="peer_out",
    )(idx, coef, h2.reshape(batch, length, d), vtab)


def _pad_cols(w, width):
    return jnp.pad(w, ((0, 0), (0, width - w.shape[1])))


def _pad_rows(w, height):
    return jnp.pad(w, ((0, height - w.shape[0]), (0, 0)))


def kernel(x, meta_tokens, norm1_g, w_in, fox_q_norm, fox_k_norm, fox_f_bias, rwkv_mu, rwkv_w0, rwkv_w_up, rwkv_a0, rwkv_a_up, rwkv_g_up, rwkv_k_k, rwkv_k_a, rwkv_r_k, rwkv_ln_g, rwkv_ln_b, w_branch_fox, w_branch_rwkv, w_out, norm2_g, peer_w_q, peer_sub_k1, peer_sub_k2, peer_u, peer_v):
    batch, seq, d = x.shape
    assert w_in.shape[0] == 1, "one layer"
    assert seq % PEER_TOK == 0 and d % LANES == 0
    t_real = seq + N_META
    pad = (-t_real) % ATTN_BLOCK
    length = t_real + pad
    ta = _attn_block(length)

    meta = jnp.broadcast_to(meta_tokens[None].astype(x.dtype), (batch, N_META, d))
    hpad = jnp.concatenate([jnp.zeros((batch, pad, d), x.dtype), meta, x], axis=1).reshape(batch * length, d)
    w = w_in[0]
    fox_w = 3 * WIDTH + N_HEADS
    o = fox_w
    lora = lambda lo, n, width: _pad_cols(w[:, lo:lo + n], width)
    w_all = jnp.concatenate([
        w[:, 0:3 * WIDTH], _pad_cols(w[:, 3 * WIDTH:fox_w], LANES),
        w[:, o:o + 3 * WIDTH],
        lora(o + 3 * WIDTH, DECAY_LORA, LANES),
        lora(o + 3 * WIDTH + DECAY_LORA, AAA_LORA, LANES),
        lora(o + 3 * WIDTH + DECAY_LORA + AAA_LORA, GATE_LORA, 2 * LANES),
        w[:, o + 3 * WIDTH + DECAY_LORA + AAA_LORA + GATE_LORA:],
    ], axis=1).astype(bf16)
    mu = rwkv_mu[0]
    mu_all = jnp.concatenate([
        mu[0:3 * WIDTH], jnp.pad(mu[3 * WIDTH:3 * WIDTH + DECAY_LORA], (0, LANES - DECAY_LORA)),
        jnp.pad(mu[3 * WIDTH + DECAY_LORA:3 * WIDTH + DECAY_LORA + AAA_LORA], (0, LANES - AAA_LORA)),
        jnp.pad(mu[3 * WIDTH + DECAY_LORA + AAA_LORA:], (0, 2 * LANES - GATE_LORA)),
    ])[None]
    head_of = jnp.arange(WIDTH) // HEAD_DIM
    bdm = (head_of[:, None] == head_of[None, :]).astype(f32) / HEAD_DIM
    vec = lambda p: p[0].reshape(1, -1)

    q, k, v, lf, zr, gates = _proj(hpad, vec(norm1_g), w_all, vec(fox_q_norm), vec(fox_k_norm),
                                   jnp.pad(fox_f_bias[0], (0, LANES - N_HEADS))[None], bdm.astype(bf16))
    three = lambda a: a.reshape(batch, length, a.shape[-1])
    y_fox = _attention(three(q), three(k), _key_bias(three(lf), pad), three(v), ta)
    y_rwkv = _rwkv(three(zr), mu_all, vec(rwkv_w0), vec(rwkv_a0), vec(rwkv_k_k), vec(rwkv_k_a), vec(rwkv_r_k),
                   vec(rwkv_ln_g), vec(rwkv_ln_b),
                   _pad_rows(rwkv_w_up[0], LANES).astype(bf16), _pad_rows(rwkv_a_up[0], LANES).astype(bf16),
                   _pad_rows(rwkv_g_up[0], 2 * LANES).astype(bf16), bdm.astype(bf16))
    h2 = _merge(y_fox.reshape(-1, WIDTH), y_rwkv.reshape(-1, WIDTH), gates, hpad,
                w_branch_fox[0].astype(bf16), w_branch_rwkv[0].astype(bf16), w_out[0].astype(bf16))

    xn2, idx, gate = _peer_select(h2, vec(norm2_g), peer_w_q[0].T.astype(bf16),
                                  peer_sub_k1[0].astype(bf16), peer_sub_k2[0].astype(bf16), batch, seq, length)
    coef = _peer_hidden(idx, xn2, gate, _pack_table(peer_u[0]))
    out = _peer_output(idx, coef, h2, _pack_table(peer_v[0]), batch, seq, length)
    return out.reshape(batch, seq, d)
```

```python
import functools

import jax
import jax.numpy as jnp
from jax import lax
from jax.experimental import pallas as pl
from jax.experimental.pallas import tpu as pltpu

f32 = jnp.float32
bf16 = jnp.bfloat16
i32 = jnp.int32
HIGHEST = lax.Precision.HIGHEST

N_META = 16
HEAD_DIM = 64
N_HEADS = 8
WIDTH = N_HEADS * HEAD_DIM
DECAY_LORA, AAA_LORA, GATE_LORA = 64, 64, 160
ATTN_BLOCK = 128
PEER_HEADS, PEER_QDIM, N_KEYS, PEER_TOPK = 8, 256, 128, 16
NORM_EPS = 1e-6
RWKV_GN_EPS = 64e-5
MASK_VALUE = -1e30
LOG2_E = 1.4426950408889634

LANES = 128
VMEM_LIMIT_BYTES = 56 * 1024 * 1024

CHUNK = 64
RWKV_ROWS = 6 * CHUNK
PAIR = 2 * HEAD_DIM
PEER_TOK = 128
PEER_PAIRS = PEER_HEADS * PEER_TOPK
PEER_STAGES = 2
TABLE_ROWS = 4
FOX_COLS = 3 * WIDTH + LANES
RWKV_COLS = 3 * WIDTH + 2 * LANES + 2 * LANES


def _row_block(n):
    for t in (512, 384, 256, 128):
        if n % t == 0:
            return t
    raise ValueError(f"row count {n} is not a multiple of 128")


def _attn_block(length):
    return 384 if length % 384 == 0 else ATTN_BLOCK


def _cparams(*sem):
    return pltpu.CompilerParams(dimension_semantics=sem, vmem_limit_bytes=VMEM_LIMIT_BYTES)


def _const_spec(shape):
    nd = len(shape)
    return pl.BlockSpec(shape, lambda *_: (0,) * nd)


def _nt(a, b, **kw):
    return lax.dot_general(a, b, (((1,), (1,)), ((), ())), preferred_element_type=f32, **kw)


def _tn(a, b, **kw):
    return lax.dot_general(a, b, (((0,), (0,)), ((), ())), preferred_element_type=f32, **kw)


def _mm(a, b, **kw):
    return jnp.dot(a, b, preferred_element_type=f32, **kw)


def _softplus(y):
    return jnp.maximum(y, 0.0) + jnp.log1p(jnp.exp(-jnp.abs(y)))


def _sigmoid(y):
    return 1.0 / (1.0 + jnp.exp(-y))


def _three_term_mm(a, x):
    hi = x.astype(bf16)
    r1 = x - hi.astype(f32)
    mid = r1.astype(bf16)
    lo = (r1 - mid.astype(f32)).astype(bf16)
    return _mm(a, hi) + _mm(a, mid) + _mm(a, lo)


def _pair_mm(x, bd):
    return jnp.concatenate([_mm(x[:, p:p + PAIR], bd) for p in range(0, x.shape[1], PAIR)], axis=1)


def _head_mean(x, bd):
    hi = x.astype(bf16)
    lo = (x - hi.astype(f32)).astype(bf16)
    return _pair_mm(hi, bd) + _pair_mm(lo, bd)


def _padded_row_specs(seq, tm, d):
    pieces = tm // ATTN_BLOCK
    last = seq // ATTN_BLOCK - 1

    def x_piece(k):
        return pl.BlockSpec((None, ATTN_BLOCK, d),
                            lambda b, j: (b, jnp.clip(j * pieces + k - 1, 0, last), 0))

    return [_const_spec((ATTN_BLOCK, d))] + [x_piece(k) for k in range(pieces)]


def _padded_rows(lead_ref, x_refs):
    first = jnp.where(pl.program_id(1) == 0, lead_ref[...], x_refs[0][...])
    return jnp.concatenate([first] + [r[...] for r in x_refs[1:]], axis=0)


def _proj_kernel(*refs):
    npiece = len(refs) - 13
    lead_ref, x_refs = refs[0], refs[1:1 + npiece]
    g_ref, w_ref, qn_ref, kn_ref, fb_ref, bd_ref, q_ref, k_ref, v_ref, lf_ref, zr_ref, gate_ref = refs[1 + npiece:]
    x = _padded_rows(lead_ref, x_refs)
    ms = jnp.mean(x * x, axis=-1, keepdims=True)
    xn = (x * lax.rsqrt(ms + NORM_EPS) * g_ref[...]).astype(bf16)

    def head_norm(z, gain):
        msq = _pair_mm((z * z).astype(bf16), bd_ref[...])
        return z * lax.rsqrt(msq + NORM_EPS) * gain

    zq = _mm(xn, w_ref[:, 0:WIDTH])
    q_ref[...] = (head_norm(zq, qn_ref[...]) * (HEAD_DIM ** -0.5 * LOG2_E)).astype(bf16)
    zk = _mm(xn, w_ref[:, WIDTH:2 * WIDTH])
    k_ref[...] = head_norm(zk, kn_ref[...]).astype(bf16)
    v_ref[...] = _mm(xn, w_ref[:, 2 * WIDTH:3 * WIDTH]).astype(bf16)
    zf = _mm(xn, w_ref[:, 3 * WIDTH:FOX_COLS]) + fb_ref[...]
    lf_ref[...] = -_softplus(-zf) * LOG2_E
    zr_ref[...] = _mm(xn, w_ref[:, FOX_COLS:FOX_COLS + RWKV_COLS])
    zg = _mm(xn, w_ref[:, FOX_COLS + RWKV_COLS:])
    gate_ref[...] = _sigmoid(zg).astype(bf16)


def _proj(lead, x, length, g, w, qn, kn, fb, bd):
    batch, seq, d = x.shape
    tm = _attn_block(length)
    nb = length // tm
    n = batch * length
    ncol = w.shape[1]
    ngate = ncol - FOX_COLS - RWKV_COLS
    row = lambda c: pl.BlockSpec((tm, c), lambda b, j: (b * nb + j, 0))
    x_specs = _padded_row_specs(seq, tm, d)
    return pl.pallas_call(
        _proj_kernel,
        grid=(batch, nb),
        in_specs=x_specs + [_const_spec((1, d)), _const_spec((d, ncol)), _const_spec((1, WIDTH)),
                            _const_spec((1, WIDTH)), _const_spec((1, LANES)), _const_spec((PAIR, PAIR))],
        out_specs=[row(WIDTH), row(WIDTH), row(WIDTH), row(LANES), row(RWKV_COLS), row(ngate)],
        out_shape=[jax.ShapeDtypeStruct((n, WIDTH), bf16)] * 3
        + [jax.ShapeDtypeStruct((n, LANES), f32), jax.ShapeDtypeStruct((n, RWKV_COLS), f32),
           jax.ShapeDtypeStruct((n, ngate), bf16)],
        compiler_params=_cparams("parallel", "parallel"),
        name="proj",
    )(lead, *([x] * (len(x_specs) - 1)), g, w, qn, kn, fb, bd)


def _cumsum_kernel(lf_ref, o_ref, *, pad):
    length = lf_ref.shape[0]
    r = lax.broadcasted_iota(i32, (LANES, LANES), 0)
    c = lax.broadcasted_iota(i32, (LANES, LANES), 1)
    tri = (r >= c).astype(f32)
    carry = jnp.zeros((1, LANES), f32)
    for blk in range(length // LANES):
        cs = _mm(tri, lf_ref[blk * LANES:(blk + 1) * LANES, :], precision=HIGHEST) + carry
        carry = cs[LANES - 1:LANES, :]
        neg = jnp.where(r + blk * LANES >= pad, -cs, MASK_VALUE)
        for h in range(N_HEADS):
            o_ref[h, blk * LANES:(blk + 1) * LANES, :] = jnp.broadcast_to(neg[:, h:h + 1], (LANES, LANES))


def _key_bias(lf, pad):
    b, length, _ = lf.shape
    return pl.pallas_call(
        functools.partial(_cumsum_kernel, pad=pad),
        grid=(b,),
        in_specs=[pl.BlockSpec((None, length, LANES), lambda i: (i, 0, 0))],
        out_specs=pl.BlockSpec((None, N_HEADS, length, LANES), lambda i: (i, 0, 0, 0)),
        out_shape=jax.ShapeDtypeStruct((b, N_HEADS, length, LANES), f32),
        compiler_params=_cparams("parallel"),
        name="cumsum",
    )(lf)


def _attn_kernel(q_ref, k_ref, kb_ref, v_ref, o_ref, *, ta):
    qi = pl.program_id(2)
    lane = lax.broadcasted_iota(i32, (1, PAIR), 1)
    key = lax.broadcasted_iota(i32, (ta, ATTN_BLOCK), 0)
    qry = lax.broadcasted_iota(i32, (ta, ATTN_BLOCK), 1)
    nrow = ta // ATTN_BLOCK
    pcs = [(h, r) for h in range(2) for r in range(nrow)]
    qa = []
    for h, r in pcs:
        qr = q_ref[r * ATTN_BLOCK:(r + 1) * ATTN_BLOCK, :]
        qa.append(jnp.where((lane // HEAD_DIM) == h, qr, jnp.zeros_like(qr)))

    def block(j, carries, diagonal):
        start = pl.multiple_of(j * ta, ta)
        ks = k_ref[pl.ds(start, ta), :]
        vt = v_ref[pl.ds(start, ta), :].T
        bias = [kb_ref[h, pl.ds(start, ta), :] for h in range(2)]
        ids = range(len(pcs))
        s = [_nt(ks, qa[i]) + bias[pcs[i][0]] for i in ids]
        if diagonal:
            s = [jnp.where(key <= qry + r * ATTN_BLOCK, s[i], MASK_VALUE) for i, (h, r) in enumerate(pcs)]
        m_new = [jnp.maximum(carries[i][0], jnp.max(s[i], axis=0, keepdims=True)) for i in ids]
        p = [jnp.exp2(s[i] - m_new[i]) for i in ids]
        alpha = [jnp.exp2(carries[i][0] - m_new[i]) for i in ids]
        l = [alpha[i] * carries[i][1] + jnp.sum(p[i], axis=0, keepdims=True) for i in ids]
        pv = [_mm(vt, p[i].astype(bf16)) for i in ids]
        acc = [alpha[i] * carries[i][2] + pv[i] for i in ids]
        return tuple((m_new[i], l[i], acc[i]) for i in ids)

    init = (jnp.full((1, ATTN_BLOCK), MASK_VALUE, f32), jnp.zeros((1, ATTN_BLOCK), f32),
            jnp.zeros((PAIR, ATTN_BLOCK), f32))
    carries = lax.fori_loop(0, qi // 2, lambda j, c: block(2 * j + 1, block(2 * j, c, False), False),
                            (init,) * len(pcs))
    carries = lax.cond(qi % 2 == 1, lambda c: block(qi - 1, c, False), lambda c: c, carries)
    final = block(qi, carries, True)
    vrow = lax.broadcasted_iota(i32, (PAIR, 1), 0)
    for r in range(nrow):
        o0 = final[r][2] / final[r][1]
        o1 = final[nrow + r][2] / final[nrow + r][1]
        o_ref[r * ATTN_BLOCK:(r + 1) * ATTN_BLOCK, :] = jnp.where(vrow < HEAD_DIM, o0, o1).T.astype(o_ref.dtype)


def _attention(q, k, kbias, v, ta):
    b, length, _ = q.shape
    nb = length // ta
    npair = N_HEADS // 2
    full = pl.BlockSpec((None, length, PAIR), lambda i, p, j: (i, 0, p))
    return pl.pallas_call(
        functools.partial(_attn_kernel, ta=ta),
        grid=(b, npair, nb),
        in_specs=[pl.BlockSpec((None, ta, PAIR), lambda i, p, j: (i, j, p)), full,
                  pl.BlockSpec((None, 2, length, LANES), lambda i, p, j: (i, p, 0, 0)), full],
        out_specs=pl.BlockSpec((None, ta, PAIR), lambda i, p, j: (i, j, p)),
        out_shape=jax.ShapeDtypeStruct((b, length, WIDTH), bf16),
        compiler_params=_cparams("parallel", "parallel", "arbitrary"),
        name="attn",
    )(q, k, kbias, v)


def _stack(x, lane_head):
    zero = jnp.zeros_like(x)
    return jnp.concatenate([jnp.where(lane_head == 0, x, zero), jnp.where(lane_head == 1, x, zero)], axis=0)


def _unit_lower_inverse(ns, r, c):
    eye = (r == c).astype(f32)
    first = ((r ^ c) == 1) & ((r & 1) == 1)
    ts = [(eye - jnp.where(first, n, 0.0)).astype(bf16) for n in ns]
    s = 2
    while s < CHUNK:
        sel = ((r // (2 * s)) == (c // (2 * s))) & ((r & s) != 0) & ((c & s) == 0)
        low = [jnp.where(sel, n, 0.0).astype(bf16) for n in ns]
        tn = [_mm(t, lo).astype(bf16) for t, lo in zip(ts, low)]
        ts = [(t.astype(f32) - _mm(x, t)).astype(bf16) for t, x in zip(ts, tn)]
        s *= 2
    return ts


def _rwkv_kernel(z_ref, mu_ref, w0_ref, a0_ref, kk_ref, ka_ref, rk_ref, lng_ref, lnb_ref,
                 wup_ref, aup_ref, gup_ref, bdm_ref, o_ref, s_ref, prev_ref):
    ci = pl.program_id(1)

    @pl.when(ci == 0)
    def _():
        s_ref[...] = jnp.zeros_like(s_ref)
        prev_ref[...] = jnp.zeros_like(prev_ref)

    z = z_ref[...]
    rows = lax.broadcasted_iota(i32, (RWKV_ROWS, 1), 0)
    zprev = jnp.where(rows == 0, prev_ref[...], pltpu.roll(z, 1, 0))
    prev_ref[...] = z[RWKV_ROWS - 1:RWKV_ROWS, :]
    zs = z + mu_ref[...] * (zprev - z)
    r = zs[:, 0:WIDTH]
    k = zs[:, WIDTH:2 * WIDTH]
    v = zs[:, 2 * WIDTH:3 * WIDTH]
    o1 = 3 * WIDTH
    wd = zs[:, o1:o1 + LANES]
    ad = zs[:, o1 + LANES:o1 + 2 * LANES]
    gd = zs[:, o1 + 2 * LANES:o1 + 4 * LANES]

    w = -_softplus(-(w0_ref[...] + _mm(jnp.tanh(wd).astype(bf16), wup_ref[...]))) - 0.5
    ld = -jnp.exp(w)
    a = _sigmoid(a0_ref[...] + _mm(ad.astype(bf16), aup_ref[...]))
    g = _mm(_sigmoid(gd).astype(bf16), gup_ref[...])
    bdm = bdm_ref[...]
    kk = k * kk_ref[...]
    nrm = jnp.sqrt(_pair_mm((kk * kk).astype(bf16), bdm) * HEAD_DIM)
    kk = kk / jnp.maximum(nrm, 1e-12)
    k2 = k * (1.0 + (a - 1.0) * ka_ref[...])
    b = kk * a

    tr = lax.broadcasted_iota(i32, (CHUNK, CHUNK), 0)
    tc = lax.broadcasted_iota(i32, (CHUNK, CHUNK), 1)
    tri = (tr >= tc).astype(bf16)
    chunks = range(RWKV_ROWS // CHUNK)
    lc = jnp.concatenate([_three_term_mm(tri, ld[c * CHUNK:(c + 1) * CHUNK, :]) for c in chunks], axis=0)
    lp = lc - ld
    mids = [lc[c * CHUNK + CHUNK // 2 - 1:c * CHUNK + CHUNK // 2, :] for c in chunks]
    tots = [lc[(c + 1) * CHUNK - 1:(c + 1) * CHUNK, :] for c in chunks]
    mid, tot = mids[-1], tots[-1]
    for c in reversed(chunks[:-1]):
        mid = jnp.where(rows < (c + 1) * CHUNK, mids[c], mid)
        tot = jnp.where(rows < (c + 1) * CHUNK, tots[c], tot)
    kq = kk * jnp.exp(lp - mid)
    rr = r * jnp.exp(lc - mid)
    e_after = jnp.exp(mid - lc)
    kh = k2 * e_after
    bh = b * e_after
    e_end = jnp.exp(tot - lc)
    kc = k2 * e_end
    bc = b * e_end
    e_mid = [jnp.exp(m) for m in mids]
    p_end = [jnp.exp(t) for t in tots]

    sr = lax.broadcasted_iota(i32, (PAIR, PAIR), 0)
    sc = lax.broadcasted_iota(i32, (PAIR, PAIR), 1)
    same = (sr // CHUNK) == (sc // CHUNK)
    strict = same & ((sr % CHUNK) > (sc % CHUNK))
    incl = same & ((sr % CHUNK) >= (sc % CHUNK))
    lane_head = lax.broadcasted_iota(i32, (1, PAIR), 1) // HEAD_DIM

    npair = N_HEADS // 2
    units = [(c, p) for c in chunks for p in range(npair)]
    ids = range(len(units))
    lanes = [slice(p * PAIR, (p + 1) * PAIR) for _, p in units]
    st = lambda x: [_stack(x[c * CHUNK:(c + 1) * CHUNK, p * PAIR:(p + 1) * PAIR], lane_head) for c, p in units]
    kq_s, rr_s, kh_s, bh_s, v_s, kc_s, bc_s = (st(x) for x in (kq, rr, kh, bh, v, kc, bc))
    amat = [_nt(jnp.concatenate([kq_s[i], rr_s[i]], axis=0).astype(bf16),
                jnp.concatenate([kh_s[i], bh_s[i]], axis=0).astype(bf16)) for i in ids]
    a_kk = [jnp.where(strict, amat[i][0:PAIR, 0:PAIR], 0.0).astype(bf16) for i in ids]
    a_kb = [jnp.where(strict, amat[i][0:PAIR, PAIR:], 0.0) for i in ids]
    a_rk = [jnp.where(incl, amat[i][PAIR:, 0:PAIR], 0.0).astype(bf16) for i in ids]
    a_rb = [jnp.where(incl, amat[i][PAIR:, PAIR:], 0.0).astype(bf16) for i in ids]
    t = _unit_lower_inverse(a_kb, sr, sc)
    vb = [v_s[i].astype(bf16) for i in ids]
    av = [_mm(a_kk[i], vb[i]).astype(bf16) for i in ids]
    wm = [_mm(t[i], kq_s[i].astype(bf16)) for i in ids]
    uv = [_mm(t[i], av[i]) for i in ids]
    rq = [rr_s[i] - _mm(a_rb[i], wm[i].astype(bf16)) for i in ids]
    yv = [_mm(a_rk[i], vb[i]) - _mm(a_rb[i], uv[i].astype(bf16)) for i in ids]
    wm_t = [(wm[i] * e_mid[units[i][0]][:, lanes[i]]).astype(bf16) for i in ids]
    rq_t = [(rq[i] * e_mid[units[i][0]][:, lanes[i]]).astype(bf16) for i in ids]
    bcb = [bc_s[i].astype(bf16) for i in ids]
    omega = [_tn(wm_t[i], bcb[i]).astype(bf16) for i in ids]
    psi = [_tn(vb[i], kc_s[i].astype(bf16)) - _tn(uv[i].astype(bf16), bcb[i]) for i in ids]
    state = [s_ref[p] for p in range(npair)]
    ystack = []
    for c in chunks:
        new = []
        for p in range(npair):
            i = c * npair + p
            s_hi = state[p].astype(bf16)
            s_lo = (state[p] - s_hi.astype(f32)).astype(bf16)
            ystack.append(_nt(rq_t[i], s_hi) + yv[i])
            new.append(state[p] * p_end[c][:, lanes[i]] - (_mm(s_hi, omega[i]) + _mm(s_lo, omega[i])) + psi[i])
        state = new
    for p in range(npair):
        s_ref[p] = state[p]
    y = jnp.concatenate(
        [jnp.concatenate([ystack[c * npair + p][0:CHUNK, :] + ystack[c * npair + p][CHUNK:, :] for p in range(npair)],
                         axis=1) for c in chunks], axis=0)

    mean = _head_mean(y, bdm)
    yc = y - mean
    var = _head_mean(yc * yc, bdm)
    yn = yc * lax.rsqrt(var + RWKV_GN_EPS) * lng_ref[...] + lnb_ref[...]
    bonus = _pair_mm((r * k2 * rk_ref[...]).astype(bf16), bdm) * HEAD_DIM * v
    o_ref[...] = ((yn + bonus) * g).astype(o_ref.dtype)


def _rwkv(zr, mu, w0, a0, k_k, k_a, r_k, ln_g, ln_b, w_up, a_up, g_up, bdm):
    b, length, _ = zr.shape
    assert length % RWKV_ROWS == 0, (length, RWKV_ROWS)
    vec = _const_spec((1, WIDTH))
    return pl.pallas_call(
        _rwkv_kernel,
        grid=(b, length // RWKV_ROWS),
        in_specs=[pl.BlockSpec((None, RWKV_ROWS, RWKV_COLS), lambda i, c: (i, c, 0)),
                  _const_spec((1, RWKV_COLS)), vec, vec, vec, vec, vec, vec, vec,
                  _const_spec((LANES, WIDTH)), _const_spec((LANES, WIDTH)), _const_spec((2 * LANES, WIDTH)),
                  _const_spec((PAIR, PAIR))],
        out_specs=pl.BlockSpec((None, RWKV_ROWS, WIDTH), lambda i, c: (i, c, 0)),
        out_shape=jax.ShapeDtypeStruct((b, length, WIDTH), bf16),
        scratch_shapes=[pltpu.VMEM((N_HEADS // 2, PAIR, PAIR), f32), pltpu.VMEM((1, RWKV_COLS), f32)],
        compiler_params=_cparams("parallel", "arbitrary"),
        name="rwkv",
    )(zr, mu, w0, a0, k_k, k_a, r_k, ln_g, ln_b, w_up, a_up, g_up, bdm)


def _merge_kernel(*refs):
    npiece = len(refs) - 8
    lead_ref, x_refs = refs[0], refs[1:1 + npiece]
    yf_ref, yr_ref, gate_ref, wf_ref, wr_ref, wo_ref, o_ref = refs[1 + npiece:]
    d = o_ref.shape[1]
    pf = _mm(yf_ref[...], wf_ref[...])
    pr = _mm(yr_ref[...], wr_ref[...])
    mixed = gate_ref[:, 0:d].astype(f32) * pf + gate_ref[:, d:2 * d].astype(f32) * pr
    o_ref[...] = _padded_rows(lead_ref, x_refs) + _mm(mixed.astype(bf16), wo_ref[...])


def _merge(lead, x, length, yf, yr, gates, wf, wr, wo):
    batch, seq, d = x.shape
    tm = _attn_block(length)
    nb = length // tm
    row = lambda c: pl.BlockSpec((tm, c), lambda b, j: (b * nb + j, 0))
    x_specs = _padded_row_specs(seq, tm, d)
    return pl.pallas_call(
        _merge_kernel,
        grid=(batch, nb),
        in_specs=x_specs + [row(WIDTH), row(WIDTH), row(2 * d),
                            _const_spec((WIDTH, d)), _const_spec((WIDTH, d)), _const_spec((d, d))],
        out_specs=row(d),
        out_shape=jax.ShapeDtypeStruct((batch * length, d), f32),
        compiler_params=_cparams("parallel", "parallel"),
        name="merge",
    )(lead, *([x] * (len(x_specs) - 1)), yf, yr, gates, wf, wr, wo)


def _top_rows(s, k):
    n = s.shape[0]
    pos = lax.broadcasted_iota(i32, s.shape, 0).astype(f32)
    vals, outs = [], []
    for _ in range(k):
        m = jnp.max(s, axis=0, keepdims=True)
        first = jnp.min(jnp.where(s == m, pos, float(n)), axis=0, keepdims=True)
        vals.append(m)
        outs.append(first)
        s = jnp.where(pos == first, -jnp.inf, s)
    return jnp.concatenate(vals, axis=0), jnp.concatenate(outs, axis=0)


def _staircase():
    pairs = [(a, b) for a in range(PEER_TOPK) for b in range(PEER_TOPK) if (a + 1) * (b + 1) <= PEER_TOPK]
    rows = -(-len(pairs) // 8) * 8
    sel = jnp.zeros((2, rows, PEER_TOPK), f32)
    r = jnp.arange(len(pairs))
    sel = sel.at[0, r, jnp.array([a for a, _ in pairs])].set(1.0)
    sel = sel.at[1, r, jnp.array([b for _, b in pairs])].set(1.0)
    return sel, len(pairs)


def _peer_sel_kernel(h_ref, g_ref, wq_ref, k1_ref, k2_ref, sel_ref, xn_ref, idx_ref, gate_ref, *, n_cand):
    x = h_ref[...]
    ms = jnp.mean(x * x, axis=-1, keepdims=True)
    xn = x * lax.rsqrt(ms + NORM_EPS) * g_ref[...]
    xn_ref[...] = xn.reshape(xn_ref.shape)
    qt = _nt(wq_ref[...], xn.astype(bf16))
    half = PEER_QDIM // 2
    sel_a, sel_b = sel_ref[0], sel_ref[1]
    pick = lambda sel, t: _mm(sel, t, precision=HIGHEST)
    s1 = jnp.concatenate([_mm(k1_ref[...], qt[h * PEER_QDIM:h * PEER_QDIM + half, :].astype(bf16))
                          for h in range(PEER_HEADS)], axis=1)
    s2 = jnp.concatenate([_mm(k2_ref[...], qt[h * PEER_QDIM + half:(h + 1) * PEER_QDIM, :].astype(bf16))
                          for h in range(PEER_HEADS)], axis=1)
    t1, i1 = _top_rows(s1, PEER_TOPK)
    t2, i2 = _top_rows(s2, PEER_TOPK)
    rows = lax.broadcasted_iota(i32, (sel_a.shape[0], PEER_HEADS * PEER_TOK), 0).astype(f32)
    cand_s = jnp.where(rows < n_cand, pick(sel_a, t1) + pick(sel_b, t2), -jnp.inf)
    cand_i = pick(sel_a, i1) * N_KEYS + pick(sel_b, i2)
    top_s, top_p = _top_rows(cand_s, PEER_TOPK)
    top_i = jnp.concatenate(
        [jnp.max(jnp.where(rows == top_p[j:j + 1, :], cand_i, -1.0), axis=0, keepdims=True)
         for j in range(PEER_TOPK)], axis=0).astype(i32)
    e = jnp.exp(top_s - top_s[0:1, :])
    gate = e / jnp.sum(e, axis=0, keepdims=True)
    by_head = lambda a: jnp.concatenate([a[:, h * PEER_TOK:(h + 1) * PEER_TOK] for h in range(PEER_HEADS)], axis=0)
    idx_ref[...] = by_head(top_i).T * TABLE_ROWS
    gate_ref[...] = by_head(gate).T


def _peer_select(h2, g, wq_t, k1, k2, batch, seq, length):
    d = h2.shape[-1]
    nt = seq // PEER_TOK
    skip = (length - seq) // PEER_TOK
    nblk = batch * nt
    sel, n_cand = _staircase()
    blk = pl.BlockSpec((None, PEER_TOK, PEER_PAIRS), lambda i, j: (i * nt + j, 0, 0))
    return pl.pallas_call(
        functools.partial(_peer_sel_kernel, n_cand=n_cand),
        grid=(batch, nt),
        in_specs=[pl.BlockSpec((None, PEER_TOK, d), lambda i, j: (i, j + skip, 0)),
                  _const_spec((1, d)), _const_spec(wq_t.shape), _const_spec(k1.shape), _const_spec(k2.shape),
                  _const_spec(sel.shape)],
        out_specs=[pl.BlockSpec((PEER_TOK, d // LANES, LANES), lambda i, j: (i * nt + j, 0, 0)), blk, blk],
        out_shape=[jax.ShapeDtypeStruct((nblk * PEER_TOK, d // LANES, LANES), f32),
                   jax.ShapeDtypeStruct((nblk, PEER_TOK, PEER_PAIRS), i32),
                   jax.ShapeDtypeStruct((nblk, PEER_TOK, PEER_PAIRS), f32)],
        compiler_params=_cparams("parallel", "parallel"),
        name="peer_sel",
    )(h2.reshape(batch, length, d), g, wq_t, k1, k2, sel)


def _pack_table(tab):
    e, d = tab.shape
    assert d == 2 * TABLE_ROWS * LANES
    tm = _row_block(e)
    return pl.pallas_call(
        _pack_kernel,
        grid=(e // tm,),
        in_specs=[pl.BlockSpec((tm, d), lambda i: (i, 0))],
        out_specs=pl.BlockSpec((tm * TABLE_ROWS, LANES), lambda i: (i, 0)),
        out_shape=jax.ShapeDtypeStruct((e * TABLE_ROWS, LANES), i32),
        compiler_params=_cparams("parallel"),
        name="pack",
    )(tab)


def _pack_kernel(x_ref, o_ref):
    half = x_ref.shape[1] // 2
    bits = lax.bitcast_convert_type(x_ref[...].astype(bf16).astype(f32), i32)
    words = bits[:, 0:half] | lax.shift_right_logical(bits[:, half:], 16)
    o_ref[...] = words.reshape(o_ref.shape)


def _unpack(word):
    return (lax.bitcast_convert_type(word & jnp.int32(-65536), f32), lax.bitcast_convert_type(word << 16, f32))


def _expert_rows(idx_ref, tab_ref, t, k):
    return tab_ref[pl.ds(pl.multiple_of(idx_ref[t, k], TABLE_ROWS), TABLE_ROWS), :]


def _gather(idx_ref, tab_ref, stage_ref, t):
    for k in range(PEER_PAIRS):
        stage_ref[TABLE_ROWS * k:TABLE_ROWS * (k + 1), :] = _expert_rows(idx_ref, tab_ref, t, k)


def _staged(stage_ref, s):
    return _unpack(stage_ref[pl.ds(s, PEER_PAIRS, stride=TABLE_ROWS), :])


def _token_pipeline(idx_ref, tab_ref, stage, consume):
    ahead = len(stage)
    for t in range(ahead):
        _gather(idx_ref, tab_ref, stage[t], t)
    for t in range(PEER_TOK):
        consume(t, stage[t % ahead])
        if t + ahead < PEER_TOK:
            _gather(idx_ref, tab_ref, stage[t % ahead], t + ahead)


def _peer_hid_kernel(idx_ref, x_ref, gate_ref, tab_ref, o_ref, *scratch):
    stage, dots_ref = scratch[:-1], scratch[-1]
    ones = jnp.ones((8, LANES), bf16)

    def dots(t, buf):
        xt = x_ref[t]
        acc = jnp.zeros((PEER_PAIRS, LANES), f32)
        for s in range(TABLE_ROWS):
            hi, lo = _staged(buf, s)
            acc = acc + hi * xt[s:s + 1, :] + lo * xt[TABLE_ROWS + s:TABLE_ROWS + s + 1, :]
        a_hi = acc.astype(bf16)
        a_lo = (acc - a_hi.astype(f32)).astype(bf16)
        sums = _nt(ones, a_hi) + _nt(ones, a_lo)
        dots_ref[t:t + 1, :] = sums[0:1, :]

    _token_pipeline(idx_ref, tab_ref, stage, dots)
    hid = dots_ref[...]
    o_ref[...] = gate_ref[...] * (0.5 * hid * (1.0 + lax.erf(hid * (2.0 ** -0.5))))


def _two_term_rows(row):
    hi = row.astype(bf16).astype(f32)
    r = lax.broadcasted_iota(i32, (8, LANES), 0)
    return jnp.where(r == 0, hi, jnp.where(r == 1, row - hi, 0.0)).astype(bf16)


def _peer_out_kernel(idx_ref, c_ref, h_ref, tab_ref, o_ref, *scratch):
    stage, acc_ref = scratch[:-1], scratch[-1]
    row = lax.broadcasted_iota(i32, (8, LANES), 0)
    acc_ref[...] = h_ref[...].reshape(acc_ref.shape)

    def combine(t, buf):
        lhs = _two_term_rows(c_ref[t:t + 1, :])
        out = acc_ref[t]
        for s in range(TABLE_ROWS):
            hi, lo = _staged(buf, s)
            a = _mm(lhs, hi.astype(bf16))
            b = _mm(lhs, lo.astype(bf16))
            out = (out + jnp.where(row == s, a[0:1, :] + a[1:2, :], 0.0)
                   + jnp.where(row == TABLE_ROWS + s, b[0:1, :] + b[1:2, :], 0.0))
        acc_ref[t] = out

    _token_pipeline(idx_ref, tab_ref, stage, combine)
    o_ref[...] = acc_ref[...].reshape(o_ref.shape)


def _smem_blk(index_map):
    return pl.BlockSpec((None, PEER_TOK, PEER_PAIRS), index_map, memory_space=pltpu.SMEM,
                        pipeline_mode=pl.Buffered(1))


def _table_spec(tab):
    return pl.BlockSpec(tab.shape, lambda *_: (0, 0), pipeline_mode=pl.Buffered(1))


def _peer_hidden(idx, x3, gate, utab):
    nblk = idx.shape[0]
    sub = x3.shape[1]
    vblk = pl.BlockSpec((None, PEER_TOK, PEER_PAIRS), lambda i: (i, 0, 0))
    rows = TABLE_ROWS * PEER_PAIRS
    return pl.pallas_call(
        _peer_hid_kernel,
        grid=(nblk,),
        in_specs=[_smem_blk(lambda i: (i, 0, 0)), pl.BlockSpec((PEER_TOK, sub, LANES), lambda i: (i, 0, 0)), vblk,
                  _table_spec(utab)],
        out_specs=vblk,
        out_shape=jax.ShapeDtypeStruct((nblk, PEER_TOK, PEER_PAIRS), f32),
        scratch_shapes=[pltpu.VMEM((rows, LANES), i32)] * PEER_STAGES + [pltpu.VMEM((PEER_TOK, PEER_PAIRS), f32)],
        compiler_params=_cparams("arbitrary"),
        name="peer_hid",
    )(idx, x3, gate, utab)


def _peer_output(idx, coef, h2, vtab, batch, seq, length):
    d = h2.shape[-1]
    sub = d // LANES
    nt = seq // PEER_TOK
    skip = (length - seq) // PEER_TOK
    rows = TABLE_ROWS * PEER_PAIRS
    blk = lambda i, j: (i * nt + j, 0, 0)
    return pl.pallas_call(
        _peer_out_kernel,
        grid=(batch, nt),
        in_specs=[_smem_blk(blk), pl.BlockSpec((None, PEER_TOK, PEER_PAIRS), blk),
                  pl.BlockSpec((None, PEER_TOK, d), lambda i, j: (i, j + skip, 0)),
                  _table_spec(vtab)],
        out_specs=pl.BlockSpec((None, PEER_TOK, d), lambda i, j: (i, j, 0)),
        out_shape=jax.ShapeDtypeStruct((batch, seq, d), f32),
        scratch_shapes=[pltpu.VMEM((rows, LANES), i32)] * PEER_STAGES + [pltpu.VMEM((PEER_TOK, sub, LANES), f32)],
        compiler_params=_cparams("arbitrary", "arbitrary"),
        name="peer_out",
    )(idx, coef, h2.reshape(batch, length, d), vtab)


def _pad_cols(w, width):
    return jnp.pad(w, ((0, 0), (0, width - w.shape[1])))


def _pad_rows(w, height):
    return jnp.pad(w, ((0, height - w.shape[0]), (0, 0)))


def kernel(x, meta_tokens, norm1_g, w_in, fox_q_norm, fox_k_norm, fox_f_bias, rwkv_mu, rwkv_w0, rwkv_w_up, rwkv_a0, rwkv_a_up, rwkv_g_up, rwkv_k_k, rwkv_k_a, rwkv_r_k, rwkv_ln_g, rwkv_ln_b, w_branch_fox, w_branch_rwkv, w_out, norm2_g, peer_w_q, peer_sub_k1, peer_sub_k2, peer_u, peer_v):
    batch, seq, d = x.shape
    assert w_in.shape[0] == 1, "one layer"
    assert seq % PEER_TOK == 0 and d % LANES == 0
    t_real = seq + N_META
    pad = (-t_real) % ATTN_BLOCK
    length = t_real + pad
    ta = _attn_block(length)

    assert length - seq == ATTN_BLOCK
    lead = jnp.pad(meta_tokens.astype(x.dtype), ((pad, 0), (0, 0)))
    w = w_in[0]
    fox_w = 3 * WIDTH + N_HEADS
    o = fox_w
    lora = lambda lo, n, width: _pad_cols(w[:, lo:lo + n], width)
    w_all = jnp.concatenate([
        w[:, 0:3 * WIDTH], _pad_cols(w[:, 3 * WIDTH:fox_w], LANES),
        w[:, o:o + 3 * WIDTH],
        lora(o + 3 * WIDTH, DECAY_LORA, LANES),
        lora(o + 3 * WIDTH + DECAY_LORA, AAA_LORA, LANES),
        lora(o + 3 * WIDTH + DECAY_LORA + AAA_LORA, GATE_LORA, 2 * LANES),
        w[:, o + 3 * WIDTH + DECAY_LORA + AAA_LORA + GATE_LORA:],
    ], axis=1).astype(bf16)
    mu = rwkv_mu[0]
    mu_all = jnp.concatenate([
        mu[0:3 * WIDTH], jnp.pad(mu[3 * WIDTH:3 * WIDTH + DECAY_LORA], (0, LANES - DECAY_LORA)),
        jnp.pad(mu[3 * WIDTH + DECAY_LORA:3 * WIDTH + DECAY_LORA + AAA_LORA], (0, LANES - AAA_LORA)),
        jnp.pad(mu[3 * WIDTH + DECAY_LORA + AAA_LORA:], (0, 2 * LANES - GATE_LORA)),
    ])[None]
    head_of = jnp.arange(PAIR) // HEAD_DIM
    bdm = ((head_of[:, None] == head_of[None, :]).astype(f32) / HEAD_DIM).astype(bf16)
    vec = lambda p: p[0].reshape(1, -1)

    q, k, v, lf, zr, gates = _proj(lead, x, length, vec(norm1_g), w_all, vec(fox_q_norm), vec(fox_k_norm),
                                   jnp.pad(fox_f_bias[0], (0, LANES - N_HEADS))[None], bdm)
    three = lambda a: a.reshape(batch, length, a.shape[-1])
    y_fox = _attention(three(q), three(k), _key_bias(three(lf), pad), three(v), ta)
    y_rwkv = _rwkv(three(zr), mu_all, vec(rwkv_w0), vec(rwkv_a0), vec(rwkv_k_k), vec(rwkv_k_a), vec(rwkv_r_k),
                   vec(rwkv_ln_g), vec(rwkv_ln_b),
                   _pad_rows(rwkv_w_up[0], LANES).astype(bf16), _pad_rows(rwkv_a_up[0], LANES).astype(bf16),
                   _pad_rows(rwkv_g_up[0], 2 * LANES).astype(bf16), bdm)
    h2 = _merge(lead, x, length, y_fox.reshape(-1, WIDTH), y_rwkv.reshape(-1, WIDTH), gates,
                w_branch_fox[0].astype(bf16), w_branch_rwkv[0].astype(bf16), w_out[0].astype(bf16))

    xn2, idx, gate = _peer_select(h2, vec(norm2_g), peer_w_q[0].T.astype(bf16),
                                  peer_sub_k1[0].astype(bf16), peer_sub_k2[0].astype(bf16), batch, seq, length)
    coef = _peer_hidden(idx, xn2, gate, _pack_table(peer_u[0]))
    out = _peer_output(idx, coef, h2, _pack_table(peer_v[0]), batch, seq, length)
    return out.reshape(batch, seq, d)
```

```python
import functools

import jax
import jax.numpy as jnp
from jax import lax
from jax.experimental import pallas as pl
from jax.experimental.pallas import tpu as pltpu

f32 = jnp.float32
bf16 = jnp.bfloat16
i32 = jnp.int32
HIGHEST = lax.Precision.HIGHEST

N_META = 16
HEAD_DIM = 64
N_HEADS = 8
WIDTH = N_HEADS * HEAD_DIM
DECAY_LORA, AAA_LORA, GATE_LORA = 64, 64, 160
ATTN_BLOCK = 128
PEER_HEADS, PEER_QDIM, N_KEYS, PEER_TOPK = 8, 256, 128, 16
NORM_EPS = 1e-6
RWKV_GN_EPS = 64e-5
MASK_VALUE = -1e30
LOG2_E = 1.4426950408889634

LANES = 128
VMEM_LIMIT_BYTES = 56 * 1024 * 1024

CHUNK = 64
RWKV_ROWS = 6 * CHUNK
PAIR = 2 * HEAD_DIM
PEER_TOK = 128
PEER_PAIRS = PEER_HEADS * PEER_TOPK
PEER_STAGES = 2
TABLE_ROWS = 4
FOX_COLS = 3 * WIDTH + LANES
RWKV_COLS = 3 * WIDTH + 2 * LANES + 2 * LANES


def _row_block(n):
    for t in (512, 384, 256, 128):
        if n % t == 0:
            return t
    raise ValueError(f"row count {n} is not a multiple of 128")


def _attn_block(length):
    return 384 if length % 384 == 0 else ATTN_BLOCK


def _cparams(*sem):
    return pltpu.CompilerParams(dimension_semantics=sem, vmem_limit_bytes=VMEM_LIMIT_BYTES)


def _const_spec(shape):
    nd = len(shape)
    return pl.BlockSpec(shape, lambda *_: (0,) * nd)


def _nt(a, b, **kw):
    return lax.dot_general(a, b, (((1,), (1,)), ((), ())), preferred_element_type=f32, **kw)


def _tn(a, b, **kw):
    return lax.dot_general(a, b, (((0,), (0,)), ((), ())), preferred_element_type=f32, **kw)


def _mm(a, b, **kw):
    return jnp.dot(a, b, preferred_element_type=f32, **kw)


def _softplus(y):
    return jnp.maximum(y, 0.0) + jnp.log1p(jnp.exp(-jnp.abs(y)))


def _sigmoid(y):
    return 1.0 / (1.0 + jnp.exp(-y))


def _three_term_mm(a, x):
    hi = x.astype(bf16)
    r1 = x - hi.astype(f32)
    mid = r1.astype(bf16)
    lo = (r1 - mid.astype(f32)).astype(bf16)
    return _mm(a, hi) + _mm(a, mid) + _mm(a, lo)


def _pair_mm(x, bd):
    return jnp.concatenate([_mm(x[:, p:p + PAIR], bd) for p in range(0, x.shape[1], PAIR)], axis=1)


def _head_mean(x, bd):
    hi = x.astype(bf16)
    lo = (x - hi.astype(f32)).astype(bf16)
    return _pair_mm(hi, bd) + _pair_mm(lo, bd)


def _padded_row_specs(seq, tm, d):
    pieces = tm // ATTN_BLOCK
    last = seq // ATTN_BLOCK - 1

    def x_piece(k):
        return pl.BlockSpec((None, ATTN_BLOCK, d),
                            lambda b, j: (b, jnp.clip(j * pieces + k - 1, 0, last), 0))

    return [_const_spec((ATTN_BLOCK, d))] + [x_piece(k) for k in range(pieces)]


def _padded_rows(lead_ref, x_refs):
    first = jnp.where(pl.program_id(1) == 0, lead_ref[...], x_refs[0][...])
    return jnp.concatenate([first] + [r[...] for r in x_refs[1:]], axis=0)


def _proj_kernel(*refs):
    npiece = len(refs) - 13
    lead_ref, x_refs = refs[0], refs[1:1 + npiece]
    g_ref, w_ref, qn_ref, kn_ref, fb_ref, bd_ref, q_ref, k_ref, v_ref, lf_ref, zr_ref, gate_ref = refs[1 + npiece:]
    x = _padded_rows(lead_ref, x_refs)
    ms = jnp.mean(x * x, axis=-1, keepdims=True)
    xn = (x * lax.rsqrt(ms + NORM_EPS) * g_ref[...]).astype(bf16)

    def head_norm(z, gain):
        msq = _pair_mm((z * z).astype(bf16), bd_ref[...])
        return z * lax.rsqrt(msq + NORM_EPS) * gain

    zq = _mm(xn, w_ref[:, 0:WIDTH])
    q_ref[...] = (head_norm(zq, qn_ref[...]) * (HEAD_DIM ** -0.5 * LOG2_E)).astype(bf16)
    zk = _mm(xn, w_ref[:, WIDTH:2 * WIDTH])
    k_ref[...] = head_norm(zk, kn_ref[...]).astype(bf16)
    v_ref[...] = _mm(xn, w_ref[:, 2 * WIDTH:3 * WIDTH]).astype(bf16)
    zf = _mm(xn, w_ref[:, 3 * WIDTH:FOX_COLS]) + fb_ref[...]
    lf_ref[...] = -_softplus(-zf) * LOG2_E
    zr_ref[...] = _mm(xn, w_ref[:, FOX_COLS:FOX_COLS + RWKV_COLS])
    zg = _mm(xn, w_ref[:, FOX_COLS + RWKV_COLS:])
    gate_ref[...] = _sigmoid(zg).astype(bf16)


def _proj(lead, x, length, g, w, qn, kn, fb, bd):
    batch, seq, d = x.shape
    tm = _attn_block(length)
    nb = length // tm
    n = batch * length
    ncol = w.shape[1]
    ngate = ncol - FOX_COLS - RWKV_COLS
    row = lambda c: pl.BlockSpec((tm, c), lambda b, j: (b * nb + j, 0))
    x_specs = _padded_row_specs(seq, tm, d)
    return pl.pallas_call(
        _proj_kernel,
        grid=(batch, nb),
        in_specs=x_specs + [_const_spec((1, d)), _const_spec((d, ncol)), _const_spec((1, WIDTH)),
                            _const_spec((1, WIDTH)), _const_spec((1, LANES)), _const_spec((PAIR, PAIR))],
        out_specs=[row(WIDTH), row(WIDTH), row(WIDTH), row(LANES), row(RWKV_COLS), row(ngate)],
        out_shape=[jax.ShapeDtypeStruct((n, WIDTH), bf16)] * 3
        + [jax.ShapeDtypeStruct((n, LANES), f32), jax.ShapeDtypeStruct((n, RWKV_COLS), f32),
           jax.ShapeDtypeStruct((n, ngate), bf16)],
        compiler_params=_cparams("parallel", "parallel"),
        name="proj",
    )(lead, *([x] * (len(x_specs) - 1)), g, w, qn, kn, fb, bd)


def _cumsum_kernel(lf_ref, o_ref, *, pad):
    length = lf_ref.shape[0]
    r = lax.broadcasted_iota(i32, (LANES, LANES), 0)
    c = lax.broadcasted_iota(i32, (LANES, LANES), 1)
    tri = (r >= c).astype(f32)
    carry = jnp.zeros((1, LANES), f32)
    for blk in range(length // LANES):
        cs = _mm(tri, lf_ref[blk * LANES:(blk + 1) * LANES, :], precision=HIGHEST) + carry
        carry = cs[LANES - 1:LANES, :]
        neg = jnp.where(r + blk * LANES >= pad, -cs, MASK_VALUE)
        for h in range(N_HEADS):
            o_ref[h, blk * LANES:(blk + 1) * LANES, :] = jnp.broadcast_to(neg[:, h:h + 1], (LANES, LANES))


def _key_bias(lf, pad):
    b, length, _ = lf.shape
    return pl.pallas_call(
        functools.partial(_cumsum_kernel, pad=pad),
        grid=(b,),
        in_specs=[pl.BlockSpec((None, length, LANES), lambda i: (i, 0, 0))],
        out_specs=pl.BlockSpec((None, N_HEADS, length, LANES), lambda i: (i, 0, 0, 0)),
        out_shape=jax.ShapeDtypeStruct((b, N_HEADS, length, LANES), f32),
        compiler_params=_cparams("parallel"),
        name="cumsum",
    )(lf)


def _attn_kernel(q_ref, k_ref, kb_ref, v_ref, o_ref, *, ta):
    qi = pl.program_id(2)
    lane = lax.broadcasted_iota(i32, (1, PAIR), 1)
    key = lax.broadcasted_iota(i32, (ta, ATTN_BLOCK), 0)
    qry = lax.broadcasted_iota(i32, (ta, ATTN_BLOCK), 1)
    nrow = ta // ATTN_BLOCK
    pcs = [(h, r) for h in range(2) for r in range(nrow)]
    qa = []
    for h, r in pcs:
        qr = q_ref[r * ATTN_BLOCK:(r + 1) * ATTN_BLOCK, :]
        qa.append(jnp.where((lane // HEAD_DIM) == h, qr, jnp.zeros_like(qr)))

    def block(j, carries, diagonal):
        start = pl.multiple_of(j * ta, ta)
        ks = k_ref[pl.ds(start, ta), :]
        vt = v_ref[pl.ds(start, ta), :].T
        bias = [kb_ref[h, pl.ds(start, ta), :] for h in range(2)]
        ids = range(len(pcs))
        s = [_nt(ks, qa[i]) + bias[pcs[i][0]] for i in ids]
        if diagonal:
            s = [jnp.where(key <= qry + r * ATTN_BLOCK, s[i], MASK_VALUE) for i, (h, r) in enumerate(pcs)]
        m_new = [jnp.maximum(carries[i][0], jnp.max(s[i], axis=0, keepdims=True)) for i in ids]
        p = [jnp.exp2(s[i] - m_new[i]) for i in ids]
        alpha = [jnp.exp2(carries[i][0] - m_new[i]) for i in ids]
        l = [alpha[i] * carries[i][1] + jnp.sum(p[i], axis=0, keepdims=True) for i in ids]
        pv = [_mm(vt, p[i].astype(bf16)) for i in ids]
        acc = [alpha[i] * carries[i][2] + pv[i] for i in ids]
        return tuple((m_new[i], l[i], acc[i]) for i in ids)

    init = (jnp.full((1, ATTN_BLOCK), MASK_VALUE, f32), jnp.zeros((1, ATTN_BLOCK), f32),
            jnp.zeros((PAIR, ATTN_BLOCK), f32))
    carries = lax.fori_loop(0, qi // 2, lambda j, c: block(2 * j + 1, block(2 * j, c, False), False),
                            (init,) * len(pcs))
    carries = lax.cond(qi % 2 == 1, lambda c: block(qi - 1, c, False), lambda c: c, carries)
    final = block(qi, carries, True)
    vrow = lax.broadcasted_iota(i32, (PAIR, 1), 0)
    for r in range(nrow):
        o0 = final[r][2] / final[r][1]
        o1 = final[nrow + r][2] / final[nrow + r][1]
        o_ref[r * ATTN_BLOCK:(r + 1) * ATTN_BLOCK, :] = jnp.where(vrow < HEAD_DIM, o0, o1).T.astype(o_ref.dtype)


def _attention(q, k, kbias, v, ta):
    b, length, _ = q.shape
    nb = length // ta
    npair = N_HEADS // 2
    full = pl.BlockSpec((None, length, PAIR), lambda i, p, j: (i, 0, p))
    return pl.pallas_call(
        functools.partial(_attn_kernel, ta=ta),
        grid=(b, npair, nb),
        in_specs=[pl.BlockSpec((None, ta, PAIR), lambda i, p, j: (i, j, p)), full,
                  pl.BlockSpec((None, 2, length, LANES), lambda i, p, j: (i, p, 0, 0)), full],
        out_specs=pl.BlockSpec((None, ta, PAIR), lambda i, p, j: (i, j, p)),
        out_shape=jax.ShapeDtypeStruct((b, length, WIDTH), bf16),
        compiler_params=_cparams("parallel", "parallel", "arbitrary"),
        name="attn",
    )(q, k, kbias, v)


def _stack(x, lane_head):
    zero = jnp.zeros_like(x)
    return jnp.concatenate([jnp.where(lane_head == 0, x, zero), jnp.where(lane_head == 1, x, zero)], axis=0)


def _unit_lower_inverse(ns, r, c):
    eye = (r == c).astype(f32)
    first = ((r ^ c) == 1) & ((r & 1) == 1)
    ts = [(eye - jnp.where(first, n, 0.0)).astype(bf16) for n in ns]
    s = 2
    while s < CHUNK:
        sel = ((r // (2 * s)) == (c // (2 * s))) & ((r & s) != 0) & ((c & s) == 0)
        low = [jnp.where(sel, n, 0.0).astype(bf16) for n in ns]
        tn = [_mm(t, lo).astype(bf16) for t, lo in zip(ts, low)]
        ts = [(t.astype(f32) - _mm(x, t)).astype(bf16) for t, x in zip(ts, tn)]
        s *= 2
    return ts


def _rwkv_kernel(z_ref, mu_ref, w0_ref, a0_ref, kk_ref, ka_ref, rk_ref, lng_ref, lnb_ref,
                 wup_ref, aup_ref, gup_ref, bdm_ref, o_ref, s_ref, prev_ref):
    ci = pl.program_id(1)

    @pl.when(ci == 0)
    def _():
        s_ref[...] = jnp.zeros_like(s_ref)
        prev_ref[...] = jnp.zeros_like(prev_ref)

    z = z_ref[...]
    rows = lax.broadcasted_iota(i32, (RWKV_ROWS, 1), 0)
    zprev = jnp.where(rows == 0, prev_ref[...], pltpu.roll(z, 1, 0))
    prev_ref[...] = z[RWKV_ROWS - 1:RWKV_ROWS, :]
    zs = z + mu_ref[...] * (zprev - z)
    r = zs[:, 0:WIDTH]
    k = zs[:, WIDTH:2 * WIDTH]
    v = zs[:, 2 * WIDTH:3 * WIDTH]
    o1 = 3 * WIDTH
    wd = zs[:, o1:o1 + LANES]
    ad = zs[:, o1 + LANES:o1 + 2 * LANES]
    gd = zs[:, o1 + 2 * LANES:o1 + 4 * LANES]

    w = -_softplus(-(w0_ref[...] + _mm(jnp.tanh(wd).astype(bf16), wup_ref[...]))) - 0.5
    ld = -jnp.exp(w)
    a = _sigmoid(a0_ref[...] + _mm(ad.astype(bf16), aup_ref[...]))
    g = _mm(_sigmoid(gd).astype(bf16), gup_ref[...])
    bdm = bdm_ref[...]
    kk = k * kk_ref[...]
    nrm = jnp.sqrt(_pair_mm((kk * kk).astype(bf16), bdm) * HEAD_DIM)
    kk = kk / jnp.maximum(nrm, 1e-12)
    k2 = k * (1.0 + (a - 1.0) * ka_ref[...])
    b = kk * a

    tr = lax.broadcasted_iota(i32, (CHUNK, CHUNK), 0)
    tc = lax.broadcasted_iota(i32, (CHUNK, CHUNK), 1)
    tri = (tr >= tc).astype(bf16)
    chunks = range(RWKV_ROWS // CHUNK)
    lc = jnp.concatenate([_three_term_mm(tri, ld[c * CHUNK:(c + 1) * CHUNK, :]) for c in chunks], axis=0)
    lp = lc - ld
    mids = [lc[c * CHUNK + CHUNK // 2 - 1:c * CHUNK + CHUNK // 2, :] for c in chunks]
    tots = [lc[(c + 1) * CHUNK - 1:(c + 1) * CHUNK, :] for c in chunks]
    mid, tot = mids[-1], tots[-1]
    for c in reversed(chunks[:-1]):
        mid = jnp.where(rows < (c + 1) * CHUNK, mids[c], mid)
        tot = jnp.where(rows < (c + 1) * CHUNK, tots[c], tot)
    kq = kk * jnp.exp(lp - mid)
    rr = r * jnp.exp(lc - mid)
    e_after = jnp.exp(mid - lc)
    kh = k2 * e_after
    bh = b * e_after
    e_end = jnp.exp(tot - lc)
    kc = k2 * e_end
    bc = b * e_end
    e_mid = [jnp.exp(m) for m in mids]
    p_end = [jnp.exp(t) for t in tots]

    sr = lax.broadcasted_iota(i32, (PAIR, PAIR), 0)
    sc = lax.broadcasted_iota(i32, (PAIR, PAIR), 1)
    same = (sr // CHUNK) == (sc // CHUNK)
    strict = same & ((sr % CHUNK) > (sc % CHUNK))
    incl = same & ((sr % CHUNK) >= (sc % CHUNK))
    lane_head = lax.broadcasted_iota(i32, (1, PAIR), 1) // HEAD_DIM

    npair = N_HEADS // 2
    units = [(c, p) for c in chunks for p in range(npair)]
    ids = range(len(units))
    lanes = [slice(p * PAIR, (p + 1) * PAIR) for _, p in units]
    st = lambda x: [_stack(x[c * CHUNK:(c + 1) * CHUNK, p * PAIR:(p + 1) * PAIR], lane_head) for c, p in units]
    kq_s, rr_s, kh_s, bh_s, v_s, kc_s, bc_s = (st(x) for x in (kq, rr, kh, bh, v, kc, bc))
    amat = [_nt(jnp.concatenate([kq_s[i], rr_s[i]], axis=0).astype(bf16),
                jnp.concatenate([kh_s[i], bh_s[i]], axis=0).astype(bf16)) for i in ids]
    a_kk = [jnp.where(strict, amat[i][0:PAIR, 0:PAIR], 0.0).astype(bf16) for i in ids]
    a_kb = [jnp.where(strict, amat[i][0:PAIR, PAIR:], 0.0) for i in ids]
    a_rk = [jnp.where(incl, amat[i][PAIR:, 0:PAIR], 0.0).astype(bf16) for i in ids]
    a_rb = [jnp.where(incl, amat[i][PAIR:, PAIR:], 0.0).astype(bf16) for i in ids]
    t = _unit_lower_inverse(a_kb, sr, sc)
    vb = [v_s[i].astype(bf16) for i in ids]
    av = [_mm(a_kk[i], vb[i]).astype(bf16) for i in ids]
    wm = [_mm(t[i], kq_s[i].astype(bf16)) for i in ids]
    uv = [_mm(t[i], av[i]) for i in ids]
    rq = [rr_s[i] - _mm(a_rb[i], wm[i].astype(bf16)) for i in ids]
    yv = [_mm(a_rk[i], vb[i]) - _mm(a_rb[i], uv[i].astype(bf16)) for i in ids]
    wm_t = [(wm[i] * e_mid[units[i][0]][:, lanes[i]]).astype(bf16) for i in ids]
    rq_t = [(rq[i] * e_mid[units[i][0]][:, lanes[i]]).astype(bf16) for i in ids]
    bcb = [bc_s[i].astype(bf16) for i in ids]
    omega = [_tn(wm_t[i], bcb[i]).astype(bf16) for i in ids]
    psi = [_tn(vb[i], kc_s[i].astype(bf16)) - _tn(uv[i].astype(bf16), bcb[i]) for i in ids]
    state = [s_ref[p] for p in range(npair)]
    ystack = []
    for c in chunks:
        new = []
        for p in range(npair):
            i = c * npair + p
            s_hi = state[p].astype(bf16)
            s_lo = (state[p] - s_hi.astype(f32)).astype(bf16)
            ystack.append(_nt(rq_t[i], s_hi) + yv[i])
            new.append(state[p] * p_end[c][:, lanes[i]] - (_mm(s_hi, omega[i]) + _mm(s_lo, omega[i])) + psi[i])
        state = new
    for p in range(npair):
        s_ref[p] = state[p]
    y = jnp.concatenate(
        [jnp.concatenate([ystack[c * npair + p][0:CHUNK, :] + ystack[c * npair + p][CHUNK:, :] for p in range(npair)],
                         axis=1) for c in chunks], axis=0)

    mean = _head_mean(y, bdm)
    yc = y - mean
    var = _head_mean(yc * yc, bdm)
    yn = yc * lax.rsqrt(var + RWKV_GN_EPS) * lng_ref[...] + lnb_ref[...]
    bonus = _pair_mm((r * k2 * rk_ref[...]).astype(bf16), bdm) * HEAD_DIM * v
    o_ref[...] = ((yn + bonus) * g).astype(o_ref.dtype)


def _rwkv(zr, mu, w0, a0, k_k, k_a, r_k, ln_g, ln_b, w_up, a_up, g_up, bdm):
    b, length, _ = zr.shape
    assert length % RWKV_ROWS == 0, (length, RWKV_ROWS)
    vec = _const_spec((1, WIDTH))
    return pl.pallas_call(
        _rwkv_kernel,
        grid=(b, length // RWKV_ROWS),
        in_specs=[pl.BlockSpec((None, RWKV_ROWS, RWKV_COLS), lambda i, c: (i, c, 0)),
                  _const_spec((1, RWKV_COLS)), vec, vec, vec, vec, vec, vec, vec,
                  _const_spec((LANES, WIDTH)), _const_spec((LANES, WIDTH)), _const_spec((2 * LANES, WIDTH)),
                  _const_spec((PAIR, PAIR))],
        out_specs=pl.BlockSpec((None, RWKV_ROWS, WIDTH), lambda i, c: (i, c, 0)),
        out_shape=jax.ShapeDtypeStruct((b, length, WIDTH), bf16),
        scratch_shapes=[pltpu.VMEM((N_HEADS // 2, PAIR, PAIR), f32), pltpu.VMEM((1, RWKV_COLS), f32)],
        compiler_params=_cparams("parallel", "arbitrary"),
        name="rwkv",
    )(zr, mu, w0, a0, k_k, k_a, r_k, ln_g, ln_b, w_up, a_up, g_up, bdm)


def _merge_kernel(*refs):
    npiece = len(refs) - 8
    lead_ref, x_refs = refs[0], refs[1:1 + npiece]
    yf_ref, yr_ref, gate_ref, wf_ref, wr_ref, wo_ref, o_ref = refs[1 + npiece:]
    d = o_ref.shape[1]
    pf = _mm(yf_ref[...], wf_ref[...])
    pr = _mm(yr_ref[...], wr_ref[...])
    mixed = gate_ref[:, 0:d].astype(f32) * pf + gate_ref[:, d:2 * d].astype(f32) * pr
    o_ref[...] = _padded_rows(lead_ref, x_refs) + _mm(mixed.astype(bf16), wo_ref[...])


def _merge(lead, x, length, yf, yr, gates, wf, wr, wo):
    batch, seq, d = x.shape
    tm = _attn_block(length)
    nb = length // tm
    row = lambda c: pl.BlockSpec((tm, c), lambda b, j: (b * nb + j, 0))
    x_specs = _padded_row_specs(seq, tm, d)
    return pl.pallas_call(
        _merge_kernel,
        grid=(batch, nb),
        in_specs=x_specs + [row(WIDTH), row(WIDTH), row(2 * d),
                            _const_spec((WIDTH, d)), _const_spec((WIDTH, d)), _const_spec((d, d))],
        out_specs=row(d),
        out_shape=jax.ShapeDtypeStruct((batch * length, d), f32),
        compiler_params=_cparams("parallel", "parallel"),
        name="merge",
    )(lead, *([x] * (len(x_specs) - 1)), yf, yr, gates, wf, wr, wo)


def _top_rows(s, k):
    n = s.shape[0]
    pos = lax.broadcasted_iota(i32, s.shape, 0).astype(f32)
    vals, outs = [], []
    for _ in range(k):
        m = jnp.max(s, axis=0, keepdims=True)
        first = jnp.min(jnp.where(s == m, pos, float(n)), axis=0, keepdims=True)
        vals.append(m)
        outs.append(first)
        s = jnp.where(pos == first, -jnp.inf, s)
    return jnp.concatenate(vals, axis=0), jnp.concatenate(outs, axis=0)


def _staircase():
    pairs = [(a, b) for a in range(PEER_TOPK) for b in range(PEER_TOPK) if (a + 1) * (b + 1) <= PEER_TOPK]
    rows = -(-len(pairs) // 8) * 8
    sel = jnp.zeros((2, rows, PEER_TOPK), f32)
    r = jnp.arange(len(pairs))
    sel = sel.at[0, r, jnp.array([a for a, _ in pairs])].set(1.0)
    sel = sel.at[1, r, jnp.array([b for _, b in pairs])].set(1.0)
    return sel, len(pairs)


def _peer_sel_kernel(h_ref, g_ref, wq_ref, k1_ref, k2_ref, sel_ref, xn_ref, idx_ref, gate_ref, *, n_cand):
    x = h_ref[...]
    ms = jnp.mean(x * x, axis=-1, keepdims=True)
    xn = x * lax.rsqrt(ms + NORM_EPS) * g_ref[...]
    xn_ref[...] = xn.reshape(xn_ref.shape)
    qt = _nt(wq_ref[...], xn.astype(bf16))
    half = PEER_QDIM // 2
    sel_a, sel_b = sel_ref[0], sel_ref[1]
    pick = lambda sel, t: _mm(sel, t, precision=HIGHEST)
    s1 = jnp.concatenate([_mm(k1_ref[...], qt[h * PEER_QDIM:h * PEER_QDIM + half, :].astype(bf16))
                          for h in range(PEER_HEADS)], axis=1)
    s2 = jnp.concatenate([_mm(k2_ref[...], qt[h * PEER_QDIM + half:(h + 1) * PEER_QDIM, :].astype(bf16))
                          for h in range(PEER_HEADS)], axis=1)
    t1, i1 = _top_rows(s1, PEER_TOPK)
    t2, i2 = _top_rows(s2, PEER_TOPK)
    rows = lax.broadcasted_iota(i32, (sel_a.shape[0], PEER_HEADS * PEER_TOK), 0).astype(f32)
    cand_s = jnp.where(rows < n_cand, pick(sel_a, t1) + pick(sel_b, t2), -jnp.inf)
    cand_i = pick(sel_a, i1) * N_KEYS + pick(sel_b, i2)
    top_s, top_p = _top_rows(cand_s, PEER_TOPK)
    top_i = jnp.concatenate(
        [jnp.max(jnp.where(rows == top_p[j:j + 1, :], cand_i, -1.0), axis=0, keepdims=True)
         for j in range(PEER_TOPK)], axis=0).astype(i32)
    e = jnp.exp(top_s - top_s[0:1, :])
    gate = e / jnp.sum(e, axis=0, keepdims=True)
    by_head = lambda a: jnp.concatenate([a[:, h * PEER_TOK:(h + 1) * PEER_TOK] for h in range(PEER_HEADS)], axis=0)
    idx_ref[...] = by_head(top_i).T * TABLE_ROWS
    gate_ref[...] = by_head(gate).T


def _peer_select(h2, g, wq_t, k1, k2, batch, seq, length):
    d = h2.shape[-1]
    nt = seq // PEER_TOK
    skip = (length - seq) // PEER_TOK
    nblk = batch * nt
    sel, n_cand = _staircase()
    blk = pl.BlockSpec((None, PEER_TOK, PEER_PAIRS), lambda i, j: (i * nt + j, 0, 0))
    return pl.pallas_call(
        functools.partial(_peer_sel_kernel, n_cand=n_cand),
        grid=(batch, nt),
        in_specs=[pl.BlockSpec((None, PEER_TOK, d), lambda i, j: (i, j + skip, 0)),
                  _const_spec((1, d)), _const_spec(wq_t.shape), _const_spec(k1.shape), _const_spec(k2.shape),
                  _const_spec(sel.shape)],
        out_specs=[pl.BlockSpec((PEER_TOK, d // LANES, LANES), lambda i, j: (i * nt + j, 0, 0)), blk, blk],
        out_shape=[jax.ShapeDtypeStruct((nblk * PEER_TOK, d // LANES, LANES), f32),
                   jax.ShapeDtypeStruct((nblk, PEER_TOK, PEER_PAIRS), i32),
                   jax.ShapeDtypeStruct((nblk, PEER_TOK, PEER_PAIRS), f32)],
        compiler_params=_cparams("parallel", "parallel"),
        name="peer_sel",
    )(h2.reshape(batch, length, d), g, wq_t, k1, k2, sel)


def _pack_table(tab):
    e, d = tab.shape
    assert d == 2 * TABLE_ROWS * LANES
    tm = _row_block(e)
    return pl.pallas_call(
        _pack_kernel,
        grid=(e // tm,),
        in_specs=[pl.BlockSpec((tm, d), lambda i: (i, 0))],
        out_specs=pl.BlockSpec((tm * TABLE_ROWS, LANES), lambda i: (i, 0)),
        out_shape=jax.ShapeDtypeStruct((e * TABLE_ROWS, LANES), i32),
        compiler_params=_cparams("parallel"),
        name="pack",
    )(tab)


def _pack_kernel(x_ref, o_ref):
    half = x_ref.shape[1] // 2
    bits = lax.bitcast_convert_type(x_ref[...].astype(bf16).astype(f32), i32)
    words = bits[:, 0:half] | lax.shift_right_logical(bits[:, half:], 16)
    o_ref[...] = words.reshape(o_ref.shape)


def _unpack(word):
    return (lax.bitcast_convert_type(word & jnp.int32(-65536), f32), lax.bitcast_convert_type(word << 16, f32))


def _expert_rows(idx_ref, tab_ref, t, k):
    return tab_ref[pl.ds(pl.multiple_of(idx_ref[t, k], TABLE_ROWS), TABLE_ROWS), :]


def _gather(idx_ref, tab_ref, stage_ref, t):
    for k in range(PEER_PAIRS):
        stage_ref[TABLE_ROWS * k:TABLE_ROWS * (k + 1), :] = _expert_rows(idx_ref, tab_ref, t, k)


def _staged(stage_ref, s):
    return _unpack(stage_ref[pl.ds(s, PEER_PAIRS, stride=TABLE_ROWS), :])


def _token_pipeline(idx_ref, tab_ref, stage, consume):
    ahead = len(stage)
    for t in range(ahead):
        _gather(idx_ref, tab_ref, stage[t], t)
    for t in range(PEER_TOK):
        consume(t, stage[t % ahead])
        if t + ahead < PEER_TOK:
            _gather(idx_ref, tab_ref, stage[t % ahead], t + ahead)


def _peer_hid_kernel(idx_ref, x_ref, gate_ref, tab_ref, o_ref, *scratch):
    stage, dots_ref = scratch[:-1], scratch[-1]
    row = lax.broadcasted_iota(i32, (16, PEER_PAIRS), 0) % 8

    def dots(t, buf):
        xt = x_ref[t]
        head = xt.astype(bf16).astype(f32)
        lhs = jnp.concatenate([head, xt - head], axis=0).astype(bf16)
        acc = jnp.zeros((16, PEER_PAIRS), f32)
        for s in range(TABLE_ROWS):
            hi, lo = _staged(buf, s)
            acc = (acc + jnp.where(row == s, _nt(lhs, hi.astype(bf16)), 0.0)
                   + jnp.where(row == TABLE_ROWS + s, _nt(lhs, lo.astype(bf16)), 0.0))
        dots_ref[t:t + 1, :] = jnp.sum(acc, axis=0, keepdims=True)

    _token_pipeline(idx_ref, tab_ref, stage, dots)
    hid = dots_ref[...]
    o_ref[...] = gate_ref[...] * (0.5 * hid * (1.0 + lax.erf(hid * (2.0 ** -0.5))))


def _two_term_rows(row):
    hi = row.astype(bf16).astype(f32)
    r = lax.broadcasted_iota(i32, (8, LANES), 0)
    return jnp.where(r == 0, hi, jnp.where(r == 1, row - hi, 0.0)).astype(bf16)


def _peer_out_kernel(idx_ref, c_ref, h_ref, tab_ref, o_ref, *scratch):
    stage, acc_ref = scratch[:-1], scratch[-1]
    row = lax.broadcasted_iota(i32, (8, LANES), 0)
    acc_ref[...] = h_ref[...].reshape(acc_ref.shape)

    def combine(t, buf):
        lhs = _two_term_rows(c_ref[t:t + 1, :])
        out = acc_ref[t]
        for s in range(TABLE_ROWS):
            hi, lo = _staged(buf, s)
            a = _mm(lhs, hi.astype(bf16))
            b = _mm(lhs, lo.astype(bf16))
            out = (out + jnp.where(row == s, a[0:1, :] + a[1:2, :], 0.0)
                   + jnp.where(row == TABLE_ROWS + s, b[0:1, :] + b[1:2, :], 0.0))
        acc_ref[t] = out

    _token_pipeline(idx_ref, tab_ref, stage, combine)
    o_ref[...] = acc_ref[...].reshape(o_ref.shape)


def _smem_blk(index_map):
    return pl.BlockSpec((None, PEER_TOK, PEER_PAIRS), index_map, memory_space=pltpu.SMEM,
                        pipeline_mode=pl.Buffered(1))


def _table_spec(tab):
    return pl.BlockSpec(tab.shape, lambda *_: (0, 0), pipeline_mode=pl.Buffered(1))


def _peer_hidden(idx, x3, gate, utab):
    nblk = idx.shape[0]
    sub = x3.shape[1]
    vblk = pl.BlockSpec((None, PEER_TOK, PEER_PAIRS), lambda i: (i, 0, 0))
    rows = TABLE_ROWS * PEER_PAIRS
    return pl.pallas_call(
        _peer_hid_kernel,
        grid=(nblk,),
        in_specs=[_smem_blk(lambda i: (i, 0, 0)), pl.BlockSpec((PEER_TOK, sub, LANES), lambda i: (i, 0, 0)), vblk,
                  _table_spec(utab)],
        out_specs=vblk,
        out_shape=jax.ShapeDtypeStruct((nblk, PEER_TOK, PEER_PAIRS), f32),
        scratch_shapes=[pltpu.VMEM((rows, LANES), i32)] * PEER_STAGES + [pltpu.VMEM((PEER_TOK, PEER_PAIRS), f32)],
        compiler_params=_cparams("arbitrary"),
        name="peer_hid",
    )(idx, x3, gate, utab)


def _peer_output(idx, coef, h2, vtab, batch, seq, length):
    d = h2.shape[-1]
    sub = d // LANES
    nt = seq // PEER_TOK
    skip = (length - seq) // PEER_TOK
    rows = TABLE_ROWS * PEER_PAIRS
    blk = lambda i, j: (i * nt + j, 0, 0)
    return pl.pallas_call(
        _peer_out_kernel,
        grid=(batch, nt),
        in_specs=[_smem_blk(blk), pl.BlockSpec((None, PEER_TOK, PEER_PAIRS), blk),
                  pl.BlockSpec((None, PEER_TOK, d), lambda i, j: (i, j + skip, 0)),
                  _table_spec(vtab)],
        out_specs=pl.BlockSpec((None, PEER_TOK, d), lambda i, j: (i, j, 0)),
        out_shape=jax.ShapeDtypeStruct((batch, seq, d), f32),
        scratch_shapes=[pltpu.VMEM((rows, LANES), i32)] * PEER_STAGES + [pltpu.VMEM((PEER_TOK, sub, LANES), f32)],
        compiler_params=_cparams("arbitrary", "arbitrary"),
        name="peer_out",
    )(idx, coef, h2.reshape(batch, length, d), vtab)


def _pad_cols(w, width):
    return jnp.pad(w, ((0, 0), (0, width - w.shape[1])))


def _pad_rows(w, height):
    return jnp.pad(w, ((0, height - w.shape[0]), (0, 0)))


def kernel(x, meta_tokens, norm1_g, w_in, fox_q_norm, fox_k_norm, fox_f_bias, rwkv_mu, rwkv_w0, rwkv_w_up, rwkv_a0, rwkv_a_up, rwkv_g_up, rwkv_k_k, rwkv_k_a, rwkv_r_k, rwkv_ln_g, rwkv_ln_b, w_branch_fox, w_branch_rwkv, w_out, norm2_g, peer_w_q, peer_sub_k1, peer_sub_k2, peer_u, peer_v):
    batch, seq, d = x.shape
    assert w_in.shape[0] == 1, "one layer"
    assert seq % PEER_TOK == 0 and d % LANES == 0
    t_real = seq + N_META
    pad = (-t_real) % ATTN_BLOCK
    length = t_real + pad
    ta = _attn_block(length)

    assert length - seq == ATTN_BLOCK
    lead = jnp.pad(meta_tokens.astype(x.dtype), ((pad, 0), (0, 0)))
    w = w_in[0]
    fox_w = 3 * WIDTH + N_HEADS
    o = fox_w
    lora = lambda lo, n, width: _pad_cols(w[:, lo:lo + n], width)
    w_all = jnp.concatenate([
        w[:, 0:3 * WIDTH], _pad_cols(w[:, 3 * WIDTH:fox_w], LANES),
        w[:, o:o + 3 * WIDTH],
        lora(o + 3 * WIDTH, DECAY_LORA, LANES),
        lora(o + 3 * WIDTH + DECAY_LORA, AAA_LORA, LANES),
        lora(o + 3 * WIDTH + DECAY_LORA + AAA_LORA, GATE_LORA, 2 * LANES),
        w[:, o + 3 * WIDTH + DECAY_LORA + AAA_LORA + GATE_LORA:],
    ], axis=1).astype(bf16)
    mu = rwkv_mu[0]
    mu_all = jnp.concatenate([
        mu[0:3 * WIDTH], jnp.pad(mu[3 * WIDTH:3 * WIDTH + DECAY_LORA], (0, LANES - DECAY_LORA)),
        jnp.pad(mu[3 * WIDTH + DECAY_LORA:3 * WIDTH + DECAY_LORA + AAA_LORA], (0, LANES - AAA_LORA)),
        jnp.pad(mu[3 * WIDTH + DECAY_LORA + AAA_LORA:], (0, 2 * LANES - GATE_LORA)),
    ])[None]
    head_of = jnp.arange(PAIR) // HEAD_DIM
    bdm = ((head_of[:, None] == head_of[None, :]).astype(f32) / HEAD_DIM).astype(bf16)
    vec = lambda p: p[0].reshape(1, -1)

    q, k, v, lf, zr, gates = _proj(lead, x, length, vec(norm1_g), w_all, vec(fox_q_norm), vec(fox_k_norm),
                                   jnp.pad(fox_f_bias[0], (0, LANES - N_HEADS))[None], bdm)
    three = lambda a: a.reshape(batch, length, a.shape[-1])
    y_fox = _attention(three(q), three(k), _key_bias(three(lf), pad), three(v), ta)
    y_rwkv = _rwkv(three(zr), mu_all, vec(rwkv_w0), vec(rwkv_a0), vec(rwkv_k_k), vec(rwkv_k_a), vec(rwkv_r_k),
                   vec(rwkv_ln_g), vec(rwkv_ln_b),
                   _pad_rows(rwkv_w_up[0], LANES).astype(bf16), _pad_rows(rwkv_a_up[0], LANES).astype(bf16),
                   _pad_rows(rwkv_g_up[0], 2 * LANES).astype(bf16), bdm)
    h2 = _merge(lead, x, length, y_fox.reshape(-1, WIDTH), y_rwkv.reshape(-1, WIDTH), gates,
                w_branch_fox[0].astype(bf16), w_branch_rwkv[0].astype(bf16), w_out[0].astype(bf16))

    xn2, idx, gate = _peer_select(h2, vec(norm2_g), peer_w_q[0].T.astype(bf16),
                                  peer_sub_k1[0].astype(bf16), peer_sub_k2[0].astype(bf16), batch, seq, length)
    coef = _peer_hidden(idx, xn2, gate, _pack_table(peer_u[0]))
    out = _peer_output(idx, coef, h2, _pack_table(peer_v[0]), batch, seq, length)
    return out.reshape(batch, seq, d)
```
